```python
import jax, jax.numpy as jnp
from jax import lax
import numpy as np

D_MODEL = 1024
BATCH = 8
SEQ = 4096
DEPTH = 4

N_BRANCH = 4
BRANCH_WIDTH = 512
EPS = 1e-6
Q_BLOCK = 128
GM_GROUPS = 4
GM_CHUNK = 128
GM_GROUP_DIM = BRANCH_WIDTH // GM_GROUPS
DSA_HEADS = 4
DSA_HEAD_DIM = BRANCH_WIDTH // DSA_HEADS
DSA_LATENT = 128
IDX_HEADS = 4
IDX_DIM = 64
TOPK_MAX = 256
CONV_WIDTH = 3
FOX_HEADS = 4
FOX_HEAD_DIM = BRANCH_WIDTH // FOX_HEADS

W = BRANCH_WIDTH
IN_SPLITS = (
    W, W, W,
    W, DSA_LATENT, IDX_HEADS * IDX_DIM, IDX_DIM, IDX_HEADS, W,
    W, W, W, W,
    W, W, W, FOX_HEADS, W,
    N_BRANCH * D_MODEL,
)
IN_WIDTH = sum(IN_SPLITS)

kernel_name = "hybrid_parallel_gated_mixers"


def rms_norm(x, g):
    xf = x.astype(jnp.float32)
    y = xf * lax.rsqrt(jnp.mean(xf * xf, axis=-1, keepdims=True) + EPS)
    return (y * g.astype(jnp.float32)).astype(x.dtype)


def layer_norm(x, g, b):
    xf = x.astype(jnp.float32)
    mu = jnp.mean(xf, axis=-1, keepdims=True)
    var = jnp.mean(jnp.square(xf - mu), axis=-1, keepdims=True)
    y = (xf - mu) * lax.rsqrt(var + EPS)
    return (y * g.astype(jnp.float32) + b.astype(jnp.float32)).astype(x.dtype)


def to_blocks(a):
    b, s = a.shape[0], a.shape[1]
    return jnp.moveaxis(a.reshape(b, s // Q_BLOCK, Q_BLOCK, *a.shape[2:]), 1, 0)


def from_blocks(a):
    nb, b = a.shape[0], a.shape[1]
    return jnp.moveaxis(a, 0, 1).reshape(b, nb * Q_BLOCK, a.shape[-1])


def chunked_spatial_gating(u, v, ln_g, ln_b, w_s, b_s):
    bsz, s, _ = v.shape
    v = layer_norm(v, ln_g, ln_b)
    vc = v.reshape(bsz, s // GM_CHUNK, GM_CHUNK, GM_GROUPS, GM_GROUP_DIM)
    mask = jnp.tril(jnp.ones((GM_CHUNK, GM_CHUNK), dtype=bool))
    w = jnp.where(mask[None], w_s, jnp.zeros_like(w_s))
    mixed = jnp.einsum('gts,bcsge->bctge', w, vc) + jnp.transpose(b_s)[None, None, :, :, None]
    return u * mixed.reshape(bsz, s, W)


def dsa_attention(q, c_kv, q_idx, k_idx, w_idx, kv_g, w_uk, w_uv):
    bsz, s, _ = q.shape
    k_sel = min(TOPK_MAX, s // 4)
    c = rms_norm(c_kv, kv_g)
    qh = q.reshape(bsz, s, DSA_HEADS, DSA_HEAD_DIM)
    q_lat = jnp.einsum('bshd,hld->bshl', qh, w_uk)
    qi = q_idx.reshape(bsz, s, IDX_HEADS, IDX_DIM)
    wi = w_idx * (IDX_HEADS ** -0.5)
    key_pos = jnp.arange(s)
    gather = jax.vmap(lambda cb, ib: cb[ib])

    def block(args):
        qlb, qib, wib, blk = args
        pos_q = blk * Q_BLOCK + jnp.arange(Q_BLOCK)
        dots = jnp.einsum('bqhd,bsd->bqhs', qib, k_idx).astype(jnp.float32) * (IDX_DIM ** -0.5)
        score = jnp.einsum('bqh,bqhs->bqs', wib.astype(jnp.float32), jax.nn.relu(dots))
        causal = key_pos[None, :] <= pos_q[:, None]
        score = jnp.where(causal[None], score, -jnp.inf)
        _, idx = lax.top_k(score, k_sel)
        valid = idx <= pos_q[None, :, None]
        c_sel = gather(c, idx)
        logits = jnp.einsum('bqhl,bqkl->bqhk', qlb, c_sel).astype(jnp.float32) * (DSA_HEAD_DIM ** -0.5)
        logits = jnp.where(valid[:, :, None, :], logits, -jnp.inf)
        p = jax.nn.softmax(logits, axis=-1).astype(c.dtype)
        o_lat = jnp.einsum('bqhk,bqkl->bqhl', p, c_sel)
        o = jnp.einsum('bqhl,hld->bqhd', o_lat, w_uv)
        return o.reshape(bsz, Q_BLOCK, W)

    nb = s // Q_BLOCK
    out = lax.map(block, (to_blocks(q_lat), to_blocks(qi), to_blocks(wi), jnp.arange(nb)))
    return from_blocks(out)


def short_gated_conv(b_gate, c_gate, x_in, conv_w):
    s = x_in.shape[1]
    y = c_gate * x_in
    yp = jnp.pad(y, ((0, 0), (CONV_WIDTH - 1, 0), (0, 0)))
    conv = conv_w[0] * yp[:, 0:s]
    for j in range(1, CONV_WIDTH):
        conv = conv + conv_w[j] * yp[:, j:j + s]
    return b_gate * conv


def forgetting_attention(q, k, v, f_logit, b_f):
    bsz, s, _ = q.shape
    qh = q.reshape(bsz, s, FOX_HEADS, FOX_HEAD_DIM)
    kh = k.reshape(bsz, s, FOX_HEADS, FOX_HEAD_DIM)
    vh = v.reshape(bsz, s, FOX_HEADS, FOX_HEAD_DIM)
    cum = jnp.cumsum(jax.nn.log_sigmoid((f_logit + b_f).astype(jnp.float32)), axis=1)
    cum_k = jnp.transpose(cum, (0, 2, 1))
    key_pos = jnp.arange(s)

    def block(args):
        qb, cq, blk = args
        pos_q = blk * Q_BLOCK + jnp.arange(Q_BLOCK)
        logits = jnp.einsum('bqhd,bshd->bhqs', qb, kh).astype(jnp.float32) * (FOX_HEAD_DIM ** -0.5)
        logits = logits + jnp.transpose(cq, (0, 2, 1))[..., None] - cum_k[:, :, None, :]
        logits = jnp.where((key_pos[None, :] <= pos_q[:, None])[None, None], logits, -jnp.inf)
        p = jax.nn.softmax(logits, axis=-1).astype(vh.dtype)
        o = jnp.einsum('bhqs,bshd->bqhd', p, vh)
        return o.reshape(bsz, Q_BLOCK, W)

    nb = s // Q_BLOCK
    out = lax.map(block, (to_blocks(qh), to_blocks(cum), jnp.arange(nb)))
    return from_blocks(out)


def hybrid_layer(x, norm_g, w_in, gm_ln_g, gm_ln_b, gm_w_s, gm_b_s, dsa_kv_g, dsa_w_uk, dsa_w_uv,
                 conv_w, fox_b_f, w_branch, w_out):
    bsz, s, d = x.shape
    h = rms_norm(x, norm_g)
    proj = h @ w_in
    points = [int(p) for p in np.cumsum(np.array(IN_SPLITS))[:-1]]
    (a_u, a_v, a_z,
     b_q, b_c, b_qi, b_ki, b_wi, b_z,
     c_b, c_c, c_x, c_z,
     d_q, d_k, d_v, d_f, d_z,
     gates) = jnp.split(proj, points, axis=-1)
    y_a = chunked_spatial_gating(a_u, a_v, gm_ln_g, gm_ln_b, gm_w_s, gm_b_s) * jax.nn.silu(a_z)
    y_b = dsa_attention(b_q, b_c, b_qi, b_ki, b_wi, dsa_kv_g, dsa_w_uk, dsa_w_uv) * jax.nn.silu(b_z)
    y_c = short_gated_conv(c_b, c_c, c_x, conv_w) * jax.nn.silu(c_z)
    y_d = forgetting_attention(d_q, d_k, d_v, d_f, fox_b_f) * jax.nn.silu(d_z)
    ys = jnp.stack([y_a, y_b, y_c, y_d], axis=0)
    branch_d = jnp.einsum('nbsw,nwd->bsnd', ys, w_branch)
    g = jax.nn.sigmoid(gates.reshape(bsz, s, N_BRANCH, d))
    merged = jnp.sum(g * branch_d, axis=2)
    return x + merged @ w_out


def setup_inputs(seed: int = 0) -> dict:
    key = jax.random.key(seed)
    ks = jax.random.split(key, 16)
    f32 = jnp.float32
    nrm = lambda k, shp, sc: jax.random.normal(k, shp, f32) * sc
    return {
        'x': nrm(ks[0], (BATCH, SEQ, D_MODEL), 1.0),
        'norm_g': 1.0 + nrm(ks[1], (DEPTH, D_MODEL), 0.05),
        'w_in': nrm(ks[2], (DEPTH, D_MODEL, IN_WIDTH), D_MODEL ** -0.5),
        'gm_ln_g': 1.0 + nrm(ks[3], (DEPTH, W), 0.05),
        'gm_ln_b': nrm(ks[4], (DEPTH, W), 0.02),
        'gm_w_s': nrm(ks[5], (DEPTH, GM_GROUPS, GM_CHUNK, GM_CHUNK), GM_CHUNK ** -0.5),
        'gm_b_s': 1.0 + nrm(ks[6], (DEPTH, GM_GROUPS, GM_CHUNK), 0.1),
        'dsa_kv_g': 1.0 + nrm(ks[7], (DEPTH, DSA_LATENT), 0.05),
        'dsa_w_uk': nrm(ks[8], (DEPTH, DSA_HEADS, DSA_LATENT, DSA_HEAD_DIM), DSA_LATENT ** -0.5),
        'dsa_w_uv': nrm(ks[9], (DEPTH, DSA_HEADS, DSA_LATENT, DSA_HEAD_DIM), DSA_LATENT ** -0.5),
        'conv_w': nrm(ks[10], (DEPTH, CONV_WIDTH, W), CONV_WIDTH ** -0.5),
        'fox_b_f': jax.random.uniform(ks[11], (DEPTH, FOX_HEADS), f32, minval=1.0, maxval=4.0),
        'w_branch': nrm(ks[12], (DEPTH, N_BRANCH, W, D_MODEL), W ** -0.5),
        'w_out': nrm(ks[13], (DEPTH, D_MODEL, D_MODEL), 0.5 * D_MODEL ** -0.5),
        'final_g': 1.0 + nrm(ks[14], (D_MODEL,), 0.05),
    }


def reference(x, norm_g, w_in, gm_ln_g, gm_ln_b, gm_w_s, gm_b_s, dsa_kv_g, dsa_w_uk, dsa_w_uv,
              conv_w, fox_b_f, w_branch, w_out, final_g):
    h = x
    for l in range(DEPTH):
        h = hybrid_layer(h, norm_g[l], w_in[l], gm_ln_g[l], gm_ln_b[l], gm_w_s[l], gm_b_s[l],
                         dsa_kv_g[l], dsa_w_uk[l], dsa_w_uv[l], conv_w[l], fox_b_f[l],
                         w_branch[l], w_out[l])
    return rms_norm(h, final_g)
```

```python
import functools

import jax
import jax.numpy as jnp
from jax import lax
from jax.experimental import pallas as pl
from jax.experimental.pallas import tpu as pltpu

D_MODEL = 1024
N_BRANCH = 4
W = 512
EPS = 1e-6
QB = 128
GM_GROUPS = 4
GM_CHUNK = 128
HEADS = 4
HEAD_DIM = W // HEADS
LATENT = 128
IDX_HEADS = 4
IDX_DIM = 64
TOPK_MAX = 256
CONV_WIDTH = 3
LANE = 128
NEG_BIG = -1e30
VMEM_LIMIT = 56 * 1024 * 1024

P16_DQ, P16_DK, P16_DV, P16_BQ = 0, 1, 2, 3
P16_QI = 4
P16_KI = 20
N16 = 21 * LANE
P32_AU, P32_AV, P32_AZ, P32_BZ = 8, 9, 10, 11
P32_CB, P32_CC, P32_CX, P32_CZ, P32_DZ = 12, 13, 14, 15, 16
P32_BC = 68
P32_SM = 69
N32 = 70 * LANE

_NT = (((1,), (1,)), ((), ()))


def _cparams(sem):
    return pltpu.CompilerParams(dimension_semantics=sem, vmem_limit_bytes=VMEM_LIMIT)


def _sigmoid(z):
    return 1.0 / (1.0 + jnp.exp(-z))


def _silu(z):
    return z * _sigmoid(z)


def _rmsnorm_kernel(x_ref, g_ref, o_ref):
    x = x_ref[...]
    y = x * lax.rsqrt(jnp.mean(x * x, axis=-1, keepdims=True) + EPS)
    o_ref[...] = (y * g_ref[...]).astype(o_ref.dtype)


def _rmsnorm(x2, g, out_dtype, tm=512):
    m, d = x2.shape
    return pl.pallas_call(
        _rmsnorm_kernel,
        grid=(m // tm,),
        in_specs=[pl.BlockSpec((tm, d), lambda i: (i, 0)), pl.BlockSpec((1, d), lambda i: (0, 0))],
        out_specs=pl.BlockSpec((tm, d), lambda i: (i, 0)),
        out_shape=jax.ShapeDtypeStruct((m, d), out_dtype),
        compiler_params=_cparams(("arbitrary",)),
        name="rmsnorm",
    )(x2, g.reshape(1, d))


def _matmul_kernel(a_ref, w_ref, o_ref):
    o_ref[...] = jnp.dot(a_ref[...], w_ref[...], preferred_element_type=jnp.float32).astype(o_ref.dtype)


def _matmul(a, w, out_dtype, tm, tn, name):
    m, k = a.shape
    n = w.shape[1]
    return pl.pallas_call(
        _matmul_kernel,
        grid=(n // tn, m // tm),
        in_specs=[pl.BlockSpec((tm, k), lambda j, i: (i, 0)), pl.BlockSpec((k, tn), lambda j, i: (0, j))],
        out_specs=pl.BlockSpec((tm, tn), lambda j, i: (i, j)),
        out_shape=jax.ShapeDtypeStruct((m, n), out_dtype),
        compiler_params=_cparams(("arbitrary", "arbitrary")),
        name=name,
    )(a, w)


def _prep_w_in(w):
    d = w.shape[0]
    sizes = (W, W, W,
             W, LATENT, IDX_HEADS * IDX_DIM, IDX_DIM, IDX_HEADS, W,
             W, W, W, W,
             W, W, W, HEADS, W,
             N_BRANCH * D_MODEL)
    parts, off = [], 0
    for s in sizes:
        parts.append(w[:, off:off + s])
        off += s
    (a_u, a_v, a_z, b_q, b_c, b_qi, b_ki, b_wi, b_z,
     c_b, c_c, c_x, c_z, d_q, d_k, d_v, d_f, d_z, gates) = parts
    zeros = lambda n: jnp.zeros((d, n), w.dtype)
    qi = []
    for h in range(IDX_HEADS):
        qi += [b_qi[:, h * IDX_DIM:(h + 1) * IDX_DIM], zeros(LANE - IDX_DIM)]
    w16 = jnp.concatenate([d_q, d_k, d_v, b_q] + qi + [b_ki, zeros(LANE - IDX_DIM)], axis=1)
    w32 = jnp.concatenate([gates, a_u, a_v, a_z, b_z, c_b, c_c, c_x, c_z, d_z, b_c,
                           d_f, b_wi, zeros(LANE - HEADS - IDX_HEADS)], axis=1)
    return w16.astype(jnp.bfloat16), w32.astype(jnp.bfloat16)


def _mix_ac_kernel(au_ref, av_ref, az_ref, cb_ref, cc_ref, cx_ref, cz_ref,
                   lng_ref, lnb_ref, ws_ref, bs_ref, cw_ref, ya_ref, yc_ref, halo_ref, *, tt):
    t = pl.program_id(1)

    v = av_ref[...]
    mu = jnp.mean(v, axis=-1, keepdims=True)
    vc = v - mu
    var = jnp.mean(vc * vc, axis=-1, keepdims=True)
    vn = (vc * lax.rsqrt(var + EPS) * lng_ref[...] + lnb_ref[...]).astype(jnp.bfloat16)
    row = lax.broadcasted_iota(jnp.int32, (GM_CHUNK, GM_CHUNK), 0)
    col = lax.broadcasted_iota(jnp.int32, (GM_CHUNK, GM_CHUNK), 1)
    tril = row >= col
    wg = [jnp.where(tril, ws_ref[g], 0.0).astype(jnp.bfloat16) for g in range(GM_GROUPS)]
    for ch in range(tt // GM_CHUNK):
        rows = slice(ch * GM_CHUNK, (ch + 1) * GM_CHUNK)
        for g in range(GM_GROUPS):
            cols = slice(g * LANE, (g + 1) * LANE)
            mixed = jnp.dot(wg[g], vn[rows, cols], preferred_element_type=jnp.float32) + bs_ref[:, cols]
            ya_ref[rows, cols] = (au_ref[rows, cols] * mixed * _silu(az_ref[rows, cols])).astype(ya_ref.dtype)

    @pl.when(t == 0)
    def _():
        halo_ref[...] = jnp.zeros_like(halo_ref)

    y = cc_ref[...] * cx_ref[...]
    ext = jnp.concatenate([halo_ref[...], y], axis=0)
    conv = cw_ref[2:3, :] * y
    for j in range(CONV_WIDTH - 1):
        shift = CONV_WIDTH - 1 - j
        conv = conv + cw_ref[j:j + 1, :] * ext[8 - shift:8 - shift + tt, :]
    yc_ref[...] = (cb_ref[...] * conv * _silu(cz_ref[...])).astype(yc_ref.dtype)
    halo_ref[...] = y[tt - 8:, :]


def _mix_ac(p32b, ln_g, ln_b, w_s, b_s, conv_w, tt=256):
    b, s, _ = p32b.shape
    blk = lambda idx: pl.BlockSpec((None, tt, W), lambda bi, ti, idx=idx: (bi, ti, idx))
    full2 = lambda shp: pl.BlockSpec(shp, lambda bi, ti: (0,) * len(shp))
    bs_full = jnp.repeat(jnp.transpose(b_s), LANE, axis=1)
    out = jax.ShapeDtypeStruct((b, s, W), jnp.bfloat16)
    return pl.pallas_call(
        functools.partial(_mix_ac_kernel, tt=tt),
        grid=(b, s // tt),
        in_specs=[blk(P32_AU), blk(P32_AV), blk(P32_AZ), blk(P32_CB), blk(P32_CC), blk(P32_CX), blk(P32_CZ),
                  full2((1, W)), full2((1, W)), full2((GM_GROUPS, GM_CHUNK, GM_CHUNK)),
                  full2((GM_CHUNK, W)), full2((CONV_WIDTH, W))],
        out_specs=[pl.BlockSpec((None, tt, W), lambda bi, ti: (bi, ti, 0))] * 2,
        out_shape=[out, out],
        scratch_shapes=[pltpu.VMEM((8, W), jnp.float32)],
        compiler_params=_cparams(("arbitrary", "arbitrary")),
        name="mix_ac",
    )(p32b, p32b, p32b, p32b, p32b, p32b, p32b,
      ln_g.reshape(1, W), ln_b.reshape(1, W), w_s, bs_full, conv_w)


def _split3(x):
    hi = x.astype(jnp.bfloat16)
    r1 = x - hi.astype(jnp.float32)
    mid = r1.astype(jnp.bfloat16)
    lo = (r1 - mid.astype(jnp.float32)).astype(jnp.bfloat16)
    return hi, mid, lo


def _fox_cum_kernel(f_ref, bias_ref, cum_ref, cumt_ref, carry_ref, *, s):
    row = lax.broadcasted_iota(jnp.int32, (LANE, LANE), 0)
    col = lax.broadcasted_iota(jnp.int32, (LANE, LANE), 1)
    ones_tril = jnp.where(row >= col, 1.0, 0.0).astype(jnp.bfloat16)
    carry_ref[...] = jnp.zeros_like(carry_ref)

    def body(c, _):
        off = pl.multiple_of(c * LANE, LANE)
        x = f_ref[pl.ds(off, LANE), :] + bias_ref[...]
        ls = jnp.minimum(x, 0.0) - jnp.log1p(jnp.exp(-jnp.abs(x)))
        hi, mid, lo = _split3(ls)
        dot = lambda p: jnp.dot(ones_tril, p, preferred_element_type=jnp.float32)
        cs = (dot(hi) + dot(mid)) + dot(lo) + carry_ref[0:1, :]
        cum_ref[pl.ds(off, LANE), :] = cs
        cumt_ref[c] = jnp.transpose(cs)[0:8, :]
        carry_ref[0:1, :] = cs[LANE - 1:LANE, :]
        return 0

    lax.fori_loop(0, s // LANE, body, 0)


def _fox_cum(p32b, b_f):
    b, s, _ = p32b.shape
    bias = jnp.zeros((1, LANE), jnp.float32).at[0, :HEADS].set(b_f)
    return pl.pallas_call(
        functools.partial(_fox_cum_kernel, s=s),
        grid=(b,),
        in_specs=[pl.BlockSpec((None, s, LANE), lambda bi: (bi, 0, P32_SM)),
                  pl.BlockSpec((1, LANE), lambda bi: (0, 0))],
        out_specs=[pl.BlockSpec((None, s, LANE), lambda bi: (bi, 0, 0)),
                   pl.BlockSpec((None, s // LANE, 8, LANE), lambda bi: (bi, 0, 0, 0))],
        out_shape=[jax.ShapeDtypeStruct((b, s, LANE), jnp.float32),
                   jax.ShapeDtypeStruct((b, s // LANE, 8, LANE), jnp.float32)],
        scratch_shapes=[pltpu.VMEM((8, LANE), jnp.float32)],
        compiler_params=_cparams(("arbitrary",)),
        name="fox_cum",
    )(p32b, bias)


def _num_chunks(i, kc):
    return ((i + 1) * QB + kc - 1) // kc


def _fox_kernel(q_ref, k_ref, v_ref, cq_ref, ckt_ref, z_ref, o_ref, m_ref, l_ref, acc_ref, *, kc):
    i = pl.program_id(1)
    nch = _num_chunks(i, kc)
    qpos = i * QB + lax.broadcasted_iota(jnp.int32, (QB, 1), 0)
    scale = HEAD_DIM ** -0.5
    for h in range(HEADS):
        hs = slice(h * HEAD_DIM, (h + 1) * HEAD_DIM)
        q = q_ref[:, hs]
        cq = cq_ref[:, h:h + 1]
        m_ref[...] = jnp.full_like(m_ref, NEG_BIG)
        l_ref[...] = jnp.zeros_like(l_ref)
        acc_ref[...] = jnp.zeros_like(acc_ref)

        def body(c, _):
            off = pl.multiple_of(c * kc, kc)
            k = k_ref[pl.ds(off, kc), hs]
            s = lax.dot_general(q, k, _NT, preferred_element_type=jnp.float32) * scale
            sub = kc // LANE
            ck = jnp.concatenate([ckt_ref[c * sub + j, h:h + 1, :] for j in range(sub)], axis=1)
            s = s + cq - ck
            kpos = off + lax.broadcasted_iota(jnp.int32, (1, kc), 1)
            s = jnp.where(kpos <= qpos, s, NEG_BIG)
            m_old = m_ref[...]
            m_new = jnp.maximum(m_old, jnp.max(s, axis=-1, keepdims=True))
            alpha = jnp.exp(m_old - m_new)
            p = jnp.exp(s - m_new)
            l_ref[...] = alpha * l_ref[...] + jnp.sum(p, axis=-1, keepdims=True)
            pv = jnp.dot(p.astype(jnp.bfloat16), v_ref[pl.ds(off, kc), hs], preferred_element_type=jnp.float32)
            acc_ref[...] = alpha * acc_ref[...] + pv
            m_ref[...] = m_new
            return 0

        lax.fori_loop(0, nch, body, 0)
        o = acc_ref[...] / l_ref[...]
        o_ref[:, hs] = (o * _silu(z_ref[:, hs])).astype(o_ref.dtype)


def _fox(p16b, p32b, cum, cumt):
    b, s, _ = p16b.shape
    kc = min(512, s)
    return pl.pallas_call(
        functools.partial(_fox_kernel, kc=kc),
        grid=(b, s // QB),
        in_specs=[pl.BlockSpec((None, QB, W), lambda bi, i: (bi, i, P16_DQ)),
                  pl.BlockSpec((None, s, W), lambda bi, i: (bi, 0, P16_DK)),
                  pl.BlockSpec((None, s, W), lambda bi, i: (bi, 0, P16_DV)),
                  pl.BlockSpec((None, QB, LANE), lambda bi, i: (bi, i, 0)),
                  pl.BlockSpec((None, s // LANE, 8, LANE), lambda bi, i: (bi, 0, 0, 0)),
                  pl.BlockSpec((None, QB, W), lambda bi, i: (bi, i, P32_DZ))],
        out_specs=pl.BlockSpec((None, QB, W), lambda bi, i: (bi, i, 0)),
        out_shape=jax.ShapeDtypeStruct((b, s, W), jnp.bfloat16),
        scratch_shapes=[pltpu.VMEM((QB, 1), jnp.float32), pltpu.VMEM((QB, 1), jnp.float32),
                        pltpu.VMEM((QB, HEAD_DIM), jnp.float32)],
        compiler_params=_cparams(("arbitrary", "arbitrary")),
        name="fox_attn",
    )(p16b, p16b, p16b, cum, cumt, p32b)


def _dsa_kernel(q_ref, qi_ref, sm_ref, z_ref, ki_ref, c_ref, kvg_ref, wuk_ref, wuv_ref, o_ref,
                cn_ref, sc_ref, thr_ref, jst_ref, lo_ref, hib_ref, hi_ref, clo_ref,
                m_ref, l_ref, acc_ref, *, s_len, kc, ksel, n_bisect):
    i = pl.program_id(1)
    nch = _num_chunks(i, kc)
    qpos = i * QB + lax.broadcasted_iota(jnp.int32, (QB, 1), 0)
    kf = float(ksel)
    inf = float("inf")

    @pl.when(i == 0)
    def _():
        def nbody(c, _):
            off = pl.multiple_of(c * kc, kc)
            x = c_ref[pl.ds(off, kc), :]
            y = x * lax.rsqrt(jnp.mean(x * x, axis=-1, keepdims=True) + EPS) * kvg_ref[...]
            cn_ref[pl.ds(off, kc), :] = y.astype(cn_ref.dtype)
            return 0
        lax.fori_loop(0, s_len // kc, nbody, 0)

    wi = sm_ref[:, HEADS:HEADS + IDX_HEADS] * (IDX_HEADS ** -0.5 * IDX_DIM ** -0.5)

    def sbody(c, _):
        off = pl.multiple_of(c * kc, kc)
        kk = ki_ref[pl.ds(off, kc), :]
        sc = jnp.zeros((QB, kc), jnp.float32)
        for h in range(IDX_HEADS):
            d = lax.dot_general(qi_ref[:, h * LANE:(h + 1) * LANE], kk, _NT, preferred_element_type=jnp.float32)
            sc = sc + wi[:, h:h + 1] * jnp.maximum(d, 0.0)
        kpos = off + lax.broadcasted_iota(jnp.int32, (1, kc), 1)
        sc_ref[c] = jnp.where(kpos <= qpos, sc, -inf)
        return 0

    lax.fori_loop(0, nch, sbody, 0)

    def fold_chunks(fn, init):
        def cbody(c, acc):
            off = pl.multiple_of(c * kc, kc)
            sc = sc_ref[c]
            for j in range(kc // LANE):
                kpos = off + j * LANE + lax.broadcasted_iota(jnp.int32, (1, LANE), 1)
                acc = fn(acc, sc[:, j * LANE:(j + 1) * LANE], kpos)
            return acc
        return lax.fori_loop(0, nch, cbody, init)

    def count(pred, t):
        tb = jnp.broadcast_to(t, (QB, LANE))
        acc = fold_chunks(lambda a, x, kp: a + jnp.where(pred(x, tb, kp), 1.0, 0.0),
                          jnp.zeros((QB, LANE), jnp.float32))
        return jnp.sum(acc, axis=-1, keepdims=True)

    def any_row(flag):
        return (jnp.max(jnp.where(flag, 1.0, 0.0)) > 0.5).astype(jnp.int32)

    thr_ref[...] = jnp.full_like(thr_ref, -inf)
    jst_ref[...] = jnp.full_like(jst_ref, -1)

    @pl.when(i * QB >= ksel)
    def _():
        mn0 = jnp.full((QB, LANE), inf, jnp.float32)
        mx0 = jnp.full((QB, LANE), -inf, jnp.float32)
        mn, mx = fold_chunks(
            lambda a, x, kp: (jnp.minimum(a[0], jnp.where(x == -inf, inf, x)), jnp.maximum(a[1], x)),
            (mn0, mx0))
        lo_ref[...] = jnp.min(mn, axis=-1, keepdims=True)
        hib_ref[...] = jnp.max(mx, axis=-1, keepdims=True)
        hi_ref[...] = jnp.full_like(hi_ref, inf)
        clo_ref[...] = (qpos + 1).astype(jnp.float32)

        def bis_cond(st):
            it, go = st
            return jnp.logical_and(it < n_bisect, go > 0)

        def bis_body(st):
            it, _ = st
            lo, hib = lo_ref[...], hib_ref[...]
            mid = 0.5 * lo + 0.5 * hib
            cnt = count(lambda x, tb, kp: x >= tb, mid)
            ge = cnt >= kf
            lo_ref[...] = jnp.where(ge, mid, lo)
            clo = jnp.where(ge, cnt, clo_ref[...])
            clo_ref[...] = clo
            hib_ref[...] = jnp.where(ge, hib, mid)
            hi_ref[...] = jnp.where(ge, hi_ref[...], mid)
            return it + 1, any_row(clo != kf)

        lax.while_loop(bis_cond, bis_body, (jnp.int32(0), jnp.int32(1)))

        resolved = clo_ref[...] == kf
        thr_ref[...] = jnp.where(resolved, lo_ref[...], inf)
        clo_ref[...] = jnp.where(resolved, kf, 0.0)

        def peel_body(go):
            done = clo_ref[...] >= kf
            hi = hi_ref[...]
            hb = jnp.broadcast_to(hi, (QB, LANE))
            nxt = fold_chunks(lambda a, x, kp: jnp.maximum(a, jnp.where(x < hb, x, -inf)),
                              jnp.full((QB, LANE), -inf, jnp.float32))
            t = jnp.where(done, thr_ref[...], jnp.max(nxt, axis=-1, keepdims=True))
            cnt = count(lambda x, tb, kp: x >= tb, t)
            thr_ref[...] = t
            clo_ref[...] = cnt
            hi_ref[...] = jnp.where(done, hi, t)
            return any_row(cnt < kf)

        lax.while_loop(lambda go: go > 0, peel_body, any_row(clo_ref[...] < kf))

        jst_ref[...] = jnp.full_like(jst_ref, s_len)

        @pl.when(any_row(clo_ref[...] > kf) > 0)
        def _():
            thr = thr_ref[...]
            need = kf - count(lambda x, tb, kp: x > tb, thr)
            excess = clo_ref[...] > kf

            def tie_body(bi, j):
                cand = j + lax.shift_left(jnp.int32(1), (s_len.bit_length() - 1) - bi)
                cb = jnp.broadcast_to(cand, (QB, LANE))
                tb = jnp.broadcast_to(thr, (QB, LANE))
                acc = fold_chunks(
                    lambda a, x, kp: a + jnp.where(jnp.logical_and(x == tb, kp < cb), 1.0, 0.0),
                    jnp.zeros((QB, LANE), jnp.float32))
                below = jnp.sum(acc, axis=-1, keepdims=True)
                return jnp.where(below < need, cand, j)

            j = lax.fori_loop(0, s_len.bit_length(), tie_body, jnp.zeros((QB, 1), jnp.int32))
            jst_ref[...] = jnp.where(excess, j, s_len)

    qlat = []
    for h in range(HEADS):
        hs = slice(h * HEAD_DIM, (h + 1) * HEAD_DIM)
        ql = lax.dot_general(q_ref[:, hs], wuk_ref[h], _NT, preferred_element_type=jnp.float32)
        qlat.append((ql * HEAD_DIM ** -0.5).astype(jnp.bfloat16))
    qlat = jnp.concatenate(qlat, axis=0)
    m_ref[...] = jnp.full_like(m_ref, NEG_BIG)
    l_ref[...] = jnp.zeros_like(l_ref)
    acc_ref[...] = jnp.zeros_like(acc_ref)
    thr = thr_ref[...]
    jst = jst_ref[...]

    def abody(c, _):
        off = pl.multiple_of(c * kc, kc)
        cn = cn_ref[pl.ds(off, kc), :]
        sc = sc_ref[c]
        kpos = off + lax.broadcasted_iota(jnp.int32, (1, kc), 1)
        sel = jnp.logical_or(sc > thr, jnp.logical_and(sc == thr, kpos <= jst))
        logits = lax.dot_general(qlat, cn, _NT, preferred_element_type=jnp.float32)
        for h in range(HEADS):
            rs = slice(h * QB, (h + 1) * QB)
            s = jnp.where(sel, logits[rs, :], NEG_BIG)
            m_old = m_ref[rs, :]
            m_new = jnp.maximum(m_old, jnp.max(s, axis=-1, keepdims=True))
            alpha = jnp.exp(m_old - m_new)
            p = jnp.exp(s - m_new)
            l_ref[rs, :] = alpha * l_ref[rs, :] + jnp.sum(p, axis=-1, keepdims=True)
            pv = jnp.dot(p.astype(jnp.bfloat16), cn, preferred_element_type=jnp.float32)
            acc_ref[rs, :] = alpha * acc_ref[rs, :] + pv
            m_ref[rs, :] = m_new
        return 0

    lax.fori_loop(0, nch, abody, 0)

    for h in range(HEADS):
        hs = slice(h * HEAD_DIM, (h + 1) * HEAD_DIM)
        rs = slice(h * QB, (h + 1) * QB)
        o_lat = (acc_ref[rs, :] / l_ref[rs, :]).astype(jnp.bfloat16)
        o = jnp.dot(o_lat, wuv_ref[h], preferred_element_type=jnp.float32)
        o_ref[:, hs] = (o * _silu(z_ref[:, hs])).astype(o_ref.dtype)


def _dsa(p16b, p32b, kv_g, w_uk, w_uv):
    b, s, _ = p16b.shape
    kc = min(512, s)
    ksel = min(TOPK_MAX, s // 4)
    assert ksel % QB == 0 and s % kc == 0
    col = lambda: pltpu.VMEM((QB, 1), jnp.float32)
    full3 = pl.BlockSpec((HEADS, LATENT, HEAD_DIM), lambda bi, i: (0, 0, 0))
    return pl.pallas_call(
        functools.partial(_dsa_kernel, s_len=s, kc=kc, ksel=ksel, n_bisect=24),
        grid=(b, s // QB),
        in_specs=[pl.BlockSpec((None, QB, W), lambda bi, i: (bi, i, P16_BQ)),
                  pl.BlockSpec((None, QB, W), lambda bi, i: (bi, i, P16_QI)),
                  pl.BlockSpec((None, QB, LANE), lambda bi, i: (bi, i, P32_SM)),
                  pl.BlockSpec((None, QB, W), lambda bi, i: (bi, i, P32_BZ)),
                  pl.BlockSpec((None, s, LANE), lambda bi, i: (bi, 0, P16_KI)),
                  pl.BlockSpec((None, s, LANE), lambda bi, i: (bi, 0, P32_BC)),
                  pl.BlockSpec((1, LATENT), lambda bi, i: (0, 0)),
                  full3, full3],
        out_specs=pl.BlockSpec((None, QB, W), lambda bi, i: (bi, i, 0)),
        out_shape=jax.ShapeDtypeStruct((b, s, W), jnp.bfloat16),
        scratch_shapes=[pltpu.VMEM((s, LATENT), jnp.bfloat16),
                        pltpu.VMEM((s // kc, QB, kc), jnp.float32),
                        col(),
                        pltpu.VMEM((QB, 1), jnp.int32),
                        col(), col(), col(), col(),
                        pltpu.VMEM((HEADS * QB, 1), jnp.float32),
                        pltpu.VMEM((HEADS * QB, 1), jnp.float32),
                        pltpu.VMEM((HEADS * QB, LATENT), jnp.float32)],
        compiler_params=_cparams(("arbitrary", "arbitrary")),
        name="dsa_attn",
    )(p16b, p16b, p32b, p32b, p16b, p32b, kv_g.reshape(1, LATENT),
      w_uk.astype(jnp.bfloat16), w_uv.astype(jnp.bfloat16))


def _merge_kernel(ya_ref, yb_ref, yc_ref, yd_ref, g_ref, x_ref, wb_ref, wo_ref, gn_ref, h_ref, hn_ref):
    merged = None
    for n, y_ref in enumerate((ya_ref, yb_ref, yc_ref, yd_ref)):
        lifted = jnp.dot(y_ref[...], wb_ref[n], preferred_element_type=jnp.float32)
        term = _sigmoid(g_ref[:, n * D_MODEL:(n + 1) * D_MODEL]) * lifted
        merged = term if merged is None else merged + term
    h = x_ref[...] + jnp.dot(merged.astype(jnp.bfloat16), wo_ref[...], preferred_element_type=jnp.float32)
    h_ref[...] = h
    hn = h * lax.rsqrt(jnp.mean(h * h, axis=-1, keepdims=True) + EPS) * gn_ref[...]
    hn_ref[...] = hn.astype(hn_ref.dtype)


def _merge(ya, yb, yc, yd, p32, h, wb, wo, g_next, hn_dtype, tm=256):
    m, d = h.shape
    yblk = pl.BlockSpec((tm, W), lambda i: (i, 0))
    hblk = pl.BlockSpec((tm, d), lambda i: (i, 0))
    return pl.pallas_call(
        _merge_kernel,
        grid=(m // tm,),
        in_specs=[yblk, yblk, yblk, yblk,
                  pl.BlockSpec((tm, N_BRANCH * d), lambda i: (i, 0)),
                  hblk,
                  pl.BlockSpec((N_BRANCH, W, d), lambda i: (0, 0, 0)),
                  pl.BlockSpec((d, d), lambda i: (0, 0)),
                  pl.BlockSpec((1, d), lambda i: (0, 0))],
        out_specs=[hblk, hblk],
        out_shape=[jax.ShapeDtypeStruct((m, d), jnp.float32), jax.ShapeDtypeStruct((m, d), hn_dtype)],
        compiler_params=_cparams(("arbitrary",)),
        name="merge",
    )(ya, yb, yc, yd, p32, h, wb, wo, g_next.reshape(1, d))


def kernel(x, norm_g, w_in, gm_ln_g, gm_ln_b, gm_w_s, gm_b_s, dsa_kv_g, dsa_w_uk, dsa_w_uv,
           conv_w, fox_b_f, w_branch, w_out, final_g):
    b, s, d = x.shape
    depth = w_in.shape[0]
    m = b * s
    h = x.reshape(m, d)
    hn = _rmsnorm(h, norm_g[0], jnp.bfloat16)
    for l in range(depth):
        w16, w32 = _prep_w_in(w_in[l])
        p16 = _matmul(hn, w16, jnp.bfloat16, tm=512, tn=N16, name="in_proj16")
        p32 = _matmul(hn, w32, jnp.float32, tm=512, tn=N32 // 5, name="in_proj32")
        p16b = p16.reshape(b, s, N16)
        p32b = p32.reshape(b, s, N32)
        ya, yc = _mix_ac(p32b, gm_ln_g[l], gm_ln_b[l], gm_w_s[l], gm_b_s[l], conv_w[l])
        cum, cumt = _fox_cum(p32b, fox_b_f[l])
        yd = _fox(p16b, p32b, cum, cumt)
        yb = _dsa(p16b, p32b, dsa_kv_g[l], dsa_w_uk[l], dsa_w_uv[l])
        last = l == depth - 1
        g_next = final_g if last else norm_g[l + 1]
        h, hn = _merge(ya.reshape(m, W), yb.reshape(m, W), yc.reshape(m, W), yd.reshape(m, W), p32, h,
                       w_branch[l].astype(jnp.bfloat16), w_out[l].astype(jnp.bfloat16), g_next,
                       jnp.float32 if last else jnp.bfloat16)
    return hn.reshape(b, s, d)
```

```python
import functools

import numpy as np
import jax
import jax.numpy as jnp
from jax import lax
from jax.experimental import pallas as pl
from jax.experimental.pallas import tpu as pltpu

D_MODEL = 1024
N_BRANCH = 4
W = 512
EPS = 1e-6
QB = 128
QBE = 256
KC = 512
GM_GROUPS = 4
GM_CHUNK = 128
HEADS = 4
HEAD_DIM = W // HEADS
LATENT = 128
IDX_HEADS = 4
IDX_DIM = 64
TOPK_MAX = 256
CONV_WIDTH = 3
LANE = 128
NEG_BIG = -1e30
VMEM_LIMIT = 56 * 1024 * 1024

P16_DQ, P16_DK, P16_BQ = 0, 1, 2
P16_QI = 3
P16_KI = 16
N16 = 17 * LANE
P32_AU, P32_AV, P32_AZ, P32_BZ = 8, 9, 10, 11
P32_CB, P32_CC, P32_CX, P32_CZ, P32_DZ = 12, 13, 14, 15, 16
P32_BC = 68
P32_SM = 69
N32 = 70 * LANE

_NT = (((1,), (1,)), ((), ()))


def _cparams(sem):
    return pltpu.CompilerParams(dimension_semantics=sem, vmem_limit_bytes=VMEM_LIMIT)


def _sigmoid(z):
    return 1.0 / (1.0 + jnp.exp(-z))


def _silu(z):
    return z * _sigmoid(z)


def _rmsnorm_kernel(x_ref, g_ref, o_ref):
    x = x_ref[...]
    y = x * lax.rsqrt(jnp.mean(x * x, axis=-1, keepdims=True) + EPS)
    o_ref[...] = (y * g_ref[...]).astype(o_ref.dtype)


def _rmsnorm(x2, g, out_dtype, tm=512):
    m, d = x2.shape
    return pl.pallas_call(
        _rmsnorm_kernel,
        grid=(m // tm,),
        in_specs=[pl.BlockSpec((tm, d), lambda i: (i, 0)), pl.BlockSpec((1, d), lambda i: (0, 0))],
        out_specs=pl.BlockSpec((tm, d), lambda i: (i, 0)),
        out_shape=jax.ShapeDtypeStruct((m, d), out_dtype),
        compiler_params=_cparams(("arbitrary",)),
        name="rmsnorm",
    )(x2, g.reshape(1, d))


def _matmul_kernel(a_ref, w_ref, o_ref):
    o_ref[...] = jnp.dot(a_ref[...], w_ref[...], preferred_element_type=jnp.float32).astype(o_ref.dtype)


def _matmul(a, w, out_dtype, tm, tn, name):
    m, k = a.shape
    n = w.shape[1]
    return pl.pallas_call(
        _matmul_kernel,
        grid=(n // tn, m // tm),
        in_specs=[pl.BlockSpec((tm, k), lambda j, i: (i, 0)), pl.BlockSpec((k, tn), lambda j, i: (0, j))],
        out_specs=pl.BlockSpec((tm, tn), lambda j, i: (i, j)),
        out_shape=jax.ShapeDtypeStruct((m, n), out_dtype),
        compiler_params=_cparams(("arbitrary", "arbitrary")),
        name=name,
    )(a, w)


def _prep_w_in(w):
    d = w.shape[0]
    sizes = (W, W, W,
             W, LATENT, IDX_HEADS * IDX_DIM, IDX_DIM, IDX_HEADS, W,
             W, W, W, W,
             W, W, W, HEADS, W,
             N_BRANCH * D_MODEL)
    parts, off = [], 0
    for s in sizes:
        parts.append(w[:, off:off + s])
        off += s
    (a_u, a_v, a_z, b_q, b_c, b_qi, b_ki, b_wi, b_z,
     c_b, c_c, c_x, c_z, d_q, d_k, d_v, d_f, d_z, gates) = parts
    zeros = lambda n: jnp.zeros((d, n), w.dtype)
    qi = []
    for h in range(IDX_HEADS):
        qi += [b_qi[:, h * IDX_DIM:(h + 1) * IDX_DIM], zeros(LANE - IDX_DIM)]
    w16 = jnp.concatenate([d_q, d_k, b_q] + qi + [b_ki, zeros(LANE - IDX_DIM)], axis=1)
    w32 = jnp.concatenate([gates, a_u, a_v, a_z, b_z, c_b, c_c, c_x, c_z, d_z, b_c,
                           d_f, b_wi, zeros(LANE - HEADS - IDX_HEADS)], axis=1)
    return w16.astype(jnp.bfloat16), w32.astype(jnp.bfloat16), jnp.transpose(d_v).astype(jnp.bfloat16)


def _matmul_t_kernel(wt_ref, a_ref, o_ref):
    o_ref[...] = lax.dot_general(wt_ref[...], a_ref[...], _NT,
                                 preferred_element_type=jnp.float32).astype(o_ref.dtype)


def _matmul_t(a, wt, tm, name):
    m, k = a.shape
    n = wt.shape[0]
    return pl.pallas_call(
        _matmul_t_kernel,
        grid=(m // tm,),
        in_specs=[pl.BlockSpec((n, k), lambda i: (0, 0)), pl.BlockSpec((tm, k), lambda i: (i, 0))],
        out_specs=pl.BlockSpec((None, n, tm), lambda i: (i, 0, 0)),
        out_shape=jax.ShapeDtypeStruct((m // tm, n, tm), jnp.bfloat16),
        compiler_params=_cparams(("arbitrary",)),
        name=name,
    )(wt, a)


def _mix_ac_kernel(au_ref, av_ref, az_ref, cb_ref, cc_ref, cx_ref, cz_ref,
                   lng_ref, lnb_ref, ws_ref, bs_ref, cw_ref, ya_ref, yc_ref, halo_ref, *, tt):
    t = pl.program_id(1)

    v = av_ref[...]
    mu = jnp.mean(v, axis=-1, keepdims=True)
    vc = v - mu
    var = jnp.mean(vc * vc, axis=-1, keepdims=True)
    vn = (vc * lax.rsqrt(var + EPS) * lng_ref[...] + lnb_ref[...]).astype(jnp.bfloat16)
    row = lax.broadcasted_iota(jnp.int32, (GM_CHUNK, GM_CHUNK), 0)
    col = lax.broadcasted_iota(jnp.int32, (GM_CHUNK, GM_CHUNK), 1)
    tril = row >= col
    wg = [jnp.where(tril, ws_ref[g], 0.0).astype(jnp.bfloat16) for g in range(GM_GROUPS)]
    for ch in range(tt // GM_CHUNK):
        rows = slice(ch * GM_CHUNK, (ch + 1) * GM_CHUNK)
        for g in range(GM_GROUPS):
            cols = slice(g * LANE, (g + 1) * LANE)
            mixed = jnp.dot(wg[g], vn[rows, cols], preferred_element_type=jnp.float32) + bs_ref[:, cols]
            ya_ref[rows, cols] = (au_ref[rows, cols] * mixed * _silu(az_ref[rows, cols])).astype(ya_ref.dtype)

    @pl.when(t == 0)
    def _():
        halo_ref[...] = jnp.zeros_like(halo_ref)

    y = cc_ref[...] * cx_ref[...]
    ext = jnp.concatenate([halo_ref[...], y], axis=0)
    conv = cw_ref[2:3, :] * y
    for j in range(CONV_WIDTH - 1):
        shift = CONV_WIDTH - 1 - j
        conv = conv + cw_ref[j:j + 1, :] * ext[8 - shift:8 - shift + tt, :]
    yc_ref[...] = (cb_ref[...] * conv * _silu(cz_ref[...])).astype(yc_ref.dtype)
    halo_ref[...] = y[tt - 8:, :]


def _mix_ac(p32b, ln_g, ln_b, w_s, b_s, conv_w, tt=256):
    b, s, _ = p32b.shape
    blk = lambda idx: pl.BlockSpec((None, tt, W), lambda bi, ti, idx=idx: (bi, ti, idx))
    full2 = lambda shp: pl.BlockSpec(shp, lambda bi, ti: (0,) * len(shp))
    bs_full = jnp.repeat(jnp.transpose(b_s), LANE, axis=1)
    out = jax.ShapeDtypeStruct((b, s, W), jnp.bfloat16)
    return pl.pallas_call(
        functools.partial(_mix_ac_kernel, tt=tt),
        grid=(b, s // tt),
        in_specs=[blk(P32_AU), blk(P32_AV), blk(P32_AZ), blk(P32_CB), blk(P32_CC), blk(P32_CX), blk(P32_CZ),
                  full2((1, W)), full2((1, W)), full2((GM_GROUPS, GM_CHUNK, GM_CHUNK)),
                  full2((GM_CHUNK, W)), full2((CONV_WIDTH, W))],
        out_specs=[pl.BlockSpec((None, tt, W), lambda bi, ti: (bi, ti, 0))] * 2,
        out_shape=[out, out],
        scratch_shapes=[pltpu.VMEM((8, W), jnp.float32)],
        compiler_params=_cparams(("arbitrary", "arbitrary")),
        name="mix_ac",
    )(p32b, p32b, p32b, p32b, p32b, p32b, p32b,
      ln_g.reshape(1, W), ln_b.reshape(1, W), w_s, bs_full, conv_w)


def _split3(x):
    hi = x.astype(jnp.bfloat16)
    r1 = x - hi.astype(jnp.float32)
    mid = r1.astype(jnp.bfloat16)
    lo = (r1 - mid.astype(jnp.float32)).astype(jnp.bfloat16)
    return hi, mid, lo


def _aug_placement():
    pq = np.zeros((3 * LANE, W), np.float32)
    pk = np.zeros((3 * LANE, W), np.float32)
    cq = np.zeros((1, W), np.float32)
    ck = np.zeros((1, W), np.float32)
    for h in range(HEADS):
        for j in range(3):
            pq[j * LANE + h, h * HEAD_DIM + j] = 1.0
            cq[0, h * HEAD_DIM + 3 + j] = 1.0
            pk[j * LANE + h, h * HEAD_DIM + 3 + j] = -1.0
            ck[0, h * HEAD_DIM + j] = 1.0
    return (jnp.asarray(pq, jnp.bfloat16), jnp.asarray(pk, jnp.bfloat16), jnp.asarray(cq), jnp.asarray(ck))


def _fox_cum_kernel(f_ref, bias_ref, pq_ref, pk_ref, cq_ref, ck_ref, qa_ref, ka_ref, carry_ref, *, s):
    row = lax.broadcasted_iota(jnp.int32, (LANE, LANE), 0)
    col = lax.broadcasted_iota(jnp.int32, (LANE, LANE), 1)
    ones_tril = jnp.where(row >= col, 1.0, 0.0).astype(jnp.bfloat16)
    carry_ref[...] = jnp.zeros_like(carry_ref)

    def body(c, _):
        off = pl.multiple_of(c * LANE, LANE)
        x = f_ref[pl.ds(off, LANE), :] + bias_ref[...]
        ls = jnp.minimum(x, 0.0) - jnp.log1p(jnp.exp(-jnp.abs(x)))
        hi, mid, lo = _split3(ls)
        dot = lambda p: jnp.dot(ones_tril, p, preferred_element_type=jnp.float32)
        cs = (dot(hi) + dot(mid)) + dot(lo) + carry_ref[0:1, :]
        carry_ref[0:1, :] = cs[LANE - 1:LANE, :]
        parts = jnp.concatenate(_split3(cs), axis=1)
        qa = jnp.dot(parts, pq_ref[...], preferred_element_type=jnp.float32) + cq_ref[...]
        ka = jnp.dot(parts, pk_ref[...], preferred_element_type=jnp.float32) + ck_ref[...]
        qa_ref[pl.ds(off, LANE), :] = qa.astype(qa_ref.dtype)
        ka_ref[pl.ds(off, LANE), :] = ka.astype(ka_ref.dtype)
        return 0

    lax.fori_loop(0, s // LANE, body, 0)


def _fox_cum(p32b, b_f):
    b, s, _ = p32b.shape
    bias = jnp.zeros((1, LANE), jnp.float32).at[0, :HEADS].set(b_f)
    pq, pk, cq, ck = _aug_placement()
    const = lambda shp: pl.BlockSpec(shp, lambda bi: (0, 0))
    out = jax.ShapeDtypeStruct((b, s, W), jnp.bfloat16)
    return pl.pallas_call(
        functools.partial(_fox_cum_kernel, s=s),
        grid=(b,),
        in_specs=[pl.BlockSpec((None, s, LANE), lambda bi: (bi, 0, P32_SM)), const((1, LANE)),
                  const((3 * LANE, W)), const((3 * LANE, W)), const((1, W)), const((1, W))],
        out_specs=[pl.BlockSpec((None, s, W), lambda bi: (bi, 0, 0))] * 2,
        out_shape=[out, out],
        scratch_shapes=[pltpu.VMEM((8, LANE), jnp.float32)],
        compiler_params=_cparams(("arbitrary",)),
        name="fox_cum",
    )(p32b, bias, pq, pk, cq, ck)


def _num_chunks(i, kc):
    return ((i + 1) * QB + kc - 1) // kc


def _fox_kernel(q_ref, qa_ref, k_ref, ka_ref, vt_ref, z_ref, o_ref, qf_ref, m_ref, l_ref, acc_ref, *, kc):
    i = pl.program_id(1)
    ndiag = (i * QBE) // kc
    scale = HEAD_DIM ** -0.5
    for h in range(HEADS):
        hs = slice(h * HEAD_DIM, (h + 1) * HEAD_DIM)
        qs = (q_ref[:, hs].astype(jnp.float32) * scale).astype(jnp.bfloat16)
        qf_ref[h] = jnp.concatenate([qs, qa_ref[:, hs]], axis=1)
    m_ref[...] = jnp.full_like(m_ref, NEG_BIG)
    l_ref[...] = jnp.zeros_like(l_ref)
    acc_ref[...] = jnp.zeros_like(acc_ref)

    def step(c, masked):
        off = pl.multiple_of(c * kc, kc)

        def scores(h):
            hs = slice(h * HEAD_DIM, (h + 1) * HEAD_DIM)
            kf = jnp.concatenate([k_ref[pl.ds(off, kc), hs], ka_ref[pl.ds(off, kc), hs]], axis=1)
            st = lax.dot_general(kf, qf_ref[h], _NT, preferred_element_type=jnp.float32)
            if masked:
                kpos = off + lax.broadcasted_iota(jnp.int32, (kc, QBE), 0)
                qpos = i * QBE + lax.broadcasted_iota(jnp.int32, (kc, QBE), 1)
                st = jnp.where(kpos <= qpos, st, NEG_BIG)
            return st

        st = [scores(h) for h in range(HEADS)]
        for h in range(HEADS):
            hs = slice(h * HEAD_DIM, (h + 1) * HEAD_DIM)
            m_old = m_ref[h]
            m_new = jnp.maximum(m_old, jnp.max(st[h], axis=0, keepdims=True))
            alpha = jnp.exp(m_old - m_new)
            p = jnp.exp(st[h] - m_new)
            l_ref[h] = alpha * l_ref[h] + jnp.sum(p, axis=0, keepdims=True)
            pv = jnp.dot(vt_ref[c, hs, :], p.astype(jnp.bfloat16), preferred_element_type=jnp.float32)
            acc_ref[h] = alpha * acc_ref[h] + pv
            m_ref[h] = m_new

    def body(c, _):
        step(c, False)
        return 0

    lax.fori_loop(0, ndiag, body, 0)
    step(ndiag, True)
    for h in range(HEADS):
        hs = slice(h * HEAD_DIM, (h + 1) * HEAD_DIM)
        o = jnp.transpose(acc_ref[h] / l_ref[h])
        o_ref[:, hs] = (o * _silu(z_ref[:, hs])).astype(o_ref.dtype)


def _fox(p16b, p32b, qaug, kaug, vt):
    b, s, _ = p16b.shape
    kc = vt.shape[-1]
    assert s % kc == 0 and kc % QBE == 0
    nck = s // kc
    vt = vt.reshape(b, nck, W, kc)
    row = lambda: pltpu.VMEM((HEADS, 1, QBE), jnp.float32)
    return pl.pallas_call(
        functools.partial(_fox_kernel, kc=kc),
        grid=(b, s // QBE),
        in_specs=[pl.BlockSpec((None, QBE, W), lambda bi, i: (bi, i, P16_DQ)),
                  pl.BlockSpec((None, QBE, W), lambda bi, i: (bi, i, 0)),
                  pl.BlockSpec((None, s, W), lambda bi, i: (bi, 0, P16_DK)),
                  pl.BlockSpec((None, s, W), lambda bi, i: (bi, 0, 0)),
                  pl.BlockSpec((None, nck, W, kc), lambda bi, i: (bi, 0, 0, 0)),
                  pl.BlockSpec((None, QBE, W), lambda bi, i: (bi, i, P32_DZ))],
        out_specs=pl.BlockSpec((None, QBE, W), lambda bi, i: (bi, i, 0)),
        out_shape=jax.ShapeDtypeStruct((b, s, W), jnp.bfloat16),
        scratch_shapes=[pltpu.VMEM((HEADS, QBE, 2 * HEAD_DIM), jnp.bfloat16),
                        row(), row(),
                        pltpu.VMEM((HEADS, HEAD_DIM, QBE), jnp.float32)],
        compiler_params=_cparams(("arbitrary", "arbitrary")),
        name="fox_attn",
    )(p16b, qaug, p16b, kaug, vt, p32b)


def _dsa_kernel(q_ref, qi_ref, sm_ref, z_ref, ki_ref, c_ref, kvg_ref, wuk_ref, wuv_ref, o_ref,
                cn_ref, sc_ref, thr_ref, jst_ref, lo_ref, hib_ref, hi_ref, clo_ref,
                m_ref, l_ref, acc_ref, *, s_len, kc, ksel, n_bisect):
    i = pl.program_id(1)
    nch = _num_chunks(i, kc)
    qpos = i * QB + lax.broadcasted_iota(jnp.int32, (QB, 1), 0)
    kf = float(ksel)
    inf = float("inf")

    @pl.when(i == 0)
    def _():
        def nbody(c, _):
            off = pl.multiple_of(c * kc, kc)
            x = c_ref[pl.ds(off, kc), :]
            y = x * lax.rsqrt(jnp.mean(x * x, axis=-1, keepdims=True) + EPS) * kvg_ref[...]
            cn_ref[pl.ds(off, kc), :] = y.astype(cn_ref.dtype)
            return 0
        lax.fori_loop(0, s_len // kc, nbody, 0)

    wi = sm_ref[:, HEADS:HEADS + IDX_HEADS] * (IDX_HEADS ** -0.5 * IDX_DIM ** -0.5)

    def sbody(c, _):
        off = pl.multiple_of(c * kc, kc)
        kk = ki_ref[pl.ds(off, kc), :]
        sc = jnp.zeros((QB, kc), jnp.float32)
        for h in range(IDX_HEADS):
            d = lax.dot_general(qi_ref[:, h * LANE:(h + 1) * LANE], kk, _NT, preferred_element_type=jnp.float32)
            sc = sc + wi[:, h:h + 1] * jnp.maximum(d, 0.0)
        kpos = off + lax.broadcasted_iota(jnp.int32, (1, kc), 1)
        sc_ref[c] = jnp.where(kpos <= qpos, sc, -inf)
        return 0

    lax.fori_loop(0, nch, sbody, 0)

    def fold_chunks(fn, init):
        def cbody(c, acc):
            off = pl.multiple_of(c * kc, kc)
            sc = sc_ref[c]
            for j in range(kc // LANE):
                kpos = off + j * LANE + lax.broadcasted_iota(jnp.int32, (1, LANE), 1)
                acc = fn(acc, sc[:, j * LANE:(j + 1) * LANE], kpos)
            return acc
        return lax.fori_loop(0, nch, cbody, init)

    def count(pred, t):
        tb = jnp.broadcast_to(t, (QB, LANE))
        acc = fold_chunks(lambda a, x, kp: a + jnp.where(pred(x, tb, kp), 1.0, 0.0),
                          jnp.zeros((QB, LANE), jnp.float32))
        return jnp.sum(acc, axis=-1, keepdims=True)

    def any_row(flag):
        return (jnp.max(jnp.where(flag, 1.0, 0.0)) > 0.5).astype(jnp.int32)

    thr_ref[...] = jnp.full_like(thr_ref, -inf)
    jst_ref[...] = jnp.full_like(jst_ref, -1)

    @pl.when(i * QB >= ksel)
    def _():
        mn0 = jnp.full((QB, LANE), inf, jnp.float32)
        mx0 = jnp.full((QB, LANE), -inf, jnp.float32)
        mn, mx = fold_chunks(
            lambda a, x, kp: (jnp.minimum(a[0], jnp.where(x == -inf, inf, x)), jnp.maximum(a[1], x)),
            (mn0, mx0))
        lo_ref[...] = jnp.min(mn, axis=-1, keepdims=True)
        hib_ref[...] = jnp.max(mx, axis=-1, keepdims=True)
        hi_ref[...] = jnp.full_like(hi_ref, inf)
        clo_ref[...] = (qpos + 1).astype(jnp.float32)

        def bis_cond(st):
            it, go = st
            return jnp.logical_and(it < n_bisect, go > 0)

        def bis_body(st):
            it, _ = st
            lo, hib = lo_ref[...], hib_ref[...]
            mid = 0.5 * lo + 0.5 * hib
            cnt = count(lambda x, tb, kp: x >= tb, mid)
            ge = cnt >= kf
            lo_ref[...] = jnp.where(ge, mid, lo)
            clo = jnp.where(ge, cnt, clo_ref[...])
            clo_ref[...] = clo
            hib_ref[...] = jnp.where(ge, hib, mid)
            hi_ref[...] = jnp.where(ge, hi_ref[...], mid)
            return it + 1, any_row(clo != kf)

        lax.while_loop(bis_cond, bis_body, (jnp.int32(0), jnp.int32(1)))

        resolved = clo_ref[...] == kf
        thr_ref[...] = jnp.where(resolved, lo_ref[...], inf)
        clo_ref[...] = jnp.where(resolved, kf, 0.0)

        def peel_body(go):
            done = clo_ref[...] >= kf
            hi = hi_ref[...]
            hb = jnp.broadcast_to(hi, (QB, LANE))
            nxt = fold_chunks(lambda a, x, kp: jnp.maximum(a, jnp.where(x < hb, x, -inf)),
                              jnp.full((QB, LANE), -inf, jnp.float32))
            t = jnp.where(done, thr_ref[...], jnp.max(nxt, axis=-1, keepdims=True))
            cnt = count(lambda x, tb, kp: x >= tb, t)
            thr_ref[...] = t
            clo_ref[...] = cnt
            hi_ref[...] = jnp.where(done, hi, t)
            return any_row(cnt < kf)

        lax.while_loop(lambda go: go > 0, peel_body, any_row(clo_ref[...] < kf))

        jst_ref[...] = jnp.full_like(jst_ref, s_len)

        @pl.when(any_row(clo_ref[...] > kf) > 0)
        def _():
            thr = thr_ref[...]
            need = kf - count(lambda x, tb, kp: x > tb, thr)
            excess = clo_ref[...] > kf

            def tie_body(bi, j):
                cand = j + lax.shift_left(jnp.int32(1), (s_len.bit_length() - 1) - bi)
                cb = jnp.broadcast_to(cand, (QB, LANE))
                tb = jnp.broadcast_to(thr, (QB, LANE))
                acc = fold_chunks(
                    lambda a, x, kp: a + jnp.where(jnp.logical_and(x == tb, kp < cb), 1.0, 0.0),
                    jnp.zeros((QB, LANE), jnp.float32))
                below = jnp.sum(acc, axis=-1, keepdims=True)
                return jnp.where(below < need, cand, j)

            j = lax.fori_loop(0, s_len.bit_length(), tie_body, jnp.zeros((QB, 1), jnp.int32))
            jst_ref[...] = jnp.where(excess, j, s_len)

    qlat = []
    for h in range(HEADS):
        hs = slice(h * HEAD_DIM, (h + 1) * HEAD_DIM)
        ql = lax.dot_general(q_ref[:, hs], wuk_ref[h], _NT, preferred_element_type=jnp.float32)
        qlat.append((ql * HEAD_DIM ** -0.5).astype(jnp.bfloat16))
    qlat = jnp.concatenate(qlat, axis=0)
    m_ref[...] = jnp.full_like(m_ref, NEG_BIG)
    l_ref[...] = jnp.zeros_like(l_ref)
    acc_ref[...] = jnp.zeros_like(acc_ref)
    thr = thr_ref[...]
    jst = jst_ref[...]

    def abody(c, _):
        off = pl.multiple_of(c * kc, kc)
        cn = cn_ref[pl.ds(off, kc), :]
        sc = sc_ref[c]
        kpos = off + lax.broadcasted_iota(jnp.int32, (1, kc), 1)
        sel = jnp.logical_or(sc > thr, jnp.logical_and(sc == thr, kpos <= jst))
        logits = lax.dot_general(qlat, cn, _NT, preferred_element_type=jnp.float32)
        for h in range(HEADS):
            rs = slice(h * QB, (h + 1) * QB)
            s = jnp.where(sel, logits[rs, :], NEG_BIG)
            m_old = m_ref[rs, :]
            m_new = jnp.maximum(m_old, jnp.max(s, axis=-1, keepdims=True))
            alpha = jnp.exp(m_old - m_new)
            p = jnp.exp(s - m_new)
            l_ref[rs, :] = alpha * l_ref[rs, :] + jnp.sum(p, axis=-1, keepdims=True)
            pv = jnp.dot(p.astype(jnp.bfloat16), cn, preferred_element_type=jnp.float32)
            acc_ref[rs, :] = alpha * acc_ref[rs, :] + pv
            m_ref[rs, :] = m_new
        return 0

    lax.fori_loop(0, nch, abody, 0)

    for h in range(HEADS):
        hs = slice(h * HEAD_DIM, (h + 1) * HEAD_DIM)
        rs = slice(h * QB, (h + 1) * QB)
        o_lat = (acc_ref[rs, :] / l_ref[rs, :]).astype(jnp.bfloat16)
        o = jnp.dot(o_lat, wuv_ref[h], preferred_element_type=jnp.float32)
        o_ref[:, hs] = (o * _silu(z_ref[:, hs])).astype(o_ref.dtype)


def _dsa(p16b, p32b, kv_g, w_uk, w_uv):
    b, s, _ = p16b.shape
    kc = min(512, s)
    ksel = min(TOPK_MAX, s // 4)
    assert ksel % QB == 0 and s % kc == 0
    col = lambda: pltpu.VMEM((QB, 1), jnp.float32)
    full3 = pl.BlockSpec((HEADS, LATENT, HEAD_DIM), lambda bi, i: (0, 0, 0))
    return pl.pallas_call(
        functools.partial(_dsa_kernel, s_len=s, kc=kc, ksel=ksel, n_bisect=24),
        grid=(b, s // QB),
        in_specs=[pl.BlockSpec((None, QB, W), lambda bi, i: (bi, i, P16_BQ)),
                  pl.BlockSpec((None, QB, W), lambda bi, i: (bi, i, P16_QI)),
                  pl.BlockSpec((None, QB, LANE), lambda bi, i: (bi, i, P32_SM)),
                  pl.BlockSpec((None, QB, W), lambda bi, i: (bi, i, P32_BZ)),
                  pl.BlockSpec((None, s, LANE), lambda bi, i: (bi, 0, P16_KI)),
                  pl.BlockSpec((None, s, LANE), lambda bi, i: (bi, 0, P32_BC)),
                  pl.BlockSpec((1, LATENT), lambda bi, i: (0, 0)),
                  full3, full3],
        out_specs=pl.BlockSpec((None, QB, W), lambda bi, i: (bi, i, 0)),
        out_shape=jax.ShapeDtypeStruct((b, s, W), jnp.bfloat16),
        scratch_shapes=[pltpu.VMEM((s, LATENT), jnp.bfloat16),
                        pltpu.VMEM((s // kc, QB, kc), jnp.float32),
                        col(),
                        pltpu.VMEM((QB, 1), jnp.int32),
                        col(), col(), col(), col(),
                        pltpu.VMEM((HEADS * QB, 1), jnp.float32),
                        pltpu.VMEM((HEADS * QB, 1), jnp.float32),
                        pltpu.VMEM((HEADS * QB, LATENT), jnp.float32)],
        compiler_params=_cparams(("arbitrary", "arbitrary")),
        name="dsa_attn",
    )(p16b, p16b, p32b, p32b, p16b, p32b, kv_g.reshape(1, LATENT),
      w_uk.astype(jnp.bfloat16), w_uv.astype(jnp.bfloat16))


def _merge_kernel(ya_ref, yb_ref, yc_ref, yd_ref, g_ref, x_ref, wb_ref, wo_ref, gn_ref, h_ref, hn_ref):
    merged = None
    for n, y_ref in enumerate((ya_ref, yb_ref, yc_ref, yd_ref)):
        lifted = jnp.dot(y_ref[...], wb_ref[n], preferred_element_type=jnp.float32)
        term = _sigmoid(g_ref[:, n * D_MODEL:(n + 1) * D_MODEL]) * lifted
        merged = term if merged is None else merged + term
    h = x_ref[...] + jnp.dot(merged.astype(jnp.bfloat16), wo_ref[...], preferred_element_type=jnp.float32)
    h_ref[...] = h
    hn = h * lax.rsqrt(jnp.mean(h * h, axis=-1, keepdims=True) + EPS) * gn_ref[...]
    hn_ref[...] = hn.astype(hn_ref.dtype)


def _merge(ya, yb, yc, yd, p32, h, wb, wo, g_next, hn_dtype, tm=256):
    m, d = h.shape
    yblk = pl.BlockSpec((tm, W), lambda i: (i, 0))
    hblk = pl.BlockSpec((tm, d), lambda i: (i, 0))
    return pl.pallas_call(
        _merge_kernel,
        grid=(m // tm,),
        in_specs=[yblk, yblk, yblk, yblk,
                  pl.BlockSpec((tm, N_BRANCH * d), lambda i: (i, 0)),
                  hblk,
                  pl.BlockSpec((N_BRANCH, W, d), lambda i: (0, 0, 0)),
                  pl.BlockSpec((d, d), lambda i: (0, 0)),
                  pl.BlockSpec((1, d), lambda i: (0, 0))],
        out_specs=[hblk, hblk],
        out_shape=[jax.ShapeDtypeStruct((m, d), jnp.float32), jax.ShapeDtypeStruct((m, d), hn_dtype)],
        compiler_params=_cparams(("arbitrary",)),
        name="merge",
    )(ya, yb, yc, yd, p32, h, wb, wo, g_next.reshape(1, d))


def kernel(x, norm_g, w_in, gm_ln_g, gm_ln_b, gm_w_s, gm_b_s, dsa_kv_g, dsa_w_uk, dsa_w_uv,
           conv_w, fox_b_f, w_branch, w_out, final_g):
    b, s, d = x.shape
    depth = w_in.shape[0]
    m = b * s
    h = x.reshape(m, d)
    hn = _rmsnorm(h, norm_g[0], jnp.bfloat16)
    for l in range(depth):
        w16, w32, wvt = _prep_w_in(w_in[l])
        p16 = _matmul(hn, w16, jnp.bfloat16, tm=512, tn=N16, name="in_proj16")
        p32 = _matmul(hn, w32, jnp.float32, tm=512, tn=N32 // 5, name="in_proj32")
        vt = _matmul_t(hn, wvt, tm=min(KC, s), name="in_proj_vt")
        p16b = p16.reshape(b, s, N16)
        p32b = p32.reshape(b, s, N32)
        ya, yc = _mix_ac(p32b, gm_ln_g[l], gm_ln_b[l], gm_w_s[l], gm_b_s[l], conv_w[l])
        qaug, kaug = _fox_cum(p32b, fox_b_f[l])
        yd = _fox(p16b, p32b, qaug, kaug, vt)
        yb = _dsa(p16b, p32b, dsa_kv_g[l], dsa_w_uk[l], dsa_w_uv[l])
        last = l == depth - 1
        g_next = final_g if last else norm_g[l + 1]
        h, hn = _merge(ya.reshape(m, W), yb.reshape(m, W), yc.reshape(m, W), yd.reshape(m, W), p32, h,
                       w_branch[l].astype(jnp.bfloat16), w_out[l].astype(jnp.bfloat16), g_next,
                       jnp.float32 if last else jnp.bfloat16)
    return hn.reshape(b, s, d)
```

```python
import functools

import numpy as np
import jax
import jax.numpy as jnp
from jax import lax
from jax.experimental import pallas as pl
from jax.experimental.pallas import tpu as pltpu

D_MODEL = 1024
N_BRANCH = 4
W = 512
EPS = 1e-6
QB = 128
QBE = 256
KC = 512
FOLD = 64
GM_GROUPS = 4
GM_CHUNK = 128
HEADS = 4
HEAD_DIM = W // HEADS
LATENT = 128
IDX_HEADS = 4
IDX_DIM = 64
TOPK_MAX = 256
CONV_WIDTH = 3
LANE = 128
NEG_BIG = -1e30
VMEM_LIMIT = 56 * 1024 * 1024

P16_DQ, P16_DK, P16_BQ = 0, 1, 2
P16_QI = 3
P16_KI = 16
N16 = 17 * LANE
P32_AU, P32_AV, P32_AZ, P32_BZ = 8, 9, 10, 11
P32_CB, P32_CC, P32_CX, P32_CZ, P32_DZ = 12, 13, 14, 15, 16
P32_BC = 68
P32_SM = 69
N32 = 70 * LANE

_NT = (((1,), (1,)), ((), ()))


def _cparams(sem):
    return pltpu.CompilerParams(dimension_semantics=sem, vmem_limit_bytes=VMEM_LIMIT)


def _sigmoid(z):
    return 1.0 / (1.0 + jnp.exp(-z))


def _silu(z):
    return z * _sigmoid(z)


def _rmsnorm_kernel(x_ref, g_ref, o_ref):
    x = x_ref[...]
    y = x * lax.rsqrt(jnp.mean(x * x, axis=-1, keepdims=True) + EPS)
    o_ref[...] = (y * g_ref[...]).astype(o_ref.dtype)


def _rmsnorm(x2, g, out_dtype, tm=512):
    m, d = x2.shape
    return pl.pallas_call(
        _rmsnorm_kernel,
        grid=(m // tm,),
        in_specs=[pl.BlockSpec((tm, d), lambda i: (i, 0)), pl.BlockSpec((1, d), lambda i: (0, 0))],
        out_specs=pl.BlockSpec((tm, d), lambda i: (i, 0)),
        out_shape=jax.ShapeDtypeStruct((m, d), out_dtype),
        compiler_params=_cparams(("arbitrary",)),
        name="rmsnorm",
    )(x2, g.reshape(1, d))


def _matmul_kernel(a_ref, w_ref, o_ref):
    o_ref[...] = jnp.dot(a_ref[...], w_ref[...], preferred_element_type=jnp.float32).astype(o_ref.dtype)


def _matmul(a, w, out_dtype, tm, tn, name):
    m, k = a.shape
    n = w.shape[1]
    return pl.pallas_call(
        _matmul_kernel,
        grid=(n // tn, m // tm),
        in_specs=[pl.BlockSpec((tm, k), lambda j, i: (i, 0)), pl.BlockSpec((k, tn), lambda j, i: (0, j))],
        out_specs=pl.BlockSpec((tm, tn), lambda j, i: (i, j)),
        out_shape=jax.ShapeDtypeStruct((m, n), out_dtype),
        compiler_params=_cparams(("arbitrary", "arbitrary")),
        name=name,
    )(a, w)


def _prep_w_in(w):
    d = w.shape[0]
    sizes = (W, W, W,
             W, LATENT, IDX_HEADS * IDX_DIM, IDX_DIM, IDX_HEADS, W,
             W, W, W, W,
             W, W, W, HEADS, W,
             N_BRANCH * D_MODEL)
    parts, off = [], 0
    for s in sizes:
        parts.append(w[:, off:off + s])
        off += s
    (a_u, a_v, a_z, b_q, b_c, b_qi, b_ki, b_wi, b_z,
     c_b, c_c, c_x, c_z, d_q, d_k, d_v, d_f, d_z, gates) = parts
    zeros = lambda n: jnp.zeros((d, n), w.dtype)
    qi = []
    for h in range(IDX_HEADS):
        qi += [b_qi[:, h * IDX_DIM:(h + 1) * IDX_DIM], zeros(LANE - IDX_DIM)]
    w16 = jnp.concatenate([d_q, d_k, b_q] + qi + [b_ki, zeros(LANE - IDX_DIM)], axis=1)
    w32 = jnp.concatenate([gates, a_u, a_v, a_z, b_z, c_b, c_c, c_x, c_z, d_z, b_c,
                           d_f, b_wi, zeros(LANE - HEADS - IDX_HEADS)], axis=1)
    return w16.astype(jnp.bfloat16), w32.astype(jnp.bfloat16), jnp.transpose(d_v).astype(jnp.bfloat16)


def _matmul_t_kernel(wt_ref, a_ref, o_ref):
    o_ref[...] = lax.dot_general(wt_ref[...], a_ref[...], _NT,
                                 preferred_element_type=jnp.float32).astype(o_ref.dtype)


def _matmul_t(a, wt, tm, name):
    m, k = a.shape
    n = wt.shape[0]
    return pl.pallas_call(
        _matmul_t_kernel,
        grid=(m // tm,),
        in_specs=[pl.BlockSpec((n, k), lambda i: (0, 0)), pl.BlockSpec((tm, k), lambda i: (i, 0))],
        out_specs=pl.BlockSpec((None, n, tm), lambda i: (i, 0, 0)),
        out_shape=jax.ShapeDtypeStruct((m // tm, n, tm), jnp.bfloat16),
        compiler_params=_cparams(("arbitrary",)),
        name=name,
    )(wt, a)


def _mix_ac_kernel(au_ref, av_ref, az_ref, cb_ref, cc_ref, cx_ref, cz_ref,
                   lng_ref, lnb_ref, ws_ref, bs_ref, cw_ref, ya_ref, yc_ref, halo_ref, *, tt):
    t = pl.program_id(1)

    v = av_ref[...]
    mu = jnp.mean(v, axis=-1, keepdims=True)
    vc = v - mu
    var = jnp.mean(vc * vc, axis=-1, keepdims=True)
    vn = (vc * lax.rsqrt(var + EPS) * lng_ref[...] + lnb_ref[...]).astype(jnp.bfloat16)
    row = lax.broadcasted_iota(jnp.int32, (GM_CHUNK, GM_CHUNK), 0)
    col = lax.broadcasted_iota(jnp.int32, (GM_CHUNK, GM_CHUNK), 1)
    tril = row >= col
    wg = [jnp.where(tril, ws_ref[g], 0.0).astype(jnp.bfloat16) for g in range(GM_GROUPS)]
    for ch in range(tt // GM_CHUNK):
        rows = slice(ch * GM_CHUNK, (ch + 1) * GM_CHUNK)
        for g in range(GM_GROUPS):
            cols = slice(g * LANE, (g + 1) * LANE)
            mixed = jnp.dot(wg[g], vn[rows, cols], preferred_element_type=jnp.float32) + bs_ref[:, cols]
            ya_ref[rows, cols] = (au_ref[rows, cols] * mixed * _silu(az_ref[rows, cols])).astype(ya_ref.dtype)

    @pl.when(t == 0)
    def _():
        halo_ref[...] = jnp.zeros_like(halo_ref)

    y = cc_ref[...] * cx_ref[...]
    ext = jnp.concatenate([halo_ref[...], y], axis=0)
    conv = cw_ref[2:3, :] * y
    for j in range(CONV_WIDTH - 1):
        shift = CONV_WIDTH - 1 - j
        conv = conv + cw_ref[j:j + 1, :] * ext[8 - shift:8 - shift + tt, :]
    yc_ref[...] = (cb_ref[...] * conv * _silu(cz_ref[...])).astype(yc_ref.dtype)
    halo_ref[...] = y[tt - 8:, :]


def _mix_ac(p32b, ln_g, ln_b, w_s, b_s, conv_w, tt=256):
    b, s, _ = p32b.shape
    blk = lambda idx: pl.BlockSpec((None, tt, W), lambda bi, ti, idx=idx: (bi, ti, idx))
    full2 = lambda shp: pl.BlockSpec(shp, lambda bi, ti: (0,) * len(shp))
    bs_full = jnp.repeat(jnp.transpose(b_s), LANE, axis=1)
    out = jax.ShapeDtypeStruct((b, s, W), jnp.bfloat16)
    return pl.pallas_call(
        functools.partial(_mix_ac_kernel, tt=tt),
        grid=(b, s // tt),
        in_specs=[blk(P32_AU), blk(P32_AV), blk(P32_AZ), blk(P32_CB), blk(P32_CC), blk(P32_CX), blk(P32_CZ),
                  full2((1, W)), full2((1, W)), full2((GM_GROUPS, GM_CHUNK, GM_CHUNK)),
                  full2((GM_CHUNK, W)), full2((CONV_WIDTH, W))],
        out_specs=[pl.BlockSpec((None, tt, W), lambda bi, ti: (bi, ti, 0))] * 2,
        out_shape=[out, out],
        scratch_shapes=[pltpu.VMEM((8, W), jnp.float32)],
        compiler_params=_cparams(("arbitrary", "arbitrary")),
        name="mix_ac",
    )(p32b, p32b, p32b, p32b, p32b, p32b, p32b,
      ln_g.reshape(1, W), ln_b.reshape(1, W), w_s, bs_full, conv_w)


def _split3(x):
    hi = x.astype(jnp.bfloat16)
    r1 = x - hi.astype(jnp.float32)
    mid = r1.astype(jnp.bfloat16)
    lo = (r1 - mid.astype(jnp.float32)).astype(jnp.bfloat16)
    return hi, mid, lo


def _aug_placement():
    pq = np.zeros((3 * LANE, W), np.float32)
    pk = np.zeros((3 * LANE, W), np.float32)
    cq = np.zeros((1, W), np.float32)
    ck = np.zeros((1, W), np.float32)
    for h in range(HEADS):
        for j in range(3):
            pq[j * LANE + h, h * HEAD_DIM + j] = 1.0
            cq[0, h * HEAD_DIM + 3 + j] = 1.0
            pk[j * LANE + h, h * HEAD_DIM + 3 + j] = -1.0
            ck[0, h * HEAD_DIM + j] = 1.0
    return (jnp.asarray(pq, jnp.bfloat16), jnp.asarray(pk, jnp.bfloat16), jnp.asarray(cq), jnp.asarray(ck))


def _fox_cum_kernel(f_ref, bias_ref, pq_ref, pk_ref, cq_ref, ck_ref, qa_ref, ka_ref, carry_ref, *, s):
    row = lax.broadcasted_iota(jnp.int32, (LANE, LANE), 0)
    col = lax.broadcasted_iota(jnp.int32, (LANE, LANE), 1)
    ones_tril = jnp.where(row >= col, 1.0, 0.0).astype(jnp.bfloat16)
    carry_ref[...] = jnp.zeros_like(carry_ref)

    def body(c, _):
        off = pl.multiple_of(c * LANE, LANE)
        x = f_ref[pl.ds(off, LANE), :] + bias_ref[...]
        ls = jnp.minimum(x, 0.0) - jnp.log1p(jnp.exp(-jnp.abs(x)))
        hi, mid, lo = _split3(ls)
        dot = lambda p: jnp.dot(ones_tril, p, preferred_element_type=jnp.float32)
        cs = (dot(hi) + dot(mid)) + dot(lo) + carry_ref[0:1, :]
        carry_ref[0:1, :] = cs[LANE - 1:LANE, :]
        parts = jnp.concatenate(_split3(cs), axis=1)
        qa = jnp.dot(parts, pq_ref[...], preferred_element_type=jnp.float32) + cq_ref[...]
        ka = jnp.dot(parts, pk_ref[...], preferred_element_type=jnp.float32) + ck_ref[...]
        qa_ref[pl.ds(off, LANE), :] = qa.astype(qa_ref.dtype)
        ka_ref[pl.ds(off, LANE), :] = ka.astype(ka_ref.dtype)
        return 0

    lax.fori_loop(0, s // LANE, body, 0)


def _fox_cum(p32b, b_f):
    b, s, _ = p32b.shape
    bias = jnp.zeros((1, LANE), jnp.float32).at[0, :HEADS].set(b_f)
    pq, pk, cq, ck = _aug_placement()
    const = lambda shp: pl.BlockSpec(shp, lambda bi: (0, 0))
    out = jax.ShapeDtypeStruct((b, s, W), jnp.bfloat16)
    return pl.pallas_call(
        functools.partial(_fox_cum_kernel, s=s),
        grid=(b,),
        in_specs=[pl.BlockSpec((None, s, LANE), lambda bi: (bi, 0, P32_SM)), const((1, LANE)),
                  const((3 * LANE, W)), const((3 * LANE, W)), const((1, W)), const((1, W))],
        out_specs=[pl.BlockSpec((None, s, W), lambda bi: (bi, 0, 0))] * 2,
        out_shape=[out, out],
        scratch_shapes=[pltpu.VMEM((8, LANE), jnp.float32)],
        compiler_params=_cparams(("arbitrary",)),
        name="fox_cum",
    )(p32b, bias, pq, pk, cq, ck)


def _num_chunks(i, kc):
    return ((i + 1) * QB + kc - 1) // kc


def _fox_kernel(q_ref, qa_ref, k_ref, ka_ref, vt_ref, z_ref, o_ref, qf_ref, m_ref, l_ref, acc_ref, *, kc):
    i = pl.program_id(1)
    ndiag = (i * QBE) // kc
    scale = HEAD_DIM ** -0.5
    for h in range(HEADS):
        hs = slice(h * HEAD_DIM, (h + 1) * HEAD_DIM)
        qs = (q_ref[:, hs].astype(jnp.float32) * scale).astype(jnp.bfloat16)
        qf_ref[h] = jnp.concatenate([qs, qa_ref[:, hs]], axis=1)
    m_ref[...] = jnp.full_like(m_ref, NEG_BIG)
    l_ref[...] = jnp.zeros_like(l_ref)
    acc_ref[...] = jnp.zeros_like(acc_ref)

    def step(c, masked):
        off = pl.multiple_of(c * kc, kc)

        def scores(h):
            hs = slice(h * HEAD_DIM, (h + 1) * HEAD_DIM)
            kf = jnp.concatenate([k_ref[pl.ds(off, kc), hs], ka_ref[pl.ds(off, kc), hs]], axis=1)
            st = lax.dot_general(kf, qf_ref[h], _NT, preferred_element_type=jnp.float32)
            if masked:
                kpos = off + lax.broadcasted_iota(jnp.int32, (kc, QBE), 0)
                qpos = i * QBE + lax.broadcasted_iota(jnp.int32, (kc, QBE), 1)
                st = jnp.where(kpos <= qpos, st, NEG_BIG)
            return st

        st = [scores(h) for h in range(HEADS)]
        for h in range(HEADS):
            hs = slice(h * HEAD_DIM, (h + 1) * HEAD_DIM)
            m_old = m_ref[h]
            m_new = jnp.maximum(m_old, jnp.max(st[h], axis=0, keepdims=True))
            alpha = jnp.exp(m_old - m_new)
            p = jnp.exp(st[h] - m_new)
            l_ref[h] = alpha * l_ref[h] + jnp.sum(p, axis=0, keepdims=True)
            pv = jnp.dot(vt_ref[c, hs, :], p.astype(jnp.bfloat16), preferred_element_type=jnp.float32)
            acc_ref[h] = alpha * acc_ref[h] + pv
            m_ref[h] = m_new

    def body(c, _):
        step(c, False)
        return 0

    lax.fori_loop(0, ndiag, body, 0)
    step(ndiag, True)
    for h in range(HEADS):
        hs = slice(h * HEAD_DIM, (h + 1) * HEAD_DIM)
        o = jnp.transpose(acc_ref[h] / l_ref[h])
        o_ref[:, hs] = (o * _silu(z_ref[:, hs])).astype(o_ref.dtype)


def _fox(p16b, p32b, qaug, kaug, vt):
    b, s, _ = p16b.shape
    kc = vt.shape[-1]
    assert s % kc == 0 and kc % QBE == 0
    nck = s // kc
    vt = vt.reshape(b, nck, W, kc)
    row = lambda: pltpu.VMEM((HEADS, 1, QBE), jnp.float32)
    return pl.pallas_call(
        functools.partial(_fox_kernel, kc=kc),
        grid=(b, s // QBE),
        in_specs=[pl.BlockSpec((None, QBE, W), lambda bi, i: (bi, i, P16_DQ)),
                  pl.BlockSpec((None, QBE, W), lambda bi, i: (bi, i, 0)),
                  pl.BlockSpec((None, s, W), lambda bi, i: (bi, 0, P16_DK)),
                  pl.BlockSpec((None, s, W), lambda bi, i: (bi, 0, 0)),
                  pl.BlockSpec((None, nck, W, kc), lambda bi, i: (bi, 0, 0, 0)),
                  pl.BlockSpec((None, QBE, W), lambda bi, i: (bi, i, P32_DZ))],
        out_specs=pl.BlockSpec((None, QBE, W), lambda bi, i: (bi, i, 0)),
        out_shape=jax.ShapeDtypeStruct((b, s, W), jnp.bfloat16),
        scratch_shapes=[pltpu.VMEM((HEADS, QBE, 2 * HEAD_DIM), jnp.bfloat16),
                        row(), row(),
                        pltpu.VMEM((HEADS, HEAD_DIM, QBE), jnp.float32)],
        compiler_params=_cparams(("arbitrary", "arbitrary")),
        name="fox_attn",
    )(p16b, qaug, p16b, kaug, vt, p32b)


def _dsa_kernel(q_ref, qi_ref, sm_ref, z_ref, ki_ref, c_ref, kvg_ref, wuk_ref, wuv_ref, o_ref,
                cn_ref, sc_ref, thr_ref, jst_ref, lo_ref, hib_ref, hi_ref, clo_ref,
                m_ref, l_ref, acc_ref, *, s_len, kc, ksel, n_bisect):
    i = pl.program_id(1)
    nch = _num_chunks(i, kc)
    qpos = i * QB + lax.broadcasted_iota(jnp.int32, (QB, 1), 0)
    kf = float(ksel)
    inf = float("inf")

    @pl.when(i == 0)
    def _():
        def nbody(c, _):
            off = pl.multiple_of(c * kc, kc)
            x = c_ref[pl.ds(off, kc), :]
            y = x * lax.rsqrt(jnp.mean(x * x, axis=-1, keepdims=True) + EPS) * kvg_ref[...]
            cn_ref[pl.ds(off, kc), :] = y.astype(cn_ref.dtype)
            return 0
        lax.fori_loop(0, s_len // kc, nbody, 0)

    wi = sm_ref[:, HEADS:HEADS + IDX_HEADS] * (IDX_HEADS ** -0.5 * IDX_DIM ** -0.5)

    def sbody(c, _):
        off = pl.multiple_of(c * kc, kc)
        kk = ki_ref[pl.ds(off, kc), :]
        sc = jnp.zeros((QB, kc), jnp.float32)
        for h in range(IDX_HEADS):
            d = lax.dot_general(qi_ref[:, h * LANE:(h + 1) * LANE], kk, _NT, preferred_element_type=jnp.float32)
            sc = sc + wi[:, h:h + 1] * jnp.maximum(d, 0.0)
        kpos = off + lax.broadcasted_iota(jnp.int32, (1, kc), 1)
        sc_ref[c] = jnp.where(kpos <= qpos, sc, -inf)
        return 0

    lax.fori_loop(0, nch, sbody, 0)

    def fold_chunks(fn, init):
        def cbody(c, acc):
            off = pl.multiple_of(c * kc, kc)
            sc = sc_ref[c]
            for j in range(kc // LANE):
                kpos = off + j * LANE + lax.broadcasted_iota(jnp.int32, (1, LANE), 1)
                acc = fn(acc, sc[:, j * LANE:(j + 1) * LANE], kpos)
            return acc
        return lax.fori_loop(0, nch, cbody, init)

    def count(pred, t):
        tb = jnp.broadcast_to(t, (QB, LANE))
        acc = fold_chunks(lambda a, x, kp: a + jnp.where(pred(x, tb, kp), 1.0, 0.0),
                          jnp.zeros((QB, LANE), jnp.float32))
        return jnp.sum(acc, axis=-1, keepdims=True)

    def any_row(flag):
        return (jnp.max(jnp.where(flag, 1.0, 0.0)) > 0.5).astype(jnp.int32)

    thr_ref[...] = jnp.full_like(thr_ref, -inf)
    jst_ref[...] = jnp.full_like(jst_ref, -1)

    @pl.when(i * QB >= ksel)
    def _():
        mn0 = jnp.full((QB, LANE), inf, jnp.float32)
        mx0 = jnp.full((QB, LANE), -inf, jnp.float32)
        mn, mx = fold_chunks(
            lambda a, x, kp: (jnp.minimum(a[0], jnp.where(x == -inf, inf, x)), jnp.maximum(a[1], x)),
            (mn0, mx0))
        lo_ref[...] = jnp.min(mn, axis=-1, keepdims=True)
        hib_ref[...] = jnp.max(mx, axis=-1, keepdims=True)
        hi_ref[...] = jnp.full_like(hi_ref, inf)
        clo_ref[...] = (qpos + 1).astype(jnp.float32)

        def bis_cond(st):
            it, go = st
            return jnp.logical_and(it < n_bisect, go > 0)

        def bis_body(st):
            it, _ = st
            lo, hib = lo_ref[...], hib_ref[...]
            mid = 0.5 * lo + 0.5 * hib
            cnt = count(lambda x, tb, kp: x >= tb, mid)
            ge = cnt >= kf
            lo_ref[...] = jnp.where(ge, mid, lo)
            clo = jnp.where(ge, cnt, clo_ref[...])
            clo_ref[...] = clo
            hib_ref[...] = jnp.where(ge, hib, mid)
            hi_ref[...] = jnp.where(ge, hi_ref[...], mid)
            return it + 1, any_row(clo != kf)

        lax.while_loop(bis_cond, bis_body, (jnp.int32(0), jnp.int32(1)))

        resolved = clo_ref[...] == kf
        thr_ref[...] = jnp.where(resolved, lo_ref[...], inf)
        clo_ref[...] = jnp.where(resolved, kf, 0.0)

        def peel_body(go):
            done = clo_ref[...] >= kf
            hi = hi_ref[...]
            hb = jnp.broadcast_to(hi, (QB, LANE))
            nxt = fold_chunks(lambda a, x, kp: jnp.maximum(a, jnp.where(x < hb, x, -inf)),
                              jnp.full((QB, LANE), -inf, jnp.float32))
            t = jnp.where(done, thr_ref[...], jnp.max(nxt, axis=-1, keepdims=True))
            cnt = count(lambda x, tb, kp: x >= tb, t)
            thr_ref[...] = t
            clo_ref[...] = cnt
            hi_ref[...] = jnp.where(done, hi, t)
            return any_row(cnt < kf)

        lax.while_loop(lambda go: go > 0, peel_body, any_row(clo_ref[...] < kf))

        jst_ref[...] = jnp.full_like(jst_ref, s_len)

        @pl.when(any_row(clo_ref[...] > kf) > 0)
        def _():
            thr = thr_ref[...]
            need = kf - count(lambda x, tb, kp: x > tb, thr)
            excess = clo_ref[...] > kf

            def tie_body(bi, j):
                cand = j + lax.shift_left(jnp.int32(1), (s_len.bit_length() - 1) - bi)
                cb = jnp.broadcast_to(cand, (QB, LANE))
                tb = jnp.broadcast_to(thr, (QB, LANE))
                acc = fold_chunks(
                    lambda a, x, kp: a + jnp.where(jnp.logical_and(x == tb, kp < cb), 1.0, 0.0),
                    jnp.zeros((QB, LANE), jnp.float32))
                below = jnp.sum(acc, axis=-1, keepdims=True)
                return jnp.where(below < need, cand, j)

            j = lax.fori_loop(0, s_len.bit_length(), tie_body, jnp.zeros((QB, 1), jnp.int32))
            jst_ref[...] = jnp.where(excess, j, s_len)

    qlat = []
    for h in range(HEADS):
        hs = slice(h * HEAD_DIM, (h + 1) * HEAD_DIM)
        ql = lax.dot_general(q_ref[:, hs], wuk_ref[h], _NT, preferred_element_type=jnp.float32)
        qlat.append((ql * HEAD_DIM ** -0.5).astype(jnp.bfloat16))
    qlat = jnp.concatenate(qlat, axis=0)
    m_ref[...] = jnp.full_like(m_ref, NEG_BIG)
    l_ref[...] = jnp.zeros_like(l_ref)
    acc_ref[...] = jnp.zeros_like(acc_ref)
    thr = thr_ref[...]
    jst = jst_ref[...]

    def abody(c, _):
        off = pl.multiple_of(c * kc, kc)
        cn = cn_ref[pl.ds(off, kc), :]
        sc = sc_ref[c]
        kpos = off + lax.broadcasted_iota(jnp.int32, (1, kc), 1)
        sel = jnp.logical_or(sc > thr, jnp.logical_and(sc == thr, kpos <= jst))
        logits = lax.dot_general(qlat, cn, _NT, preferred_element_type=jnp.float32)
        for h in range(HEADS):
            rs = slice(h * QB, (h + 1) * QB)
            s = jnp.where(sel, logits[rs, :], NEG_BIG)
            m_old = m_ref[rs, :]
            m_new = jnp.maximum(m_old, jnp.max(s, axis=-1, keepdims=True))
            alpha = jnp.exp(m_old - m_new)
            p = jnp.exp(s - m_new)
            l_ref[rs, :] = alpha * l_ref[rs, :] + jnp.sum(p, axis=-1, keepdims=True)
            pv = jnp.dot(p.astype(jnp.bfloat16), cn, preferred_element_type=jnp.float32)
            acc_ref[rs, :] = alpha * acc_ref[rs, :] + pv
            m_ref[rs, :] = m_new
        return 0

    lax.fori_loop(0, nch, abody, 0)

    for h in range(HEADS):
        hs = slice(h * HEAD_DIM, (h + 1) * HEAD_DIM)
        rs = slice(h * QB, (h + 1) * QB)
        o_lat = (acc_ref[rs, :] / l_ref[rs, :]).astype(jnp.bfloat16)
        o = jnp.dot(o_lat, wuv_ref[h], preferred_element_type=jnp.float32)
        o_ref[:, hs] = (o * _silu(z_ref[:, hs])).astype(o_ref.dtype)


def _dsa(p16b, p32b, kv_g, w_uk, w_uv):
    b, s, _ = p16b.shape
    kc = min(512, s)
    ksel = min(TOPK_MAX, s // 4)
    assert ksel % QB == 0 and s % kc == 0
    col = lambda: pltpu.VMEM((QB, 1), jnp.float32)
    full3 = pl.BlockSpec((HEADS, LATENT, HEAD_DIM), lambda bi, i: (0, 0, 0))
    return pl.pallas_call(
        functools.partial(_dsa_kernel, s_len=s, kc=kc, ksel=ksel, n_bisect=24),
        grid=(b, s // QB),
        in_specs=[pl.BlockSpec((None, QB, W), lambda bi, i: (bi, i, P16_BQ)),
                  pl.BlockSpec((None, QB, W), lambda bi, i: (bi, i, P16_QI)),
                  pl.BlockSpec((None, QB, LANE), lambda bi, i: (bi, i, P32_SM)),
                  pl.BlockSpec((None, QB, W), lambda bi, i: (bi, i, P32_BZ)),
                  pl.BlockSpec((None, s, LANE), lambda bi, i: (bi, 0, P16_KI)),
                  pl.BlockSpec((None, s, LANE), lambda bi, i: (bi, 0, P32_BC)),
                  pl.BlockSpec((1, LATENT), lambda bi, i: (0, 0)),
                  full3, full3],
        out_specs=pl.BlockSpec((None, QB, W), lambda bi, i: (bi, i, 0)),
        out_shape=jax.ShapeDtypeStruct((b, s, W), jnp.bfloat16),
        scratch_shapes=[pltpu.VMEM((s, LATENT), jnp.bfloat16),
                        pltpu.VMEM((s // kc, QB, kc), jnp.float32),
                        col(),
                        pltpu.VMEM((QB, 1), jnp.int32),
                        col(), col(), col(), col(),
                        pltpu.VMEM((HEADS * QB, 1), jnp.float32),
                        pltpu.VMEM((HEADS * QB, 1), jnp.float32),
                        pltpu.VMEM((HEADS * QB, LATENT), jnp.float32)],
        compiler_params=_cparams(("arbitrary", "arbitrary")),
        name="dsa_attn",
    )(p16b, p16b, p32b, p32b, p16b, p32b, kv_g.reshape(1, LATENT),
      w_uk.astype(jnp.bfloat16), w_uv.astype(jnp.bfloat16))


def _dsat_kernel(q_ref, qi_ref, sm_ref, z_ref, ki_ref, c_ref, kvg_ref, wuk_ref, wuv_ref, o_ref,
                 cn_ref, cnt_ref, sc_ref, qia_ref, qlat_ref, thr_ref, jst_ref, lo_ref, hib_ref, hi_ref, clo_ref,
                 m_ref, l_ref, acc_ref, *, s_len, kc, ksel, n_bisect):
    i = pl.program_id(1)
    ndiag = (i * QBE) // kc
    nch = ndiag + 1
    qpos = i * QBE + lax.broadcasted_iota(jnp.int32, (1, QBE), 1)
    kf = float(ksel)
    inf = float("inf")

    def key_pos(c):
        return c * kc + lax.broadcasted_iota(jnp.int32, (kc, QBE), 0)

    @pl.when(i == 0)
    def _():
        def nbody(c, _):
            off = pl.multiple_of(c * kc, kc)
            x = c_ref[pl.ds(off, kc), :]
            y = x * lax.rsqrt(jnp.mean(x * x, axis=-1, keepdims=True) + EPS) * kvg_ref[...]
            cn_ref[pl.ds(off, kc), :] = y.astype(cn_ref.dtype)
            cnt_ref[c] = jnp.transpose(y).astype(cnt_ref.dtype)
            return 0
        lax.fori_loop(0, s_len // kc, nbody, 0)

    wit = jnp.transpose(sm_ref[...])[HEADS:HEADS + IDX_HEADS, :] * (IDX_HEADS ** -0.5 * IDX_DIM ** -0.5)
    for h in range(IDX_HEADS):
        qia_ref[h * QBE:(h + 1) * QBE, :] = qi_ref[:, h * LANE:(h + 1) * LANE]

    def score_chunk(c, masked):
        off = pl.multiple_of(c * kc, kc)
        d = lax.dot_general(ki_ref[pl.ds(off, kc), :], qia_ref[...], _NT,
                            preferred_element_type=jnp.float32)
        sc = None
        for h in range(IDX_HEADS):
            term = wit[h:h + 1, :] * jnp.maximum(d[:, h * QBE:(h + 1) * QBE], 0.0)
            sc = term if sc is None else sc + term
        if masked:
            sc = jnp.where(key_pos(c) <= qpos, sc, -inf)
        sc_ref[c] = sc

    def sbody(c, _):
        score_chunk(c, False)
        return 0

    lax.fori_loop(0, ndiag, sbody, 0)
    score_chunk(ndiag, True)

    def fold_slabs(op, fn, init_val):
        def cbody(c, acc):
            x = sc_ref[c]
            for j in range(kc // FOLD):
                acc = op(acc, fn(x[j * FOLD:(j + 1) * FOLD, :], c, j * FOLD))
            return acc
        return lax.fori_loop(0, nch, cbody, jnp.full((FOLD, QBE), init_val, jnp.float32))

    def count(pred, t):
        acc = fold_slabs(jnp.add, lambda x, c, r0: jnp.where(pred(x, t), 1.0, 0.0), 0.0)
        return jnp.sum(acc, axis=0, keepdims=True)

    def any_lane(flag):
        return (jnp.max(jnp.where(flag, 1.0, 0.0)) > 0.5).astype(jnp.int32)

    searched = qpos + 1 > ksel
    thr_ref[...] = jnp.full_like(thr_ref, -inf)
    jst_ref[...] = jnp.full_like(jst_ref, -1)

    @pl.when((i + 1) * QBE > ksel)
    def _():
        mn = fold_slabs(jnp.minimum, lambda x, c, r0: jnp.where(x == -inf, inf, x), inf)
        mx = fold_slabs(jnp.maximum, lambda x, c, r0: x, -inf)
        lo_ref[...] = jnp.min(mn, axis=0, keepdims=True)
        hib_ref[...] = jnp.max(mx, axis=0, keepdims=True)
        hi_ref[...] = jnp.full_like(hi_ref, inf)
        clo_ref[...] = (qpos + 1).astype(jnp.float32)

        def bis_body(_, carry):
            lo, hib = lo_ref[...], hib_ref[...]
            mid = 0.5 * lo + 0.5 * hib
            cnt = count(lambda x, t: x >= t, mid)
            ge = cnt >= kf
            lo_ref[...] = jnp.where(ge, mid, lo)
            clo_ref[...] = jnp.where(ge, cnt, clo_ref[...])
            hib_ref[...] = jnp.where(ge, hib, mid)
            hi_ref[...] = jnp.where(ge, hi_ref[...], mid)
            return carry

        lax.fori_loop(0, n_bisect, bis_body, 0)

        resolved = jnp.logical_or(clo_ref[...] == kf, jnp.logical_not(searched))
        thr_ref[...] = jnp.where(resolved, lo_ref[...], inf)
        clo_ref[...] = jnp.where(resolved, kf, 0.0)

        def peel_body(go):
            done = clo_ref[...] >= kf
            hi = hi_ref[...]
            nxt = fold_slabs(jnp.maximum, lambda x, c, r0: jnp.where(x < hi, x, -inf), -inf)
            t = jnp.where(done, thr_ref[...], jnp.max(nxt, axis=0, keepdims=True))
            cnt = count(lambda x, tt: x >= tt, t)
            thr_ref[...] = t
            clo_ref[...] = jnp.where(done, clo_ref[...], cnt)
            hi_ref[...] = jnp.where(done, hi, t)
            return any_lane(jnp.logical_and(jnp.logical_not(done), cnt < kf))

        lax.while_loop(lambda go: go > 0, peel_body, any_lane(clo_ref[...] < kf))

        jst_ref[...] = jnp.where(searched, s_len, -1)
        thr_ref[...] = jnp.where(searched, thr_ref[...], -inf)
        excess = jnp.logical_and(searched, clo_ref[...] > kf)

        @pl.when(any_lane(excess) > 0)
        def _():
            thr = thr_ref[...]
            need = kf - count(lambda x, t: x > t, thr)
            nbits = s_len.bit_length()

            def tie_body(bi, j):
                cand = j + lax.shift_left(jnp.int32(1), (nbits - 1) - bi)
                def tied_below(x, c, r0):
                    kpos = c * kc + r0 + lax.broadcasted_iota(jnp.int32, (FOLD, QBE), 0)
                    return jnp.where(jnp.logical_and(x == thr, kpos < cand), 1.0, 0.0)
                below = jnp.sum(fold_slabs(jnp.add, tied_below, 0.0), axis=0, keepdims=True)
                return jnp.where(below < need, cand, j)

            j = lax.fori_loop(0, nbits, tie_body, jnp.zeros((1, QBE), jnp.int32))
            jst_ref[...] = jnp.where(excess, j, jst_ref[...])

    for h in range(HEADS):
        hs = slice(h * HEAD_DIM, (h + 1) * HEAD_DIM)
        ql = lax.dot_general(q_ref[:, hs], wuk_ref[h], _NT, preferred_element_type=jnp.float32)
        qlat_ref[h * QBE:(h + 1) * QBE, :] = (ql * HEAD_DIM ** -0.5).astype(qlat_ref.dtype)
    m_ref[...] = jnp.full_like(m_ref, NEG_BIG)
    l_ref[...] = jnp.zeros_like(l_ref)
    acc_ref[...] = jnp.zeros_like(acc_ref)
    thr = thr_ref[...]
    jst = jst_ref[...]

    def abody(c, _):
        off = pl.multiple_of(c * kc, kc)
        logits = lax.dot_general(cn_ref[pl.ds(off, kc), :], qlat_ref[...], _NT,
                                 preferred_element_type=jnp.float32)
        sc = sc_ref[c]
        sel = jnp.logical_or(sc > thr, jnp.logical_and(sc == thr, key_pos(c) <= jst))
        s = jnp.concatenate([jnp.where(sel, logits[:, h * QBE:(h + 1) * QBE], NEG_BIG) for h in range(HEADS)],
                            axis=1)
        m_old = m_ref[...]
        m_new = jnp.maximum(m_old, jnp.max(s, axis=0, keepdims=True))
        alpha = jnp.exp(m_old - m_new)
        p = jnp.exp(s - m_new)
        l_ref[...] = alpha * l_ref[...] + jnp.sum(p, axis=0, keepdims=True)
        pv = jnp.dot(cnt_ref[c], p.astype(jnp.bfloat16), preferred_element_type=jnp.float32)
        acc_ref[...] = alpha * acc_ref[...] + pv
        m_ref[...] = m_new
        return 0

    lax.fori_loop(0, nch, abody, 0)

    for h in range(HEADS):
        hs = slice(h * HEAD_DIM, (h + 1) * HEAD_DIM)
        qs = slice(h * QBE, (h + 1) * QBE)
        o_lat = jnp.transpose(acc_ref[:, qs] / l_ref[:, qs]).astype(jnp.bfloat16)
        o = jnp.dot(o_lat, wuv_ref[h], preferred_element_type=jnp.float32)
        o_ref[:, hs] = (o * _silu(z_ref[:, hs])).astype(o_ref.dtype)


def _dsat(p16b, p32b, kv_g, w_uk, w_uv):
    b, s, _ = p16b.shape
    kc = min(KC, s)
    ksel = min(TOPK_MAX, s // 4)
    assert s % kc == 0 and kc % QBE == 0
    nq = HEADS * QBE
    row = lambda n: pltpu.VMEM((1, n), jnp.float32)
    full3 = pl.BlockSpec((HEADS, LATENT, HEAD_DIM), lambda bi, i: (0, 0, 0))
    return pl.pallas_call(
        functools.partial(_dsat_kernel, s_len=s, kc=kc, ksel=ksel, n_bisect=16),
        grid=(b, s // QBE),
        in_specs=[pl.BlockSpec((None, QBE, W), lambda bi, i: (bi, i, P16_BQ)),
                  pl.BlockSpec((None, QBE, W), lambda bi, i: (bi, i, P16_QI)),
                  pl.BlockSpec((None, QBE, LANE), lambda bi, i: (bi, i, P32_SM)),
                  pl.BlockSpec((None, QBE, W), lambda bi, i: (bi, i, P32_BZ)),
                  pl.BlockSpec((None, s, LANE), lambda bi, i: (bi, 0, P16_KI)),
                  pl.BlockSpec((None, s, LANE), lambda bi, i: (bi, 0, P32_BC)),
                  pl.BlockSpec((1, LATENT), lambda bi, i: (0, 0)),
                  full3, full3],
        out_specs=pl.BlockSpec((None, QBE, W), lambda bi, i: (bi, i, 0)),
        out_shape=jax.ShapeDtypeStruct((b, s, W), jnp.bfloat16),
        scratch_shapes=[pltpu.VMEM((s, LATENT), jnp.bfloat16),
                        pltpu.VMEM((s // kc, LATENT, kc), jnp.bfloat16),
                        pltpu.VMEM((s // kc, kc, QBE), jnp.float32),
                        pltpu.VMEM((IDX_HEADS * QBE, LANE), jnp.bfloat16),
                        pltpu.VMEM((nq, LATENT), jnp.bfloat16),
                        row(QBE),
                        pltpu.VMEM((1, QBE), jnp.int32),
                        row(QBE), row(QBE), row(QBE), row(QBE),
                        row(nq), row(nq),
                        pltpu.VMEM((LATENT, nq), jnp.float32)],
        compiler_params=_cparams(("arbitrary", "arbitrary")),
        name="dsa_attn",
    )(p16b, p16b, p32b, p32b, p16b, p32b, kv_g.reshape(1, LATENT),
      w_uk.astype(jnp.bfloat16), w_uv.astype(jnp.bfloat16))


def _merge_kernel(ya_ref, yb_ref, yc_ref, yd_ref, g_ref, x_ref, wb_ref, wo_ref, gn_ref, h_ref, hn_ref):
    merged = None
    for n, y_ref in enumerate((ya_ref, yb_ref, yc_ref, yd_ref)):
        lifted = jnp.dot(y_ref[...], wb_ref[n], preferred_element_type=jnp.float32)
        term = _sigmoid(g_ref[:, n * D_MODEL:(n + 1) * D_MODEL]) * lifted
        merged = term if merged is None else merged + term
    h = x_ref[...] + jnp.dot(merged.astype(jnp.bfloat16), wo_ref[...], preferred_element_type=jnp.float32)
    h_ref[...] = h
    hn = h * lax.rsqrt(jnp.mean(h * h, axis=-1, keepdims=True) + EPS) * gn_ref[...]
    hn_ref[...] = hn.astype(hn_ref.dtype)


def _merge(ya, yb, yc, yd, p32, h, wb, wo, g_next, hn_dtype, tm=256):
    m, d = h.shape
    yblk = pl.BlockSpec((tm, W), lambda i: (i, 0))
    hblk = pl.BlockSpec((tm, d), lambda i: (i, 0))
    return pl.pallas_call(
        _merge_kernel,
        grid=(m // tm,),
        in_specs=[yblk, yblk, yblk, yblk,
                  pl.BlockSpec((tm, N_BRANCH * d), lambda i: (i, 0)),
                  hblk,
                  pl.BlockSpec((N_BRANCH, W, d), lambda i: (0, 0, 0)),
                  pl.BlockSpec((d, d), lambda i: (0, 0)),
                  pl.BlockSpec((1, d), lambda i: (0, 0))],
        out_specs=[hblk, hblk],
        out_shape=[jax.ShapeDtypeStruct((m, d), jnp.float32), jax.ShapeDtypeStruct((m, d), hn_dtype)],
        compiler_params=_cparams(("arbitrary",)),
        name="merge",
    )(ya, yb, yc, yd, p32, h, wb, wo, g_next.reshape(1, d))


def kernel(x, norm_g, w_in, gm_ln_g, gm_ln_b, gm_w_s, gm_b_s, dsa_kv_g, dsa_w_uk, dsa_w_uv,
           conv_w, fox_b_f, w_branch, w_out, final_g):
    b, s, d = x.shape
    depth = w_in.shape[0]
    m = b * s
    h = x.reshape(m, d)
    hn = _rmsnorm(h, norm_g[0], jnp.bfloat16)
    for l in range(depth):
        w16, w32, wvt = _prep_w_in(w_in[l])
        p16 = _matmul(hn, w16, jnp.bfloat16, tm=512, tn=N16, name="in_proj16")
        p32 = _matmul(hn, w32, jnp.float32, tm=512, tn=N32 // 5, name="in_proj32")
        vt = _matmul_t(hn, wvt, tm=min(KC, s), name="in_proj_vt")
        p16b = p16.reshape(b, s, N16)
        p32b = p32.reshape(b, s, N32)
        ya, yc = _mix_ac(p32b, gm_ln_g[l], gm_ln_b[l], gm_w_s[l], gm_b_s[l], conv_w[l])
        qaug, kaug = _fox_cum(p32b, fox_b_f[l])
        yd = _fox(p16b, p32b, qaug, kaug, vt)
        yb = _dsat(p16b, p32b, dsa_kv_g[l], dsa_w_uk[l], dsa_w_uv[l])
        last = l == depth - 1
        g_next = final_g if last else norm_g[l + 1]
        h, hn = _merge(ya.reshape(m, W), yb.reshape(m, W), yc.reshape(m, W), yd.reshape(m, W), p32, h,
                       w_branch[l].astype(jnp.bfloat16), w_out[l].astype(jnp.bfloat16), g_next,
                       jnp.float32 if last else jnp.bfloat16)
    return hn.reshape(b, s, d)
```

```python
import functools

import numpy as np
import jax
import jax.numpy as jnp
from jax import lax
from jax.experimental import pallas as pl
from jax.experimental.pallas import tpu as pltpu

D_MODEL = 1024
N_BRANCH = 4
W = 512
EPS = 1e-6
QB = 128
QBE = 256
KC = 512
FOLD = 64
GM_GROUPS = 4
GM_CHUNK = 128
HEADS = 4
HEAD_DIM = W // HEADS
LATENT = 128
IDX_HEADS = 4
IDX_DIM = 64
TOPK_MAX = 256
CONV_WIDTH = 3
LANE = 128
NEG_BIG = -1e30
LOG2E = 1.4426950408889634
VMEM_LIMIT = 56 * 1024 * 1024

P16_DQ, P16_DK, P16_BQ = 0, 1, 2
P16_QI = 3
P16_KI = 16
N16 = 17 * LANE
P32_AU, P32_AV, P32_AZ, P32_BZ = 8, 9, 10, 11
P32_CB, P32_CC, P32_CX, P32_CZ, P32_DZ = 12, 13, 14, 15, 16
P32_BC = 68
P32_SM = 69
N32 = 70 * LANE

_NT = (((1,), (1,)), ((), ()))


def _cparams(sem):
    return pltpu.CompilerParams(dimension_semantics=sem, vmem_limit_bytes=VMEM_LIMIT)


def _sigmoid(z):
    return 1.0 / (1.0 + jnp.exp(-z))


def _silu(z):
    return z * _sigmoid(z)


def _rmsnorm_kernel(x_ref, g_ref, o_ref):
    x = x_ref[...]
    y = x * lax.rsqrt(jnp.mean(x * x, axis=-1, keepdims=True) + EPS)
    o_ref[...] = (y * g_ref[...]).astype(o_ref.dtype)


def _rmsnorm(x2, g, out_dtype, tm=512):
    m, d = x2.shape
    return pl.pallas_call(
        _rmsnorm_kernel,
        grid=(m // tm,),
        in_specs=[pl.BlockSpec((tm, d), lambda i: (i, 0)), pl.BlockSpec((1, d), lambda i: (0, 0))],
        out_specs=pl.BlockSpec((tm, d), lambda i: (i, 0)),
        out_shape=jax.ShapeDtypeStruct((m, d), out_dtype),
        compiler_params=_cparams(("arbitrary",)),
        name="rmsnorm",
    )(x2, g.reshape(1, d))


def _matmul_kernel(a_ref, w_ref, o_ref):
    o_ref[...] = jnp.dot(a_ref[...], w_ref[...], preferred_element_type=jnp.float32).astype(o_ref.dtype)


def _matmul(a, w, out_dtype, tm, tn, name):
    m, k = a.shape
    n = w.shape[1]
    return pl.pallas_call(
        _matmul_kernel,
        grid=(n // tn, m // tm),
        in_specs=[pl.BlockSpec((tm, k), lambda j, i: (i, 0)), pl.BlockSpec((k, tn), lambda j, i: (0, j))],
        out_specs=pl.BlockSpec((tm, tn), lambda j, i: (i, j)),
        out_shape=jax.ShapeDtypeStruct((m, n), out_dtype),
        compiler_params=_cparams(("arbitrary", "arbitrary")),
        name=name,
    )(a, w)


def _prep_w_in(w):
    d = w.shape[0]
    sizes = (W, W, W,
             W, LATENT, IDX_HEADS * IDX_DIM, IDX_DIM, IDX_HEADS, W,
             W, W, W, W,
             W, W, W, HEADS, W,
             N_BRANCH * D_MODEL)
    parts, off = [], 0
    for s in sizes:
        parts.append(w[:, off:off + s])
        off += s
    (a_u, a_v, a_z, b_q, b_c, b_qi, b_ki, b_wi, b_z,
     c_b, c_c, c_x, c_z, d_q, d_k, d_v, d_f, d_z, gates) = parts
    zeros = lambda n: jnp.zeros((d, n), w.dtype)
    qi = []
    for h in range(IDX_HEADS):
        qi += [b_qi[:, h * IDX_DIM:(h + 1) * IDX_DIM], zeros(LANE - IDX_DIM)]
    w16 = jnp.concatenate([d_q, d_k, b_q] + qi + [b_ki, zeros(LANE - IDX_DIM)], axis=1)
    w32 = jnp.concatenate([gates, a_u, a_v, a_z, b_z, c_b, c_c, c_x, c_z, d_z, b_c,
                           d_f, b_wi, zeros(LANE - HEADS - IDX_HEADS)], axis=1)
    return w16.astype(jnp.bfloat16), w32.astype(jnp.bfloat16), jnp.transpose(d_v).astype(jnp.bfloat16)


def _matmul_t_kernel(wt_ref, a_ref, o_ref):
    o_ref[...] = lax.dot_general(wt_ref[...], a_ref[...], _NT,
                                 preferred_element_type=jnp.float32).astype(o_ref.dtype)


def _matmul_t(a, wt, tm, name):
    m, k = a.shape
    n = wt.shape[0]
    return pl.pallas_call(
        _matmul_t_kernel,
        grid=(m // tm,),
        in_specs=[pl.BlockSpec((n, k), lambda i: (0, 0)), pl.BlockSpec((tm, k), lambda i: (i, 0))],
        out_specs=pl.BlockSpec((None, n, tm), lambda i: (i, 0, 0)),
        out_shape=jax.ShapeDtypeStruct((m // tm, n, tm), jnp.bfloat16),
        compiler_params=_cparams(("arbitrary",)),
        name=name,
    )(wt, a)


def _mix_ac_kernel(au_ref, av_ref, az_ref, cb_ref, cc_ref, cx_ref, cz_ref,
                   lng_ref, lnb_ref, ws_ref, bs_ref, cw_ref, ya_ref, yc_ref, halo_ref, *, tt):
    t = pl.program_id(1)

    v = av_ref[...]
    mu = jnp.mean(v, axis=-1, keepdims=True)
    vc = v - mu
    var = jnp.mean(vc * vc, axis=-1, keepdims=True)
    vn = (vc * lax.rsqrt(var + EPS) * lng_ref[...] + lnb_ref[...]).astype(jnp.bfloat16)
    row = lax.broadcasted_iota(jnp.int32, (GM_CHUNK, GM_CHUNK), 0)
    col = lax.broadcasted_iota(jnp.int32, (GM_CHUNK, GM_CHUNK), 1)
    tril = row >= col
    wg = [jnp.where(tril, ws_ref[g], 0.0).astype(jnp.bfloat16) for g in range(GM_GROUPS)]
    for ch in range(tt // GM_CHUNK):
        rows = slice(ch * GM_CHUNK, (ch + 1) * GM_CHUNK)
        for g in range(GM_GROUPS):
            cols = slice(g * LANE, (g + 1) * LANE)
            mixed = jnp.dot(wg[g], vn[rows, cols], preferred_element_type=jnp.float32) + bs_ref[:, cols]
            ya_ref[rows, cols] = (au_ref[rows, cols] * mixed * _silu(az_ref[rows, cols])).astype(ya_ref.dtype)

    @pl.when(t == 0)
    def _():
        halo_ref[...] = jnp.zeros_like(halo_ref)

    y = cc_ref[...] * cx_ref[...]
    ext = jnp.concatenate([halo_ref[...], y], axis=0)
    conv = cw_ref[2:3, :] * y
    for j in range(CONV_WIDTH - 1):
        shift = CONV_WIDTH - 1 - j
        conv = conv + cw_ref[j:j + 1, :] * ext[8 - shift:8 - shift + tt, :]
    yc_ref[...] = (cb_ref[...] * conv * _silu(cz_ref[...])).astype(yc_ref.dtype)
    halo_ref[...] = y[tt - 8:, :]


def _mix_ac(p32b, ln_g, ln_b, w_s, b_s, conv_w, tt=256):
    b, s, _ = p32b.shape
    blk = lambda idx: pl.BlockSpec((None, tt, W), lambda bi, ti, idx=idx: (bi, ti, idx))
    full2 = lambda shp: pl.BlockSpec(shp, lambda bi, ti: (0,) * len(shp))
    bs_full = jnp.repeat(jnp.transpose(b_s), LANE, axis=1)
    out = jax.ShapeDtypeStruct((b, s, W), jnp.bfloat16)
    return pl.pallas_call(
        functools.partial(_mix_ac_kernel, tt=tt),
        grid=(b, s // tt),
        in_specs=[blk(P32_AU), blk(P32_AV), blk(P32_AZ), blk(P32_CB), blk(P32_CC), blk(P32_CX), blk(P32_CZ),
                  full2((1, W)), full2((1, W)), full2((GM_GROUPS, GM_CHUNK, GM_CHUNK)),
                  full2((GM_CHUNK, W)), full2((CONV_WIDTH, W))],
        out_specs=[pl.BlockSpec((None, tt, W), lambda bi, ti: (bi, ti, 0))] * 2,
        out_shape=[out, out],
        scratch_shapes=[pltpu.VMEM((8, W), jnp.float32)],
        compiler_params=_cparams(("arbitrary", "arbitrary")),
        name="mix_ac",
    )(p32b, p32b, p32b, p32b, p32b, p32b, p32b,
      ln_g.reshape(1, W), ln_b.reshape(1, W), w_s, bs_full, conv_w)


def _split3(x):
    hi = x.astype(jnp.bfloat16)
    r1 = x - hi.astype(jnp.float32)
    mid = r1.astype(jnp.bfloat16)
    lo = (r1 - mid.astype(jnp.float32)).astype(jnp.bfloat16)
    return hi, mid, lo


def _aug_placement():
    pq = np.zeros((3 * LANE, W), np.float32)
    pk = np.zeros((3 * LANE, W), np.float32)
    cq = np.zeros((1, W), np.float32)
    ck = np.zeros((1, W), np.float32)
    for h in range(HEADS):
        for j in range(3):
            pq[j * LANE + h, h * HEAD_DIM + j] = 1.0
            cq[0, h * HEAD_DIM + 3 + j] = 1.0
            pk[j * LANE + h, h * HEAD_DIM + 3 + j] = -1.0
            ck[0, h * HEAD_DIM + j] = 1.0
    return (jnp.asarray(pq, jnp.bfloat16), jnp.asarray(pk, jnp.bfloat16), jnp.asarray(cq), jnp.asarray(ck))


def _fox_cum_kernel(f_ref, bias_ref, pq_ref, pk_ref, cq_ref, ck_ref, qa_ref, ka_ref, carry_ref, *, s):
    row = lax.broadcasted_iota(jnp.int32, (LANE, LANE), 0)
    col = lax.broadcasted_iota(jnp.int32, (LANE, LANE), 1)
    ones_tril = jnp.where(row >= col, 1.0, 0.0).astype(jnp.bfloat16)
    carry_ref[...] = jnp.zeros_like(carry_ref)

    def body(c, _):
        off = pl.multiple_of(c * LANE, LANE)
        x = f_ref[pl.ds(off, LANE), :] + bias_ref[...]
        ls = jnp.minimum(x, 0.0) - jnp.log1p(jnp.exp(-jnp.abs(x)))
        hi, mid, lo = _split3(ls)
        dot = lambda p: jnp.dot(ones_tril, p, preferred_element_type=jnp.float32)
        cs = (dot(hi) + dot(mid)) + dot(lo) + carry_ref[0:1, :]
        carry_ref[0:1, :] = cs[LANE - 1:LANE, :]
        parts = jnp.concatenate(_split3(cs), axis=1)
        qa = jnp.dot(parts, pq_ref[...], preferred_element_type=jnp.float32) + cq_ref[...]
        ka = jnp.dot(parts, pk_ref[...], preferred_element_type=jnp.float32) + ck_ref[...]
        qa_ref[pl.ds(off, LANE), :] = qa.astype(qa_ref.dtype)
        ka_ref[pl.ds(off, LANE), :] = ka.astype(ka_ref.dtype)
        return 0

    lax.fori_loop(0, s // LANE, body, 0)


def _fox_cum(p32b, b_f):
    b, s, _ = p32b.shape
    bias = jnp.zeros((1, LANE), jnp.float32).at[0, :HEADS].set(b_f)
    pq, pk, cq, ck = _aug_placement()
    const = lambda shp: pl.BlockSpec(shp, lambda bi: (0, 0))
    out = jax.ShapeDtypeStruct((b, s, W), jnp.bfloat16)
    return pl.pallas_call(
        functools.partial(_fox_cum_kernel, s=s),
        grid=(b,),
        in_specs=[pl.BlockSpec((None, s, LANE), lambda bi: (bi, 0, P32_SM)), const((1, LANE)),
                  const((3 * LANE, W)), const((3 * LANE, W)), const((1, W)), const((1, W))],
        out_specs=[pl.BlockSpec((None, s, W), lambda bi: (bi, 0, 0))] * 2,
        out_shape=[out, out],
        scratch_shapes=[pltpu.VMEM((8, LANE), jnp.float32)],
        compiler_params=_cparams(("arbitrary",)),
        name="fox_cum",
    )(p32b, bias, pq, pk, cq, ck)


def _num_chunks(i, kc):
    return ((i + 1) * QB + kc - 1) // kc


def _fox_kernel(q_ref, qa_ref, k_ref, ka_ref, vt_ref, z_ref, o_ref, qf_ref, m_ref, l_ref, acc_ref, *, kc):
    i = pl.program_id(1)
    ndiag = (i * QBE) // kc
    scale = HEAD_DIM ** -0.5
    for h in range(HEADS):
        hs = slice(h * HEAD_DIM, (h + 1) * HEAD_DIM)
        qs = (q_ref[:, hs].astype(jnp.float32) * scale).astype(jnp.bfloat16)
        qf_ref[h] = jnp.concatenate([qs, qa_ref[:, hs]], axis=1)
    m_ref[...] = jnp.full_like(m_ref, NEG_BIG)
    l_ref[...] = jnp.zeros_like(l_ref)
    acc_ref[...] = jnp.zeros_like(acc_ref)

    def step(c, masked):
        off = pl.multiple_of(c * kc, kc)

        def scores(h):
            hs = slice(h * HEAD_DIM, (h + 1) * HEAD_DIM)
            kf = jnp.concatenate([k_ref[pl.ds(off, kc), hs], ka_ref[pl.ds(off, kc), hs]], axis=1)
            st = lax.dot_general(kf, qf_ref[h], _NT, preferred_element_type=jnp.float32)
            if masked:
                kpos = off + lax.broadcasted_iota(jnp.int32, (kc, QBE), 0)
                qpos = i * QBE + lax.broadcasted_iota(jnp.int32, (kc, QBE), 1)
                st = jnp.where(kpos <= qpos, st, NEG_BIG)
            return st

        st = [scores(h) for h in range(HEADS)]
        for h in range(HEADS):
            hs = slice(h * HEAD_DIM, (h + 1) * HEAD_DIM)
            m_old = m_ref[h]
            m_new = jnp.maximum(m_old, jnp.max(st[h], axis=0, keepdims=True))
            alpha = jnp.exp(m_old - m_new)
            p = jnp.exp(st[h] - m_new)
            l_ref[h] = alpha * l_ref[h] + jnp.sum(p, axis=0, keepdims=True)
            pv = jnp.dot(vt_ref[c, hs, :], p.astype(jnp.bfloat16), preferred_element_type=jnp.float32)
            acc_ref[h] = alpha * acc_ref[h] + pv
            m_ref[h] = m_new

    def body(c, _):
        step(c, False)
        return 0

    lax.fori_loop(0, ndiag, body, 0)
    step(ndiag, True)
    for h in range(HEADS):
        hs = slice(h * HEAD_DIM, (h + 1) * HEAD_DIM)
        o = jnp.transpose(acc_ref[h] / l_ref[h])
        o_ref[:, hs] = (o * _silu(z_ref[:, hs])).astype(o_ref.dtype)


def _fox(p16b, p32b, qaug, kaug, vt):
    b, s, _ = p16b.shape
    kc = vt.shape[-1]
    assert s % kc == 0 and kc % QBE == 0
    nck = s // kc
    vt = vt.reshape(b, nck, W, kc)
    row = lambda: pltpu.VMEM((HEADS, 1, QBE), jnp.float32)
    return pl.pallas_call(
        functools.partial(_fox_kernel, kc=kc),
        grid=(b, s // QBE),
        in_specs=[pl.BlockSpec((None, QBE, W), lambda bi, i: (bi, i, P16_DQ)),
                  pl.BlockSpec((None, QBE, W), lambda bi, i: (bi, i, 0)),
                  pl.BlockSpec((None, s, W), lambda bi, i: (bi, 0, P16_DK)),
                  pl.BlockSpec((None, s, W), lambda bi, i: (bi, 0, 0)),
                  pl.BlockSpec((None, nck, W, kc), lambda bi, i: (bi, 0, 0, 0)),
                  pl.BlockSpec((None, QBE, W), lambda bi, i: (bi, i, P32_DZ))],
        out_specs=pl.BlockSpec((None, QBE, W), lambda bi, i: (bi, i, 0)),
        out_shape=jax.ShapeDtypeStruct((b, s, W), jnp.bfloat16),
        scratch_shapes=[pltpu.VMEM((HEADS, QBE, 2 * HEAD_DIM), jnp.bfloat16),
                        row(), row(),
                        pltpu.VMEM((HEADS, HEAD_DIM, QBE), jnp.float32)],
        compiler_params=_cparams(("arbitrary", "arbitrary")),
        name="fox_attn",
    )(p16b, qaug, p16b, kaug, vt, p32b)


def _dsa_kernel(q_ref, qi_ref, sm_ref, z_ref, ki_ref, c_ref, kvg_ref, wuk_ref, wuv_ref, o_ref,
                cn_ref, sc_ref, thr_ref, jst_ref, lo_ref, hib_ref, hi_ref, clo_ref,
                m_ref, l_ref, acc_ref, *, s_len, kc, ksel, n_bisect):
    i = pl.program_id(1)
    nch = _num_chunks(i, kc)
    qpos = i * QB + lax.broadcasted_iota(jnp.int32, (QB, 1), 0)
    kf = float(ksel)
    inf = float("inf")

    @pl.when(i == 0)
    def _():
        def nbody(c, _):
            off = pl.multiple_of(c * kc, kc)
            x = c_ref[pl.ds(off, kc), :]
            y = x * lax.rsqrt(jnp.mean(x * x, axis=-1, keepdims=True) + EPS) * kvg_ref[...]
            cn_ref[pl.ds(off, kc), :] = y.astype(cn_ref.dtype)
            return 0
        lax.fori_loop(0, s_len // kc, nbody, 0)

    wi = sm_ref[:, HEADS:HEADS + IDX_HEADS] * (IDX_HEADS ** -0.5 * IDX_DIM ** -0.5)

    def sbody(c, _):
        off = pl.multiple_of(c * kc, kc)
        kk = ki_ref[pl.ds(off, kc), :]
        sc = jnp.zeros((QB, kc), jnp.float32)
        for h in range(IDX_HEADS):
            d = lax.dot_general(qi_ref[:, h * LANE:(h + 1) * LANE], kk, _NT, preferred_element_type=jnp.float32)
            sc = sc + wi[:, h:h + 1] * jnp.maximum(d, 0.0)
        kpos = off + lax.broadcasted_iota(jnp.int32, (1, kc), 1)
        sc_ref[c] = jnp.where(kpos <= qpos, sc, -inf)
        return 0

    lax.fori_loop(0, nch, sbody, 0)

    def fold_chunks(fn, init):
        def cbody(c, acc):
            off = pl.multiple_of(c * kc, kc)
            sc = sc_ref[c]
            for j in range(kc // LANE):
                kpos = off + j * LANE + lax.broadcasted_iota(jnp.int32, (1, LANE), 1)
                acc = fn(acc, sc[:, j * LANE:(j + 1) * LANE], kpos)
            return acc
        return lax.fori_loop(0, nch, cbody, init)

    def count(pred, t):
        tb = jnp.broadcast_to(t, (QB, LANE))
        acc = fold_chunks(lambda a, x, kp: a + jnp.where(pred(x, tb, kp), 1.0, 0.0),
                          jnp.zeros((QB, LANE), jnp.float32))
        return jnp.sum(acc, axis=-1, keepdims=True)

    def any_row(flag):
        return (jnp.max(jnp.where(flag, 1.0, 0.0)) > 0.5).astype(jnp.int32)

    thr_ref[...] = jnp.full_like(thr_ref, -inf)
    jst_ref[...] = jnp.full_like(jst_ref, -1)

    @pl.when(i * QB >= ksel)
    def _():
        mn0 = jnp.full((QB, LANE), inf, jnp.float32)
        mx0 = jnp.full((QB, LANE), -inf, jnp.float32)
        mn, mx = fold_chunks(
            lambda a, x, kp: (jnp.minimum(a[0], jnp.where(x == -inf, inf, x)), jnp.maximum(a[1], x)),
            (mn0, mx0))
        lo_ref[...] = jnp.min(mn, axis=-1, keepdims=True)
        hib_ref[...] = jnp.max(mx, axis=-1, keepdims=True)
        hi_ref[...] = jnp.full_like(hi_ref, inf)
        clo_ref[...] = (qpos + 1).astype(jnp.float32)

        def bis_cond(st):
            it, go = st
            return jnp.logical_and(it < n_bisect, go > 0)

        def bis_body(st):
            it, _ = st
            lo, hib = lo_ref[...], hib_ref[...]
            mid = 0.5 * lo + 0.5 * hib
            cnt = count(lambda x, tb, kp: x >= tb, mid)
            ge = cnt >= kf
            lo_ref[...] = jnp.where(ge, mid, lo)
            clo = jnp.where(ge, cnt, clo_ref[...])
            clo_ref[...] = clo
            hib_ref[...] = jnp.where(ge, hib, mid)
            hi_ref[...] = jnp.where(ge, hi_ref[...], mid)
            return it + 1, any_row(clo != kf)

        lax.while_loop(bis_cond, bis_body, (jnp.int32(0), jnp.int32(1)))

        resolved = clo_ref[...] == kf
        thr_ref[...] = jnp.where(resolved, lo_ref[...], inf)
        clo_ref[...] = jnp.where(resolved, kf, 0.0)

        def peel_body(go):
            done = clo_ref[...] >= kf
            hi = hi_ref[...]
            hb = jnp.broadcast_to(hi, (QB, LANE))
            nxt = fold_chunks(lambda a, x, kp: jnp.maximum(a, jnp.where(x < hb, x, -inf)),
                              jnp.full((QB, LANE), -inf, jnp.float32))
            t = jnp.where(done, thr_ref[...], jnp.max(nxt, axis=-1, keepdims=True))
            cnt = count(lambda x, tb, kp: x >= tb, t)
            thr_ref[...] = t
            clo_ref[...] = cnt
            hi_ref[...] = jnp.where(done, hi, t)
            return any_row(cnt < kf)

        lax.while_loop(lambda go: go > 0, peel_body, any_row(clo_ref[...] < kf))

        jst_ref[...] = jnp.full_like(jst_ref, s_len)

        @pl.when(any_row(clo_ref[...] > kf) > 0)
        def _():
            thr = thr_ref[...]
            need = kf - count(lambda x, tb, kp: x > tb, thr)
            excess = clo_ref[...] > kf

            def tie_body(bi, j):
                cand = j + lax.shift_left(jnp.int32(1), (s_len.bit_length() - 1) - bi)
                cb = jnp.broadcast_to(cand, (QB, LANE))
                tb = jnp.broadcast_to(thr, (QB, LANE))
                acc = fold_chunks(
                    lambda a, x, kp: a + jnp.where(jnp.logical_and(x == tb, kp < cb), 1.0, 0.0),
                    jnp.zeros((QB, LANE), jnp.float32))
                below = jnp.sum(acc, axis=-1, keepdims=True)
                return jnp.where(below < need, cand, j)

            j = lax.fori_loop(0, s_len.bit_length(), tie_body, jnp.zeros((QB, 1), jnp.int32))
            jst_ref[...] = jnp.where(excess, j, s_len)

    qlat = []
    for h in range(HEADS):
        hs = slice(h * HEAD_DIM, (h + 1) * HEAD_DIM)
        ql = lax.dot_general(q_ref[:, hs], wuk_ref[h], _NT, preferred_element_type=jnp.float32)
        qlat.append((ql * HEAD_DIM ** -0.5).astype(jnp.bfloat16))
    qlat = jnp.concatenate(qlat, axis=0)
    m_ref[...] = jnp.full_like(m_ref, NEG_BIG)
    l_ref[...] = jnp.zeros_like(l_ref)
    acc_ref[...] = jnp.zeros_like(acc_ref)
    thr = thr_ref[...]
    jst = jst_ref[...]

    def abody(c, _):
        off = pl.multiple_of(c * kc, kc)
        cn = cn_ref[pl.ds(off, kc), :]
        sc = sc_ref[c]
        kpos = off + lax.broadcasted_iota(jnp.int32, (1, kc), 1)
        sel = jnp.logical_or(sc > thr, jnp.logical_and(sc == thr, kpos <= jst))
        logits = lax.dot_general(qlat, cn, _NT, preferred_element_type=jnp.float32)
        for h in range(HEADS):
            rs = slice(h * QB, (h + 1) * QB)
            s = jnp.where(sel, logits[rs, :], NEG_BIG)
            m_old = m_ref[rs, :]
            m_new = jnp.maximum(m_old, jnp.max(s, axis=-1, keepdims=True))
            alpha = jnp.exp(m_old - m_new)
            p = jnp.exp(s - m_new)
            l_ref[rs, :] = alpha * l_ref[rs, :] + jnp.sum(p, axis=-1, keepdims=True)
            pv = jnp.dot(p.astype(jnp.bfloat16), cn, preferred_element_type=jnp.float32)
            acc_ref[rs, :] = alpha * acc_ref[rs, :] + pv
            m_ref[rs, :] = m_new
        return 0

    lax.fori_loop(0, nch, abody, 0)

    for h in range(HEADS):
        hs = slice(h * HEAD_DIM, (h + 1) * HEAD_DIM)
        rs = slice(h * QB, (h + 1) * QB)
        o_lat = (acc_ref[rs, :] / l_ref[rs, :]).astype(jnp.bfloat16)
        o = jnp.dot(o_lat, wuv_ref[h], preferred_element_type=jnp.float32)
        o_ref[:, hs] = (o * _silu(z_ref[:, hs])).astype(o_ref.dtype)


def _dsa(p16b, p32b, kv_g, w_uk, w_uv):
    b, s, _ = p16b.shape
    kc = min(512, s)
    ksel = min(TOPK_MAX, s // 4)
    assert ksel % QB == 0 and s % kc == 0
    col = lambda: pltpu.VMEM((QB, 1), jnp.float32)
    full3 = pl.BlockSpec((HEADS, LATENT, HEAD_DIM), lambda bi, i: (0, 0, 0))
    return pl.pallas_call(
        functools.partial(_dsa_kernel, s_len=s, kc=kc, ksel=ksel, n_bisect=24),
        grid=(b, s // QB),
        in_specs=[pl.BlockSpec((None, QB, W), lambda bi, i: (bi, i, P16_BQ)),
                  pl.BlockSpec((None, QB, W), lambda bi, i: (bi, i, P16_QI)),
                  pl.BlockSpec((None, QB, LANE), lambda bi, i: (bi, i, P32_SM)),
                  pl.BlockSpec((None, QB, W), lambda bi, i: (bi, i, P32_BZ)),
                  pl.BlockSpec((None, s, LANE), lambda bi, i: (bi, 0, P16_KI)),
                  pl.BlockSpec((None, s, LANE), lambda bi, i: (bi, 0, P32_BC)),
                  pl.BlockSpec((1, LATENT), lambda bi, i: (0, 0)),
                  full3, full3],
        out_specs=pl.BlockSpec((None, QB, W), lambda bi, i: (bi, i, 0)),
        out_shape=jax.ShapeDtypeStruct((b, s, W), jnp.bfloat16),
        scratch_shapes=[pltpu.VMEM((s, LATENT), jnp.bfloat16),
                        pltpu.VMEM((s // kc, QB, kc), jnp.float32),
                        col(),
                        pltpu.VMEM((QB, 1), jnp.int32),
                        col(), col(), col(), col(),
                        pltpu.VMEM((HEADS * QB, 1), jnp.float32),
                        pltpu.VMEM((HEADS * QB, 1), jnp.float32),
                        pltpu.VMEM((HEADS * QB, LATENT), jnp.float32)],
        compiler_params=_cparams(("arbitrary", "arbitrary")),
        name="dsa_attn",
    )(p16b, p16b, p32b, p32b, p16b, p32b, kv_g.reshape(1, LATENT),
      w_uk.astype(jnp.bfloat16), w_uv.astype(jnp.bfloat16))


def _dsat_kernel(q_ref, qi_ref, sm_ref, z_ref, ki_ref, c_ref, kvg_ref, wuk_ref, wuv_ref, o_ref,
                 cn_ref, cnt_ref, sc_ref, qia_ref, qlat_ref, lga_ref, lgb_ref, bias_ref, p_ref,
                 thr_ref, jst_ref, lo_ref, hib_ref, hi_ref, clo_ref,
                 m_ref, l_ref, acc_ref, *, s_len, kc, ksel, n_bisect):
    i = pl.program_id(1)
    ndiag = (i * QBE) // kc
    nch = ndiag + 1
    qpos = i * QBE + lax.broadcasted_iota(jnp.int32, (1, QBE), 1)
    kf = float(ksel)
    inf = float("inf")

    def key_pos(c):
        return c * kc + lax.broadcasted_iota(jnp.int32, (kc, QBE), 0)

    @pl.when(i == 0)
    def _():
        def nbody(c, _):
            off = pl.multiple_of(c * kc, kc)
            x = c_ref[pl.ds(off, kc), :]
            y = x * lax.rsqrt(jnp.mean(x * x, axis=-1, keepdims=True) + EPS) * kvg_ref[...]
            cn_ref[pl.ds(off, kc), :] = y.astype(cn_ref.dtype)
            cnt_ref[c] = jnp.transpose(y).astype(cnt_ref.dtype)
            return 0
        lax.fori_loop(0, s_len // kc, nbody, 0)

    wit = jnp.transpose(sm_ref[...])[HEADS:HEADS + IDX_HEADS, :] * (IDX_HEADS ** -0.5 * IDX_DIM ** -0.5)
    for h in range(IDX_HEADS):
        qia_ref[h * QBE:(h + 1) * QBE, :] = qi_ref[:, h * LANE:(h + 1) * LANE]

    def dots_into(buf_ref, c):
        off = pl.multiple_of(c * kc, kc)
        buf_ref[...] = lax.dot_general(ki_ref[pl.ds(off, kc), :], qia_ref[...], _NT,
                                       preferred_element_type=jnp.float32)

    def scores_from(buf_ref, c):
        sc = None
        for h in range(IDX_HEADS):
            term = wit[h:h + 1, :] * jnp.maximum(buf_ref[:, h * QBE:(h + 1) * QBE], 0.0)
            sc = term if sc is None else sc + term
        sc_ref[c] = sc

    def run_pairs(produce, consume):
        produce(lga_ref, 0)

        def pair_body(j, _):
            produce(lgb_ref, 2 * j + 1)
            consume(lga_ref, 2 * j)
            produce(lga_ref, jnp.minimum(2 * j + 2, ndiag))
            consume(lgb_ref, 2 * j + 1)
            return 0

        lax.fori_loop(0, nch // 2, pair_body, 0)

        @pl.when(nch % 2 == 1)
        def _():
            consume(lga_ref, ndiag)

    run_pairs(dots_into, scores_from)
    sc_ref[ndiag] = jnp.where(key_pos(ndiag) <= qpos, sc_ref[ndiag], -inf)

    def fold_slabs(op, fn, init_val):
        def cbody(c, acc):
            x = sc_ref[c]
            for j in range(kc // FOLD):
                acc = op(acc, fn(x[j * FOLD:(j + 1) * FOLD, :], c, j * FOLD))
            return acc
        return lax.fori_loop(0, nch, cbody, jnp.full((FOLD, QBE), init_val, jnp.float32))

    def count(pred, t):
        acc = fold_slabs(lambda a, m: jnp.where(m, a + 1.0, a), lambda x, c, r0: pred(x, t), 0.0)
        return jnp.sum(acc, axis=0, keepdims=True)

    def any_lane(flag):
        return (jnp.max(jnp.where(flag, 1.0, 0.0)) > 0.5).astype(jnp.int32)

    searched = qpos + 1 > ksel
    thr_ref[...] = jnp.full_like(thr_ref, -inf)
    jst_ref[...] = jnp.full_like(jst_ref, -1)

    @pl.when((i + 1) * QBE > ksel)
    def _():
        mn = fold_slabs(jnp.minimum, lambda x, c, r0: jnp.where(x == -inf, inf, x), inf)
        mx = fold_slabs(jnp.maximum, lambda x, c, r0: x, -inf)
        lo_ref[...] = jnp.min(mn, axis=0, keepdims=True)
        hib_ref[...] = jnp.max(mx, axis=0, keepdims=True)
        hi_ref[...] = jnp.full_like(hi_ref, inf)
        clo_ref[...] = (qpos + 1).astype(jnp.float32)

        def bis_body(_, carry):
            lo, hib = lo_ref[...], hib_ref[...]
            mid = 0.5 * lo + 0.5 * hib
            cnt = count(lambda x, t: x >= t, mid)
            ge = cnt >= kf
            lo_ref[...] = jnp.where(ge, mid, lo)
            clo_ref[...] = jnp.where(ge, cnt, clo_ref[...])
            hib_ref[...] = jnp.where(ge, hib, mid)
            hi_ref[...] = jnp.where(ge, hi_ref[...], mid)
            return carry

        lax.fori_loop(0, n_bisect, bis_body, 0)

        resolved = jnp.logical_or(clo_ref[...] == kf, jnp.logical_not(searched))
        thr_ref[...] = jnp.where(resolved, lo_ref[...], inf)
        clo_ref[...] = jnp.where(resolved, kf, 0.0)

        def peel_body(go):
            done = clo_ref[...] >= kf
            hi = hi_ref[...]
            nxt = fold_slabs(jnp.maximum, lambda x, c, r0: jnp.where(x < hi, x, -inf), -inf)
            t = jnp.where(done, thr_ref[...], jnp.max(nxt, axis=0, keepdims=True))
            cnt = count(lambda x, tt: x >= tt, t)
            thr_ref[...] = t
            clo_ref[...] = jnp.where(done, clo_ref[...], cnt)
            hi_ref[...] = jnp.where(done, hi, t)
            return any_lane(jnp.logical_and(jnp.logical_not(done), cnt < kf))

        lax.while_loop(lambda go: go > 0, peel_body, any_lane(clo_ref[...] < kf))

        jst_ref[...] = jnp.where(searched, s_len, -1)
        thr_ref[...] = jnp.where(searched, thr_ref[...], -inf)
        excess = jnp.logical_and(searched, clo_ref[...] > kf)

        @pl.when(any_lane(excess) > 0)
        def _():
            thr = thr_ref[...]
            need = kf - count(lambda x, t: x > t, thr)
            nbits = s_len.bit_length()

            def tie_body(bi, j):
                cand = j + lax.shift_left(jnp.int32(1), (nbits - 1) - bi)
                def tied_below(x, c, r0):
                    kpos = c * kc + r0 + lax.broadcasted_iota(jnp.int32, (FOLD, QBE), 0)
                    return jnp.where(jnp.logical_and(x == thr, kpos < cand), 1.0, 0.0)
                below = jnp.sum(fold_slabs(jnp.add, tied_below, 0.0), axis=0, keepdims=True)
                return jnp.where(below < need, cand, j)

            j = lax.fori_loop(0, nbits, tie_body, jnp.zeros((1, QBE), jnp.int32))
            jst_ref[...] = jnp.where(excess, j, jst_ref[...])

    for h in range(HEADS):
        hs = slice(h * HEAD_DIM, (h + 1) * HEAD_DIM)
        ql = lax.dot_general(q_ref[:, hs], wuk_ref[h], _NT, preferred_element_type=jnp.float32)
        qlat_ref[h * QBE:(h + 1) * QBE, :] = (ql * (HEAD_DIM ** -0.5 * LOG2E)).astype(qlat_ref.dtype)
    m_ref[...] = jnp.full_like(m_ref, NEG_BIG)
    l_ref[...] = jnp.zeros_like(l_ref)
    acc_ref[...] = jnp.zeros_like(acc_ref)
    thr = thr_ref[...]
    jst = jst_ref[...]

    def logits_into(buf_ref, c):
        off = pl.multiple_of(c * kc, kc)
        buf_ref[...] = lax.dot_general(cn_ref[pl.ds(off, kc), :], qlat_ref[...], _NT,
                                       preferred_element_type=jnp.float32)

    def softmax_pv(buf_ref, c):
        sc = sc_ref[c]
        sel = jnp.logical_or(sc > thr, jnp.logical_and(sc == thr, key_pos(c) <= jst))
        bias_ref[...] = jnp.where(sel, 0.0, NEG_BIG)
        m_old = m_ref[...]
        m_new = jnp.maximum(m_old, jnp.concatenate(
            [jnp.max(buf_ref[:, h * QBE:(h + 1) * QBE] + bias_ref[...], axis=0, keepdims=True)
             for h in range(HEADS)], axis=1))
        alpha = jnp.exp2(m_old - m_new)
        psum = []
        for h in range(HEADS):
            qs = slice(h * QBE, (h + 1) * QBE)
            p = jnp.exp2(buf_ref[:, qs] + bias_ref[...] - m_new[:, qs])
            psum.append(jnp.sum(p, axis=0, keepdims=True))
            p_ref[:, qs] = p.astype(p_ref.dtype)
        l_ref[...] = alpha * l_ref[...] + jnp.concatenate(psum, axis=1)
        pv = jnp.dot(cnt_ref[c], p_ref[...], preferred_element_type=jnp.float32)
        acc_ref[...] = alpha * acc_ref[...] + pv
        m_ref[...] = m_new

    run_pairs(logits_into, softmax_pv)

    for h in range(HEADS):
        hs = slice(h * HEAD_DIM, (h + 1) * HEAD_DIM)
        qs = slice(h * QBE, (h + 1) * QBE)
        o_lat = jnp.transpose(acc_ref[:, qs] / l_ref[:, qs]).astype(jnp.bfloat16)
        o = jnp.dot(o_lat, wuv_ref[h], preferred_element_type=jnp.float32)
        o_ref[:, hs] = (o * _silu(z_ref[:, hs])).astype(o_ref.dtype)


def _dsat(p16b, p32b, kv_g, w_uk, w_uv):
    b, s, _ = p16b.shape
    kc = min(KC, s)
    ksel = min(TOPK_MAX, s // 4)
    assert s % kc == 0 and kc % QBE == 0
    nq = HEADS * QBE
    row = lambda n: pltpu.VMEM((1, n), jnp.float32)
    full3 = pl.BlockSpec((HEADS, LATENT, HEAD_DIM), lambda bi, i: (0, 0, 0))
    return pl.pallas_call(
        functools.partial(_dsat_kernel, s_len=s, kc=kc, ksel=ksel, n_bisect=16),
        grid=(b, s // QBE),
        in_specs=[pl.BlockSpec((None, QBE, W), lambda bi, i: (bi, i, P16_BQ)),
                  pl.BlockSpec((None, QBE, W), lambda bi, i: (bi, i, P16_QI)),
                  pl.BlockSpec((None, QBE, LANE), lambda bi, i: (bi, i, P32_SM)),
                  pl.BlockSpec((None, QBE, W), lambda bi, i: (bi, i, P32_BZ)),
                  pl.BlockSpec((None, s, LANE), lambda bi, i: (bi, 0, P16_KI)),
                  pl.BlockSpec((None, s, LANE), lambda bi, i: (bi, 0, P32_BC)),
                  pl.BlockSpec((1, LATENT), lambda bi, i: (0, 0)),
                  full3, full3],
        out_specs=pl.BlockSpec((None, QBE, W), lambda bi, i: (bi, i, 0)),
        out_shape=jax.ShapeDtypeStruct((b, s, W), jnp.bfloat16),
        scratch_shapes=[pltpu.VMEM((s, LATENT), jnp.bfloat16),
                        pltpu.VMEM((s // kc, LATENT, kc), jnp.bfloat16),
                        pltpu.VMEM((s // kc, kc, QBE), jnp.float32),
                        pltpu.VMEM((IDX_HEADS * QBE, LANE), jnp.bfloat16),
                        pltpu.VMEM((nq, LATENT), jnp.bfloat16),
                        pltpu.VMEM((kc, nq), jnp.float32),
                        pltpu.VMEM((kc, nq), jnp.float32),
                        pltpu.VMEM((kc, QBE), jnp.float32),
                        pltpu.VMEM((kc, nq), jnp.bfloat16),
                        row(QBE),
                        pltpu.VMEM((1, QBE), jnp.int32),
                        row(QBE), row(QBE), row(QBE), row(QBE),
                        row(nq), row(nq),
                        pltpu.VMEM((LATENT, nq), jnp.float32)],
        compiler_params=_cparams(("arbitrary", "arbitrary")),
        name="dsa_attn",
    )(p16b, p16b, p32b, p32b, p16b, p32b, kv_g.reshape(1, LATENT),
      w_uk.astype(jnp.bfloat16), w_uv.astype(jnp.bfloat16))


def _merge_kernel(ya_ref, yb_ref, yc_ref, yd_ref, g_ref, x_ref, wb_ref, wo_ref, gn_ref, h_ref, hn_ref):
    merged = None
    for n, y_ref in enumerate((ya_ref, yb_ref, yc_ref, yd_ref)):
        lifted = jnp.dot(y_ref[...], wb_ref[n], preferred_element_type=jnp.float32)
        term = _sigmoid(g_ref[:, n * D_MODEL:(n + 1) * D_MODEL]) * lifted
        merged = term if merged is None else merged + term
    h = x_ref[...] + jnp.dot(merged.astype(jnp.bfloat16), wo_ref[...], preferred_element_type=jnp.float32)
    h_ref[...] = h
    hn = h * lax.rsqrt(jnp.mean(h * h, axis=-1, keepdims=True) + EPS) * gn_ref[...]
    hn_ref[...] = hn.astype(hn_ref.dtype)


def _merge(ya, yb, yc, yd, p32, h, wb, wo, g_next, hn_dtype, tm=256):
    m, d = h.shape
    yblk = pl.BlockSpec((tm, W), lambda i: (i, 0))
    hblk = pl.BlockSpec((tm, d), lambda i: (i, 0))
    return pl.pallas_call(
        _merge_kernel,
        grid=(m // tm,),
        in_specs=[yblk, yblk, yblk, yblk,
                  pl.BlockSpec((tm, N_BRANCH * d), lambda i: (i, 0)),
                  hblk,
                  pl.BlockSpec((N_BRANCH, W, d), lambda i: (0, 0, 0)),
                  pl.BlockSpec((d, d), lambda i: (0, 0)),
                  pl.BlockSpec((1, d), lambda i: (0, 0))],
        out_specs=[hblk, hblk],
        out_shape=[jax.ShapeDtypeStruct((m, d), jnp.float32), jax.ShapeDtypeStruct((m, d), hn_dtype)],
        compiler_params=_cparams(("arbitrary",)),
        name="merge",
    )(ya, yb, yc, yd, p32, h, wb, wo, g_next.reshape(1, d))


def kernel(x, norm_g, w_in, gm_ln_g, gm_ln_b, gm_w_s, gm_b_s, dsa_kv_g, dsa_w_uk, dsa_w_uv,
           conv_w, fox_b_f, w_branch, w_out, final_g):
    b, s, d = x.shape
    depth = w_in.shape[0]
    m = b * s
    h = x.reshape(m, d)
    hn = _rmsnorm(h, norm_g[0], jnp.bfloat16)
    for l in range(depth):
        w16, w32, wvt = _prep_w_in(w_in[l])
        p16 = _matmul(hn, w16, jnp.bfloat16, tm=512, tn=N16, name="in_proj16")
        p32 = _matmul(hn, w32, jnp.float32, tm=512, tn=N32 // 5, name="in_proj32")
        vt = _matmul_t(hn, wvt, tm=min(KC, s), name="in_proj_vt")
        p16b = p16.reshape(b, s, N16)
        p32b = p32.reshape(b, s, N32)
        ya, yc = _mix_ac(p32b, gm_ln_g[l], gm_ln_b[l], gm_w_s[l], gm_b_s[l], conv_w[l])
        qaug, kaug = _fox_cum(p32b, fox_b_f[l])
        yd = _fox(p16b, p32b, qaug, kaug, vt)
        yb = _dsat(p16b, p32b, dsa_kv_g[l], dsa_w_uk[l], dsa_w_uv[l])
        last = l == depth - 1
        g_next = final_g if last else norm_g[l + 1]
        h, hn = _merge(ya.reshape(m, W), yb.reshape(m, W), yc.reshape(m, W), yd.reshape(m, W), p32, h,
                       w_branch[l].astype(jnp.bfloat16), w_out[l].astype(jnp.bfloat16), g_next,
                       jnp.float32 if last else jnp.bfloat16)
    return hn.reshape(b, s, d)
```

```python
import functools

import numpy as np
import jax
import jax.numpy as jnp
from jax import lax
from jax.experimental import pallas as pl
from jax.experimental.pallas import tpu as pltpu

D_MODEL = 1024
N_BRANCH = 4
W = 512
EPS = 1e-6
QB = 128
QBE = 256
KC = 512
FOLD = 64
GM_GROUPS = 4
GM_CHUNK = 128
HEADS = 4
HEAD_DIM = W // HEADS
LATENT = 128
IDX_HEADS = 4
IDX_DIM = 64
TOPK_MAX = 256
CONV_WIDTH = 3
LANE = 128
NEG_BIG = -1e30
LOG2E = 1.4426950408889634
VMEM_LIMIT = 56 * 1024 * 1024

P16_DQ, P16_DK, P16_BQ = 0, 1, 2
P16_QI = 3
P16_KI = 16
N16 = 17 * LANE
P32_AU, P32_AV, P32_AZ, P32_BZ = 8, 9, 10, 11
P32_CB, P32_CC, P32_CX, P32_CZ, P32_DZ = 12, 13, 14, 15, 16
P32_BC = 68
P32_SM = 69
N32 = 70 * LANE

_NT = (((1,), (1,)), ((), ()))


def _cparams(sem):
    return pltpu.CompilerParams(dimension_semantics=sem, vmem_limit_bytes=VMEM_LIMIT)


def _sigmoid(z):
    return 1.0 / (1.0 + jnp.exp(-z))


def _silu(z):
    return z * _sigmoid(z)


def _rmsnorm_kernel(x_ref, g_ref, o_ref):
    x = x_ref[...]
    y = x * lax.rsqrt(jnp.mean(x * x, axis=-1, keepdims=True) + EPS)
    o_ref[...] = (y * g_ref[...]).astype(o_ref.dtype)


def _rmsnorm(x2, g, out_dtype, tm=512):
    m, d = x2.shape
    return pl.pallas_call(
        _rmsnorm_kernel,
        grid=(m // tm,),
        in_specs=[pl.BlockSpec((tm, d), lambda i: (i, 0)), pl.BlockSpec((1, d), lambda i: (0, 0))],
        out_specs=pl.BlockSpec((tm, d), lambda i: (i, 0)),
        out_shape=jax.ShapeDtypeStruct((m, d), out_dtype),
        compiler_params=_cparams(("arbitrary",)),
        name="rmsnorm",
    )(x2, g.reshape(1, d))


def _matmul_kernel(a_ref, w_ref, o_ref):
    o_ref[...] = jnp.dot(a_ref[...], w_ref[...], preferred_element_type=jnp.float32).astype(o_ref.dtype)


def _matmul(a, w, out_dtype, tm, tn, name):
    m, k = a.shape
    n = w.shape[1]
    return pl.pallas_call(
        _matmul_kernel,
        grid=(n // tn, m // tm),
        in_specs=[pl.BlockSpec((tm, k), lambda j, i: (i, 0)), pl.BlockSpec((k, tn), lambda j, i: (0, j))],
        out_specs=pl.BlockSpec((tm, tn), lambda j, i: (i, j)),
        out_shape=jax.ShapeDtypeStruct((m, n), out_dtype),
        compiler_params=_cparams(("arbitrary", "arbitrary")),
        name=name,
    )(a, w)


def _prep_w_in(w):
    d = w.shape[0]
    sizes = (W, W, W,
             W, LATENT, IDX_HEADS * IDX_DIM, IDX_DIM, IDX_HEADS, W,
             W, W, W, W,
             W, W, W, HEADS, W,
             N_BRANCH * D_MODEL)
    parts, off = [], 0
    for s in sizes:
        parts.append(w[:, off:off + s])
        off += s
    (a_u, a_v, a_z, b_q, b_c, b_qi, b_ki, b_wi, b_z,
     c_b, c_c, c_x, c_z, d_q, d_k, d_v, d_f, d_z, gates) = parts
    zeros = lambda n: jnp.zeros((d, n), w.dtype)
    qi = []
    for h in range(IDX_HEADS):
        qi += [b_qi[:, h * IDX_DIM:(h + 1) * IDX_DIM], zeros(LANE - IDX_DIM)]
    w16 = jnp.concatenate([d_q, d_k, b_q] + qi + [b_ki, zeros(LANE - IDX_DIM)], axis=1)
    w32 = jnp.concatenate([gates, a_u, a_v, a_z, b_z, c_b, c_c, c_x, c_z, d_z, b_c,
                           d_f, b_wi, zeros(LANE - HEADS - IDX_HEADS)], axis=1)
    return w16.astype(jnp.bfloat16), w32.astype(jnp.bfloat16), jnp.transpose(d_v).astype(jnp.bfloat16)


def _matmul_t_kernel(wt_ref, a_ref, o_ref):
    o_ref[...] = lax.dot_general(wt_ref[...], a_ref[...], _NT,
                                 preferred_element_type=jnp.float32).astype(o_ref.dtype)


def _matmul_t(a, wt, tm, name):
    m, k = a.shape
    n = wt.shape[0]
    return pl.pallas_call(
        _matmul_t_kernel,
        grid=(m // tm,),
        in_specs=[pl.BlockSpec((n, k), lambda i: (0, 0)), pl.BlockSpec((tm, k), lambda i: (i, 0))],
        out_specs=pl.BlockSpec((None, n, tm), lambda i: (i, 0, 0)),
        out_shape=jax.ShapeDtypeStruct((m // tm, n, tm), jnp.bfloat16),
        compiler_params=_cparams(("arbitrary",)),
        name=name,
    )(wt, a)


def _mix_ac_kernel(au_ref, av_ref, az_ref, cb_ref, cc_ref, cx_ref, cz_ref,
                   lng_ref, lnb_ref, ws_ref, bs_ref, cw_ref, ya_ref, yc_ref, halo_ref, *, tt):
    t = pl.program_id(1)

    v = av_ref[...]
    mu = jnp.mean(v, axis=-1, keepdims=True)
    vc = v - mu
    var = jnp.mean(vc * vc, axis=-1, keepdims=True)
    vn = (vc * lax.rsqrt(var + EPS) * lng_ref[...] + lnb_ref[...]).astype(jnp.bfloat16)
    row = lax.broadcasted_iota(jnp.int32, (GM_CHUNK, GM_CHUNK), 0)
    col = lax.broadcasted_iota(jnp.int32, (GM_CHUNK, GM_CHUNK), 1)
    tril = row >= col
    wg = [jnp.where(tril, ws_ref[g], 0.0).astype(jnp.bfloat16) for g in range(GM_GROUPS)]
    for ch in range(tt // GM_CHUNK):
        rows = slice(ch * GM_CHUNK, (ch + 1) * GM_CHUNK)
        for g in range(GM_GROUPS):
            cols = slice(g * LANE, (g + 1) * LANE)
            mixed = jnp.dot(wg[g], vn[rows, cols], preferred_element_type=jnp.float32) + bs_ref[:, cols]
            ya_ref[rows, cols] = (au_ref[rows, cols] * mixed * _silu(az_ref[rows, cols])).astype(ya_ref.dtype)

    @pl.when(t == 0)
    def _():
        halo_ref[...] = jnp.zeros_like(halo_ref)

    y = cc_ref[...] * cx_ref[...]
    ext = jnp.concatenate([halo_ref[...], y], axis=0)
    conv = cw_ref[2:3, :] * y
    for j in range(CONV_WIDTH - 1):
        shift = CONV_WIDTH - 1 - j
        conv = conv + cw_ref[j:j + 1, :] * ext[8 - shift:8 - shift + tt, :]
    yc_ref[...] = (cb_ref[...] * conv * _silu(cz_ref[...])).astype(yc_ref.dtype)
    halo_ref[...] = y[tt - 8:, :]


def _mix_ac(p32b, ln_g, ln_b, w_s, b_s, conv_w, tt=256):
    b, s, _ = p32b.shape
    blk = lambda idx: pl.BlockSpec((None, tt, W), lambda bi, ti, idx=idx: (bi, ti, idx))
    full2 = lambda shp: pl.BlockSpec(shp, lambda bi, ti: (0,) * len(shp))
    bs_full = jnp.repeat(jnp.transpose(b_s), LANE, axis=1)
    out = jax.ShapeDtypeStruct((b, s, W), jnp.bfloat16)
    return pl.pallas_call(
        functools.partial(_mix_ac_kernel, tt=tt),
        grid=(b, s // tt),
        in_specs=[blk(P32_AU), blk(P32_AV), blk(P32_AZ), blk(P32_CB), blk(P32_CC), blk(P32_CX), blk(P32_CZ),
                  full2((1, W)), full2((1, W)), full2((GM_GROUPS, GM_CHUNK, GM_CHUNK)),
                  full2((GM_CHUNK, W)), full2((CONV_WIDTH, W))],
        out_specs=[pl.BlockSpec((None, tt, W), lambda bi, ti: (bi, ti, 0))] * 2,
        out_shape=[out, out],
        scratch_shapes=[pltpu.VMEM((8, W), jnp.float32)],
        compiler_params=_cparams(("arbitrary", "arbitrary")),
        name="mix_ac",
    )(p32b, p32b, p32b, p32b, p32b, p32b, p32b,
      ln_g.reshape(1, W), ln_b.reshape(1, W), w_s, bs_full, conv_w)


def _split3(x):
    hi = x.astype(jnp.bfloat16)
    r1 = x - hi.astype(jnp.float32)
    mid = r1.astype(jnp.bfloat16)
    lo = (r1 - mid.astype(jnp.float32)).astype(jnp.bfloat16)
    return hi, mid, lo


def _aug_placement():
    pq = np.zeros((3 * LANE, W), np.float32)
    pk = np.zeros((3 * LANE, W), np.float32)
    cq = np.zeros((1, W), np.float32)
    ck = np.zeros((1, W), np.float32)
    for h in range(HEADS):
        for j in range(3):
            pq[j * LANE + h, h * HEAD_DIM + j] = 1.0
            cq[0, h * HEAD_DIM + 3 + j] = 1.0
            pk[j * LANE + h, h * HEAD_DIM + 3 + j] = -1.0
            ck[0, h * HEAD_DIM + j] = 1.0
    return (jnp.asarray(pq, jnp.bfloat16), jnp.asarray(pk, jnp.bfloat16), jnp.asarray(cq), jnp.asarray(ck))


def _fox_cum_kernel(f_ref, bias_ref, pq_ref, pk_ref, cq_ref, ck_ref, qa_ref, ka_ref, carry_ref, *, s):
    row = lax.broadcasted_iota(jnp.int32, (LANE, LANE), 0)
    col = lax.broadcasted_iota(jnp.int32, (LANE, LANE), 1)
    ones_tril = jnp.where(row >= col, 1.0, 0.0).astype(jnp.bfloat16)
    carry_ref[...] = jnp.zeros_like(carry_ref)

    def body(c, _):
        off = pl.multiple_of(c * LANE, LANE)
        x = f_ref[pl.ds(off, LANE), :] + bias_ref[...]
        ls = jnp.minimum(x, 0.0) - jnp.log1p(jnp.exp(-jnp.abs(x)))
        hi, mid, lo = _split3(ls)
        dot = lambda p: jnp.dot(ones_tril, p, preferred_element_type=jnp.float32)
        cs = (dot(hi) + dot(mid)) + dot(lo) + carry_ref[0:1, :]
        carry_ref[0:1, :] = cs[LANE - 1:LANE, :]
        parts = jnp.concatenate(_split3(cs), axis=1)
        qa = jnp.dot(parts, pq_ref[...], preferred_element_type=jnp.float32) + cq_ref[...]
        ka = jnp.dot(parts, pk_ref[...], preferred_element_type=jnp.float32) + ck_ref[...]
        qa_ref[pl.ds(off, LANE), :] = qa.astype(qa_ref.dtype)
        ka_ref[pl.ds(off, LANE), :] = ka.astype(ka_ref.dtype)
        return 0

    lax.fori_loop(0, s // LANE, body, 0)


def _fox_cum(p32b, b_f):
    b, s, _ = p32b.shape
    bias = jnp.zeros((1, LANE), jnp.float32).at[0, :HEADS].set(b_f)
    pq, pk, cq, ck = _aug_placement()
    const = lambda shp: pl.BlockSpec(shp, lambda bi: (0, 0))
    out = jax.ShapeDtypeStruct((b, s, W), jnp.bfloat16)
    return pl.pallas_call(
        functools.partial(_fox_cum_kernel, s=s),
        grid=(b,),
        in_specs=[pl.BlockSpec((None, s, LANE), lambda bi: (bi, 0, P32_SM)), const((1, LANE)),
                  const((3 * LANE, W)), const((3 * LANE, W)), const((1, W)), const((1, W))],
        out_specs=[pl.BlockSpec((None, s, W), lambda bi: (bi, 0, 0))] * 2,
        out_shape=[out, out],
        scratch_shapes=[pltpu.VMEM((8, LANE), jnp.float32)],
        compiler_params=_cparams(("arbitrary",)),
        name="fox_cum",
    )(p32b, bias, pq, pk, cq, ck)


def _num_chunks(i, kc):
    return ((i + 1) * QB + kc - 1) // kc


def _fox_kernel(q_ref, qa_ref, k_ref, ka_ref, vt_ref, z_ref, o_ref, qf_ref, m_ref, l_ref, acc_ref, *, kc):
    i = pl.program_id(1)
    ndiag = (i * QBE) // kc
    scale = HEAD_DIM ** -0.5
    for h in range(HEADS):
        hs = slice(h * HEAD_DIM, (h + 1) * HEAD_DIM)
        qs = (q_ref[:, hs].astype(jnp.float32) * scale).astype(jnp.bfloat16)
        qf_ref[h] = jnp.concatenate([qs, qa_ref[:, hs]], axis=1)
    m_ref[...] = jnp.full_like(m_ref, NEG_BIG)
    l_ref[...] = jnp.zeros_like(l_ref)
    acc_ref[...] = jnp.zeros_like(acc_ref)

    def step(c, masked):
        off = pl.multiple_of(c * kc, kc)

        def scores(h):
            hs = slice(h * HEAD_DIM, (h + 1) * HEAD_DIM)
            kf = jnp.concatenate([k_ref[pl.ds(off, kc), hs], ka_ref[pl.ds(off, kc), hs]], axis=1)
            st = lax.dot_general(kf, qf_ref[h], _NT, preferred_element_type=jnp.float32)
            if masked:
                kpos = off + lax.broadcasted_iota(jnp.int32, (kc, QBE), 0)
                qpos = i * QBE + lax.broadcasted_iota(jnp.int32, (kc, QBE), 1)
                st = jnp.where(kpos <= qpos, st, NEG_BIG)
            return st

        st = [scores(h) for h in range(HEADS)]
        for h in range(HEADS):
            hs = slice(h * HEAD_DIM, (h + 1) * HEAD_DIM)
            m_old = m_ref[h]
            m_new = jnp.maximum(m_old, jnp.max(st[h], axis=0, keepdims=True))
            alpha = jnp.exp(m_old - m_new)
            p = jnp.exp(st[h] - m_new)
            l_ref[h] = alpha * l_ref[h] + jnp.sum(p, axis=0, keepdims=True)
            pv = jnp.dot(vt_ref[c, hs, :], p.astype(jnp.bfloat16), preferred_element_type=jnp.float32)
            acc_ref[h] = alpha * acc_ref[h] + pv
            m_ref[h] = m_new

    def body(c, _):
        step(c, False)
        return 0

    lax.fori_loop(0, ndiag, body, 0)
    step(ndiag, True)
    for h in range(HEADS):
        hs = slice(h * HEAD_DIM, (h + 1) * HEAD_DIM)
        o = jnp.transpose(acc_ref[h] / l_ref[h])
        o_ref[:, hs] = (o * _silu(z_ref[:, hs])).astype(o_ref.dtype)


def _fox(p16b, p32b, qaug, kaug, vt):
    b, s, _ = p16b.shape
    kc = vt.shape[-1]
    assert s % kc == 0 and kc % QBE == 0
    nck = s // kc
    vt = vt.reshape(b, nck, W, kc)
    row = lambda: pltpu.VMEM((HEADS, 1, QBE), jnp.float32)
    return pl.pallas_call(
        functools.partial(_fox_kernel, kc=kc),
        grid=(b, s // QBE),
        in_specs=[pl.BlockSpec((None, QBE, W), lambda bi, i: (bi, i, P16_DQ)),
                  pl.BlockSpec((None, QBE, W), lambda bi, i: (bi, i, 0)),
                  pl.BlockSpec((None, s, W), lambda bi, i: (bi, 0, P16_DK)),
                  pl.BlockSpec((None, s, W), lambda bi, i: (bi, 0, 0)),
                  pl.BlockSpec((None, nck, W, kc), lambda bi, i: (bi, 0, 0, 0)),
                  pl.BlockSpec((None, QBE, W), lambda bi, i: (bi, i, P32_DZ))],
        out_specs=pl.BlockSpec((None, QBE, W), lambda bi, i: (bi, i, 0)),
        out_shape=jax.ShapeDtypeStruct((b, s, W), jnp.bfloat16),
        scratch_shapes=[pltpu.VMEM((HEADS, QBE, 2 * HEAD_DIM), jnp.bfloat16),
                        row(), row(),
                        pltpu.VMEM((HEADS, HEAD_DIM, QBE), jnp.float32)],
        compiler_params=_cparams(("arbitrary", "arbitrary")),
        name="fox_attn",
    )(p16b, qaug, p16b, kaug, vt, p32b)


def _dsa_kernel(q_ref, qi_ref, sm_ref, z_ref, ki_ref, c_ref, kvg_ref, wuk_ref, wuv_ref, o_ref,
                cn_ref, sc_ref, thr_ref, jst_ref, lo_ref, hib_ref, hi_ref, clo_ref,
                m_ref, l_ref, acc_ref, *, s_len, kc, ksel, n_bisect):
    i = pl.program_id(1)
    nch = _num_chunks(i, kc)
    qpos = i * QB + lax.broadcasted_iota(jnp.int32, (QB, 1), 0)
    kf = float(ksel)
    inf = float("inf")

    @pl.when(i == 0)
    def _():
        def nbody(c, _):
            off = pl.multiple_of(c * kc, kc)
            x = c_ref[pl.ds(off, kc), :]
            y = x * lax.rsqrt(jnp.mean(x * x, axis=-1, keepdims=True) + EPS) * kvg_ref[...]
            cn_ref[pl.ds(off, kc), :] = y.astype(cn_ref.dtype)
            return 0
        lax.fori_loop(0, s_len // kc, nbody, 0)

    wi = sm_ref[:, HEADS:HEADS + IDX_HEADS] * (IDX_HEADS ** -0.5 * IDX_DIM ** -0.5)

    def sbody(c, _):
        off = pl.multiple_of(c * kc, kc)
        kk = ki_ref[pl.ds(off, kc), :]
        sc = jnp.zeros((QB, kc), jnp.float32)
        for h in range(IDX_HEADS):
            d = lax.dot_general(qi_ref[:, h * LANE:(h + 1) * LANE], kk, _NT, preferred_element_type=jnp.float32)
            sc = sc + wi[:, h:h + 1] * jnp.maximum(d, 0.0)
        kpos = off + lax.broadcasted_iota(jnp.int32, (1, kc), 1)
        sc_ref[c] = jnp.where(kpos <= qpos, sc, -inf)
        return 0

    lax.fori_loop(0, nch, sbody, 0)

    def fold_chunks(fn, init):
        def cbody(c, acc):
            off = pl.multiple_of(c * kc, kc)
            sc = sc_ref[c]
            for j in range(kc // LANE):
                kpos = off + j * LANE + lax.broadcasted_iota(jnp.int32, (1, LANE), 1)
                acc = fn(acc, sc[:, j * LANE:(j + 1) * LANE], kpos)
            return acc
        return lax.fori_loop(0, nch, cbody, init)

    def count(pred, t):
        tb = jnp.broadcast_to(t, (QB, LANE))
        acc = fold_chunks(lambda a, x, kp: a + jnp.where(pred(x, tb, kp), 1.0, 0.0),
                          jnp.zeros((QB, LANE), jnp.float32))
        return jnp.sum(acc, axis=-1, keepdims=True)

    def any_row(flag):
        return (jnp.max(jnp.where(flag, 1.0, 0.0)) > 0.5).astype(jnp.int32)

    thr_ref[...] = jnp.full_like(thr_ref, -inf)
    jst_ref[...] = jnp.full_like(jst_ref, -1)

    @pl.when(i * QB >= ksel)
    def _():
        mn0 = jnp.full((QB, LANE), inf, jnp.float32)
        mx0 = jnp.full((QB, LANE), -inf, jnp.float32)
        mn, mx = fold_chunks(
            lambda a, x, kp: (jnp.minimum(a[0], jnp.where(x == -inf, inf, x)), jnp.maximum(a[1], x)),
            (mn0, mx0))
        lo_ref[...] = jnp.min(mn, axis=-1, keepdims=True)
        hib_ref[...] = jnp.max(mx, axis=-1, keepdims=True)
        hi_ref[...] = jnp.full_like(hi_ref, inf)
        clo_ref[...] = (qpos + 1).astype(jnp.float32)

        def bis_cond(st):
            it, go = st
            return jnp.logical_and(it < n_bisect, go > 0)

        def bis_body(st):
            it, _ = st
            lo, hib = lo_ref[...], hib_ref[...]
            mid = 0.5 * lo + 0.5 * hib
            cnt = count(lambda x, tb, kp: x >= tb, mid)
            ge = cnt >= kf
            lo_ref[...] = jnp.where(ge, mid, lo)
            clo = jnp.where(ge, cnt, clo_ref[...])
            clo_ref[...] = clo
            hib_ref[...] = jnp.where(ge, hib, mid)
            hi_ref[...] = jnp.where(ge, hi_ref[...], mid)
            return it + 1, any_row(clo != kf)

        lax.while_loop(bis_cond, bis_body, (jnp.int32(0), jnp.int32(1)))

        resolved = clo_ref[...] == kf
        thr_ref[...] = jnp.where(resolved, lo_ref[...], inf)
        clo_ref[...] = jnp.where(resolved, kf, 0.0)

        def peel_body(go):
            done = clo_ref[...] >= kf
            hi = hi_ref[...]
            hb = jnp.broadcast_to(hi, (QB, LANE))
            nxt = fold_chunks(lambda a, x, kp: jnp.maximum(a, jnp.where(x < hb, x, -inf)),
                              jnp.full((QB, LANE), -inf, jnp.float32))
            t = jnp.where(done, thr_ref[...], jnp.max(nxt, axis=-1, keepdims=True))
            cnt = count(lambda x, tb, kp: x >= tb, t)
            thr_ref[...] = t
            clo_ref[...] = cnt
            hi_ref[...] = jnp.where(done, hi, t)
            return any_row(cnt < kf)

        lax.while_loop(lambda go: go > 0, peel_body, any_row(clo_ref[...] < kf))

        jst_ref[...] = jnp.full_like(jst_ref, s_len)

        @pl.when(any_row(clo_ref[...] > kf) > 0)
        def _():
            thr = thr_ref[...]
            need = kf - count(lambda x, tb, kp: x > tb, thr)
            excess = clo_ref[...] > kf

            def tie_body(bi, j):
                cand = j + lax.shift_left(jnp.int32(1), (s_len.bit_length() - 1) - bi)
                cb = jnp.broadcast_to(cand, (QB, LANE))
                tb = jnp.broadcast_to(thr, (QB, LANE))
                acc = fold_chunks(
                    lambda a, x, kp: a + jnp.where(jnp.logical_and(x == tb, kp < cb), 1.0, 0.0),
                    jnp.zeros((QB, LANE), jnp.float32))
                below = jnp.sum(acc, axis=-1, keepdims=True)
                return jnp.where(below < need, cand, j)

            j = lax.fori_loop(0, s_len.bit_length(), tie_body, jnp.zeros((QB, 1), jnp.int32))
            jst_ref[...] = jnp.where(excess, j, s_len)

    qlat = []
    for h in range(HEADS):
        hs = slice(h * HEAD_DIM, (h + 1) * HEAD_DIM)
        ql = lax.dot_general(q_ref[:, hs], wuk_ref[h], _NT, preferred_element_type=jnp.float32)
        qlat.append((ql * HEAD_DIM ** -0.5).astype(jnp.bfloat16))
    qlat = jnp.concatenate(qlat, axis=0)
    m_ref[...] = jnp.full_like(m_ref, NEG_BIG)
    l_ref[...] = jnp.zeros_like(l_ref)
    acc_ref[...] = jnp.zeros_like(acc_ref)
    thr = thr_ref[...]
    jst = jst_ref[...]

    def abody(c, _):
        off = pl.multiple_of(c * kc, kc)
        cn = cn_ref[pl.ds(off, kc), :]
        sc = sc_ref[c]
        kpos = off + lax.broadcasted_iota(jnp.int32, (1, kc), 1)
        sel = jnp.logical_or(sc > thr, jnp.logical_and(sc == thr, kpos <= jst))
        logits = lax.dot_general(qlat, cn, _NT, preferred_element_type=jnp.float32)
        for h in range(HEADS):
            rs = slice(h * QB, (h + 1) * QB)
            s = jnp.where(sel, logits[rs, :], NEG_BIG)
            m_old = m_ref[rs, :]
            m_new = jnp.maximum(m_old, jnp.max(s, axis=-1, keepdims=True))
            alpha = jnp.exp(m_old - m_new)
            p = jnp.exp(s - m_new)
            l_ref[rs, :] = alpha * l_ref[rs, :] + jnp.sum(p, axis=-1, keepdims=True)
            pv = jnp.dot(p.astype(jnp.bfloat16), cn, preferred_element_type=jnp.float32)
            acc_ref[rs, :] = alpha * acc_ref[rs, :] + pv
            m_ref[rs, :] = m_new
        return 0

    lax.fori_loop(0, nch, abody, 0)

    for h in range(HEADS):
        hs = slice(h * HEAD_DIM, (h + 1) * HEAD_DIM)
        rs = slice(h * QB, (h + 1) * QB)
        o_lat = (acc_ref[rs, :] / l_ref[rs, :]).astype(jnp.bfloat16)
        o = jnp.dot(o_lat, wuv_ref[h], preferred_element_type=jnp.float32)
        o_ref[:, hs] = (o * _silu(z_ref[:, hs])).astype(o_ref.dtype)


def _dsa(p16b, p32b, kv_g, w_uk, w_uv):
    b, s, _ = p16b.shape
    kc = min(512, s)
    ksel = min(TOPK_MAX, s // 4)
    assert ksel % QB == 0 and s % kc == 0
    col = lambda: pltpu.VMEM((QB, 1), jnp.float32)
    full3 = pl.BlockSpec((HEADS, LATENT, HEAD_DIM), lambda bi, i: (0, 0, 0))
    return pl.pallas_call(
        functools.partial(_dsa_kernel, s_len=s, kc=kc, ksel=ksel, n_bisect=24),
        grid=(b, s // QB),
        in_specs=[pl.BlockSpec((None, QB, W), lambda bi, i: (bi, i, P16_BQ)),
                  pl.BlockSpec((None, QB, W), lambda bi, i: (bi, i, P16_QI)),
                  pl.BlockSpec((None, QB, LANE), lambda bi, i: (bi, i, P32_SM)),
                  pl.BlockSpec((None, QB, W), lambda bi, i: (bi, i, P32_BZ)),
                  pl.BlockSpec((None, s, LANE), lambda bi, i: (bi, 0, P16_KI)),
                  pl.BlockSpec((None, s, LANE), lambda bi, i: (bi, 0, P32_BC)),
                  pl.BlockSpec((1, LATENT), lambda bi, i: (0, 0)),
                  full3, full3],
        out_specs=pl.BlockSpec((None, QB, W), lambda bi, i: (bi, i, 0)),
        out_shape=jax.ShapeDtypeStruct((b, s, W), jnp.bfloat16),
        scratch_shapes=[pltpu.VMEM((s, LATENT), jnp.bfloat16),
                        pltpu.VMEM((s // kc, QB, kc), jnp.float32),
                        col(),
                        pltpu.VMEM((QB, 1), jnp.int32),
                        col(), col(), col(), col(),
                        pltpu.VMEM((HEADS * QB, 1), jnp.float32),
                        pltpu.VMEM((HEADS * QB, 1), jnp.float32),
                        pltpu.VMEM((HEADS * QB, LATENT), jnp.float32)],
        compiler_params=_cparams(("arbitrary", "arbitrary")),
        name="dsa_attn",
    )(p16b, p16b, p32b, p32b, p16b, p32b, kv_g.reshape(1, LATENT),
      w_uk.astype(jnp.bfloat16), w_uv.astype(jnp.bfloat16))


def _dsat_kernel(q_ref, qi_ref, sm_ref, z_ref, ki_ref, c_ref, kvg_ref, wuk_ref, wuv_ref, o_ref,
                 cn_ref, cnt_ref, sc_ref, qia_ref, qlat_ref, lga_ref, lgb_ref, bias_ref, p_ref,
                 thr_ref, jst_ref, lo_ref, hib_ref, hi_ref, clo_ref,
                 m_ref, l_ref, acc_ref, *, s_len, kc, ksel, n_bisect):
    i = pl.program_id(1)
    ndiag = (i * QBE) // kc
    nch = ndiag + 1
    qpos = i * QBE + lax.broadcasted_iota(jnp.int32, (1, QBE), 1)
    kf = float(ksel)
    inf = float("inf")

    def key_pos(c):
        return c * kc + lax.broadcasted_iota(jnp.int32, (kc, QBE), 0)

    @pl.when(i == 0)
    def _():
        def nbody(c, _):
            off = pl.multiple_of(c * kc, kc)
            x = c_ref[pl.ds(off, kc), :]
            y = x * lax.rsqrt(jnp.mean(x * x, axis=-1, keepdims=True) + EPS) * kvg_ref[...]
            cn_ref[pl.ds(off, kc), :] = y.astype(cn_ref.dtype)
            cnt_ref[c] = jnp.transpose(y).astype(cnt_ref.dtype)
            return 0
        lax.fori_loop(0, s_len // kc, nbody, 0)

    wit = jnp.transpose(sm_ref[...])[HEADS:HEADS + IDX_HEADS, :] * (IDX_HEADS ** -0.5 * IDX_DIM ** -0.5)
    for h in range(IDX_HEADS):
        qia_ref[h * QBE:(h + 1) * QBE, :] = qi_ref[:, h * LANE:(h + 1) * LANE]

    def dots_into(buf_ref, c):
        off = pl.multiple_of(c * kc, kc)
        buf_ref[...] = lax.dot_general(ki_ref[pl.ds(off, kc), :], qia_ref[...], _NT,
                                       preferred_element_type=jnp.float32)

    def scores_from(buf_ref, c):
        sc = None
        for h in range(IDX_HEADS):
            term = wit[h:h + 1, :] * jnp.maximum(buf_ref[:, h * QBE:(h + 1) * QBE], 0.0)
            sc = term if sc is None else sc + term
        sc_ref[c] = sc

    def run_pairs(produce, consume):
        produce(lga_ref, 0)

        def pair_body(j, _):
            produce(lgb_ref, 2 * j + 1)
            consume(lga_ref, 2 * j)
            produce(lga_ref, jnp.minimum(2 * j + 2, ndiag))
            consume(lgb_ref, 2 * j + 1)
            return 0

        lax.fori_loop(0, nch // 2, pair_body, 0)

        @pl.when(nch % 2 == 1)
        def _():
            consume(lga_ref, ndiag)

    run_pairs(dots_into, scores_from)
    sc_ref[ndiag] = jnp.where(key_pos(ndiag) <= qpos, sc_ref[ndiag], -inf)

    def fold_slabs(op, fn, init_val):
        def cbody(c, acc):
            x = sc_ref[c]
            for j in range(kc // FOLD):
                acc = op(acc, fn(x[j * FOLD:(j + 1) * FOLD, :], c, j * FOLD))
            return acc
        return lax.fori_loop(0, nch, cbody, jnp.full((FOLD, QBE), init_val, jnp.float32))

    def count(pred, t):
        acc = fold_slabs(lambda a, m: jnp.where(m, a + 1.0, a), lambda x, c, r0: pred(x, t), 0.0)
        return jnp.sum(acc, axis=0, keepdims=True)

    def any_lane(flag):
        return (jnp.max(jnp.where(flag, 1.0, 0.0)) > 0.5).astype(jnp.int32)

    searched = qpos + 1 > ksel
    thr_ref[...] = jnp.full_like(thr_ref, -inf)
    jst_ref[...] = jnp.full_like(jst_ref, -1)

    @pl.when((i + 1) * QBE > ksel)
    def _():
        mn = fold_slabs(jnp.minimum, lambda x, c, r0: jnp.where(x == -inf, inf, x), inf)
        mx = fold_slabs(jnp.maximum, lambda x, c, r0: x, -inf)
        lo_ref[...] = jnp.min(mn, axis=0, keepdims=True)
        hib_ref[...] = jnp.max(mx, axis=0, keepdims=True)
        hi_ref[...] = jnp.full_like(hi_ref, inf)
        clo_ref[...] = (qpos + 1).astype(jnp.float32)

        def bis_body(_, carry):
            lo, hib = lo_ref[...], hib_ref[...]
            mid = 0.5 * lo + 0.5 * hib
            cnt = count(lambda x, t: x >= t, mid)
            ge = cnt >= kf
            lo_ref[...] = jnp.where(ge, mid, lo)
            clo_ref[...] = jnp.where(ge, cnt, clo_ref[...])
            hib_ref[...] = jnp.where(ge, hib, mid)
            hi_ref[...] = jnp.where(ge, hi_ref[...], mid)
            return carry

        lax.fori_loop(0, n_bisect, bis_body, 0)

        resolved = jnp.logical_or(clo_ref[...] == kf, jnp.logical_not(searched))
        thr_ref[...] = jnp.where(resolved, lo_ref[...], inf)
        clo_ref[...] = jnp.where(resolved, kf, 0.0)

        def peel_body(go):
            done = clo_ref[...] >= kf
            hi = hi_ref[...]
            nxt = fold_slabs(jnp.maximum, lambda x, c, r0: jnp.where(x < hi, x, -inf), -inf)
            t = jnp.where(done, thr_ref[...], jnp.max(nxt, axis=0, keepdims=True))
            cnt = count(lambda x, tt: x >= tt, t)
            thr_ref[...] = t
            clo_ref[...] = jnp.where(done, clo_ref[...], cnt)
            hi_ref[...] = jnp.where(done, hi, t)
            return any_lane(jnp.logical_and(jnp.logical_not(done), cnt < kf))

        lax.while_loop(lambda go: go > 0, peel_body, any_lane(clo_ref[...] < kf))

        jst_ref[...] = jnp.where(searched, s_len, -1)
        thr_ref[...] = jnp.where(searched, thr_ref[...], -inf)
        excess = jnp.logical_and(searched, clo_ref[...] > kf)

        @pl.when(any_lane(excess) > 0)
        def _():
            thr = thr_ref[...]
            need = kf - count(lambda x, t: x > t, thr)
            tril = jnp.where(lax.broadcasted_iota(jnp.int32, (kc, kc), 0) >=
                             lax.broadcasted_iota(jnp.int32, (kc, kc), 1), 1.0, 0.0).astype(jnp.bfloat16)

            def tie_body(c, carry):
                seen, last = carry
                tie = sc_ref[c] == thr
                rank = jnp.dot(tril, jnp.where(tie, 1.0, 0.0).astype(jnp.bfloat16),
                               preferred_element_type=jnp.float32) + seen
                keep = jnp.logical_and(tie, rank <= need)
                kept_pos = jnp.where(keep, key_pos(c).astype(jnp.float32), -1.0)
                return rank[kc - 1:kc, :], jnp.maximum(last, jnp.max(kept_pos, axis=0, keepdims=True))

            _, last = lax.fori_loop(0, nch, tie_body, (jnp.zeros((1, QBE), jnp.float32),
                                                       jnp.full((1, QBE), -1.0, jnp.float32)))
            jst_ref[...] = jnp.where(excess, last.astype(jnp.int32), jst_ref[...])

    for h in range(HEADS):
        hs = slice(h * HEAD_DIM, (h + 1) * HEAD_DIM)
        ql = lax.dot_general(q_ref[:, hs], wuk_ref[h], _NT, preferred_element_type=jnp.float32)
        qlat_ref[h * QBE:(h + 1) * QBE, :] = (ql * (HEAD_DIM ** -0.5 * LOG2E)).astype(qlat_ref.dtype)
    m_ref[...] = jnp.full_like(m_ref, NEG_BIG)
    l_ref[...] = jnp.zeros_like(l_ref)
    acc_ref[...] = jnp.zeros_like(acc_ref)
    thr = thr_ref[...]
    jst = jst_ref[...]

    def logits_into(buf_ref, c):
        off = pl.multiple_of(c * kc, kc)
        buf_ref[...] = lax.dot_general(cn_ref[pl.ds(off, kc), :], qlat_ref[...], _NT,
                                       preferred_element_type=jnp.float32)

    def softmax_pv(buf_ref, c):
        sc = sc_ref[c]
        sel = jnp.logical_or(sc > thr, jnp.logical_and(sc == thr, key_pos(c) <= jst))
        bias_ref[...] = jnp.where(sel, 0.0, NEG_BIG)
        m_old = m_ref[...]
        m_new = jnp.maximum(m_old, jnp.concatenate(
            [jnp.max(buf_ref[:, h * QBE:(h + 1) * QBE] + bias_ref[...], axis=0, keepdims=True)
             for h in range(HEADS)], axis=1))
        alpha = jnp.exp2(m_old - m_new)
        psum = []
        for h in range(HEADS):
            qs = slice(h * QBE, (h + 1) * QBE)
            p = jnp.exp2(buf_ref[:, qs] + bias_ref[...] - m_new[:, qs])
            psum.append(jnp.sum(p, axis=0, keepdims=True))
            p_ref[:, qs] = p.astype(p_ref.dtype)
        l_ref[...] = alpha * l_ref[...] + jnp.concatenate(psum, axis=1)
        pv = jnp.dot(cnt_ref[c], p_ref[...], preferred_element_type=jnp.float32)
        acc_ref[...] = alpha * acc_ref[...] + pv
        m_ref[...] = m_new

    run_pairs(logits_into, softmax_pv)

    for h in range(HEADS):
        hs = slice(h * HEAD_DIM, (h + 1) * HEAD_DIM)
        qs = slice(h * QBE, (h + 1) * QBE)
        o_lat = jnp.transpose(acc_ref[:, qs] / l_ref[:, qs]).astype(jnp.bfloat16)
        o = jnp.dot(o_lat, wuv_ref[h], preferred_element_type=jnp.float32)
        o_ref[:, hs] = (o * _silu(z_ref[:, hs])).astype(o_ref.dtype)


def _dsat(p16b, p32b, kv_g, w_uk, w_uv):
    b, s, _ = p16b.shape
    kc = min(KC, s)
    ksel = min(TOPK_MAX, s // 4)
    assert s % kc == 0 and kc % QBE == 0
    nq = HEADS * QBE
    row = lambda n: pltpu.VMEM((1, n), jnp.float32)
    full3 = pl.BlockSpec((HEADS, LATENT, HEAD_DIM), lambda bi, i: (0, 0, 0))
    return pl.pallas_call(
        functools.partial(_dsat_kernel, s_len=s, kc=kc, ksel=ksel, n_bisect=16),
        grid=(b, s // QBE),
        in_specs=[pl.BlockSpec((None, QBE, W), lambda bi, i: (bi, i, P16_BQ)),
                  pl.BlockSpec((None, QBE, W), lambda bi, i: (bi, i, P16_QI)),
                  pl.BlockSpec((None, QBE, LANE), lambda bi, i: (bi, i, P32_SM)),
                  pl.BlockSpec((None, QBE, W), lambda bi, i: (bi, i, P32_BZ)),
                  pl.BlockSpec((None, s, LANE), lambda bi, i: (bi, 0, P16_KI)),
                  pl.BlockSpec((None, s, LANE), lambda bi, i: (bi, 0, P32_BC)),
                  pl.BlockSpec((1, LATENT), lambda bi, i: (0, 0)),
                  full3, full3],
        out_specs=pl.BlockSpec((None, QBE, W), lambda bi, i: (bi, i, 0)),
        out_shape=jax.ShapeDtypeStruct((b, s, W), jnp.bfloat16),
        scratch_shapes=[pltpu.VMEM((s, LATENT), jnp.bfloat16),
                        pltpu.VMEM((s // kc, LATENT, kc), jnp.bfloat16),
                        pltpu.VMEM((s // kc, kc, QBE), jnp.float32),
                        pltpu.VMEM((IDX_HEADS * QBE, LANE), jnp.bfloat16),
                        pltpu.VMEM((nq, LATENT), jnp.bfloat16),
                        pltpu.VMEM((kc, nq), jnp.float32),
                        pltpu.VMEM((kc, nq), jnp.float32),
                        pltpu.VMEM((kc, QBE), jnp.float32),
                        pltpu.VMEM((kc, nq), jnp.bfloat16),
                        row(QBE),
                        pltpu.VMEM((1, QBE), jnp.int32),
                        row(QBE), row(QBE), row(QBE), row(QBE),
                        row(nq), row(nq),
                        pltpu.VMEM((LATENT, nq), jnp.float32)],
        compiler_params=_cparams(("arbitrary", "arbitrary")),
        name="dsa_attn",
    )(p16b, p16b, p32b, p32b, p16b, p32b, kv_g.reshape(1, LATENT),
      w_uk.astype(jnp.bfloat16), w_uv.astype(jnp.bfloat16))


def _merge_kernel(ya_ref, yb_ref, yc_ref, yd_ref, g_ref, x_ref, wb_ref, wo_ref, gn_ref, h_ref, hn_ref):
    merged = None
    for n, y_ref in enumerate((ya_ref, yb_ref, yc_ref, yd_ref)):
        lifted = jnp.dot(y_ref[...], wb_ref[n], preferred_element_type=jnp.float32)
        term = _sigmoid(g_ref[:, n * D_MODEL:(n + 1) * D_MODEL]) * lifted
        merged = term if merged is None else merged + term
    h = x_ref[...] + jnp.dot(merged.astype(jnp.bfloat16), wo_ref[...], preferred_element_type=jnp.float32)
    h_ref[...] = h
    hn = h * lax.rsqrt(jnp.mean(h * h, axis=-1, keepdims=True) + EPS) * gn_ref[...]
    hn_ref[...] = hn.astype(hn_ref.dtype)


def _merge(ya, yb, yc, yd, p32, h, wb, wo, g_next, hn_dtype, tm=256):
    m, d = h.shape
    yblk = pl.BlockSpec((tm, W), lambda i: (i, 0))
    hblk = pl.BlockSpec((tm, d), lambda i: (i, 0))
    return pl.pallas_call(
        _merge_kernel,
        grid=(m // tm,),
        in_specs=[yblk, yblk, yblk, yblk,
                  pl.BlockSpec((tm, N_BRANCH * d), lambda i: (i, 0)),
                  hblk,
                  pl.BlockSpec((N_BRANCH, W, d), lambda i: (0, 0, 0)),
                  pl.BlockSpec((d, d), lambda i: (0, 0)),
                  pl.BlockSpec((1, d), lambda i: (0, 0))],
        out_specs=[hblk, hblk],
        out_shape=[jax.ShapeDtypeStruct((m, d), jnp.float32), jax.ShapeDtypeStruct((m, d), hn_dtype)],
        compiler_params=_cparams(("arbitrary",)),
        name="merge",
    )(ya, yb, yc, yd, p32, h, wb, wo, g_next.reshape(1, d))


def kernel(x, norm_g, w_in, gm_ln_g, gm_ln_b, gm_w_s, gm_b_s, dsa_kv_g, dsa_w_uk, dsa_w_uv,
           conv_w, fox_b_f, w_branch, w_out, final_g):
    b, s, d = x.shape
    depth = w_in.shape[0]
    m = b * s
    h = x.reshape(m, d)
    hn = _rmsnorm(h, norm_g[0], jnp.bfloat16)
    for l in range(depth):
        w16, w32, wvt = _prep_w_in(w_in[l])
        p16 = _matmul(hn, w16, jnp.bfloat16, tm=512, tn=N16, name="in_proj16")
        p32 = _matmul(hn, w32, jnp.float32, tm=512, tn=N32 // 5, name="in_proj32")
        vt = _matmul_t(hn, wvt, tm=min(KC, s), name="in_proj_vt")
        p16b = p16.reshape(b, s, N16)
        p32b = p32.reshape(b, s, N32)
        ya, yc = _mix_ac(p32b, gm_ln_g[l], gm_ln_b[l], gm_w_s[l], gm_b_s[l], conv_w[l])
        qaug, kaug = _fox_cum(p32b, fox_b_f[l])
        yd = _fox(p16b, p32b, qaug, kaug, vt)
        yb = _dsat(p16b, p32b, dsa_kv_g[l], dsa_w_uk[l], dsa_w_uv[l])
        last = l == depth - 1
        g_next = final_g if last else norm_g[l + 1]
        h, hn = _merge(ya.reshape(m, W), yb.reshape(m, W), yc.reshape(m, W), yd.reshape(m, W), p32, h,
                       w_branch[l].astype(jnp.bfloat16), w_out[l].astype(jnp.bfloat16), g_next,
                       jnp.float32 if last else jnp.bfloat16)
    return hn.reshape(b, s, d)
```

```python
import functools

import numpy as np
import jax
import jax.numpy as jnp
from jax import lax
from jax.experimental import pallas as pl
from jax.experimental.pallas import tpu as pltpu

D_MODEL = 1024
N_BRANCH = 4
W = 512
EPS = 1e-6
QB = 128
QBE = 256
KC = 512
FOLD = 64
CUM_ROWS = 512
GM_GROUPS = 4
GM_CHUNK = 128
HEADS = 4
HEAD_DIM = W // HEADS
LATENT = 128
IDX_HEADS = 4
IDX_DIM = 64
TOPK_MAX = 256
CONV_WIDTH = 3
LANE = 128
NEG_BIG = -1e30
LOG2E = 1.4426950408889634
VMEM_LIMIT = 56 * 1024 * 1024

P16_DQ, P16_DK, P16_BQ = 0, 1, 2
P16_QI = 3
P16_KI = 16
N16 = 17 * LANE
P32_AU, P32_AV, P32_AZ, P32_BZ = 8, 9, 10, 11
P32_CB, P32_CC, P32_CX, P32_CZ, P32_DZ = 12, 13, 14, 15, 16
P32_BC = 68
P32_SM = 69
N32 = 70 * LANE

_NT = (((1,), (1,)), ((), ()))


def _cparams(sem):
    return pltpu.CompilerParams(dimension_semantics=sem, vmem_limit_bytes=VMEM_LIMIT)


def _sigmoid(z):
    return 1.0 / (1.0 + jnp.exp(-z))


def _silu(z):
    return z * _sigmoid(z)


def _rmsnorm_kernel(x_ref, g_ref, o_ref):
    x = x_ref[...]
    y = x * lax.rsqrt(jnp.mean(x * x, axis=-1, keepdims=True) + EPS)
    o_ref[...] = (y * g_ref[...]).astype(o_ref.dtype)


def _rmsnorm(x2, g, out_dtype, tm=512):
    m, d = x2.shape
    return pl.pallas_call(
        _rmsnorm_kernel,
        grid=(m // tm,),
        in_specs=[pl.BlockSpec((tm, d), lambda i: (i, 0)), pl.BlockSpec((1, d), lambda i: (0, 0))],
        out_specs=pl.BlockSpec((tm, d), lambda i: (i, 0)),
        out_shape=jax.ShapeDtypeStruct((m, d), out_dtype),
        compiler_params=_cparams(("arbitrary",)),
        name="rmsnorm",
    )(x2, g.reshape(1, d))


def _matmul_kernel(a_ref, w_ref, o_ref):
    o_ref[...] = jnp.dot(a_ref[...], w_ref[...], preferred_element_type=jnp.float32).astype(o_ref.dtype)


def _matmul(a, w, out_dtype, tm, tn, name):
    m, k = a.shape
    n = w.shape[1]
    return pl.pallas_call(
        _matmul_kernel,
        grid=(n // tn, m // tm),
        in_specs=[pl.BlockSpec((tm, k), lambda j, i: (i, 0)), pl.BlockSpec((k, tn), lambda j, i: (0, j))],
        out_specs=pl.BlockSpec((tm, tn), lambda j, i: (i, j)),
        out_shape=jax.ShapeDtypeStruct((m, n), out_dtype),
        compiler_params=_cparams(("arbitrary", "arbitrary")),
        name=name,
    )(a, w)


def _prep_w_in(w):
    d = w.shape[0]
    sizes = (W, W, W,
             W, LATENT, IDX_HEADS * IDX_DIM, IDX_DIM, IDX_HEADS, W,
             W, W, W, W,
             W, W, W, HEADS, W,
             N_BRANCH * D_MODEL)
    parts, off = [], 0
    for s in sizes:
        parts.append(w[:, off:off + s])
        off += s
    (a_u, a_v, a_z, b_q, b_c, b_qi, b_ki, b_wi, b_z,
     c_b, c_c, c_x, c_z, d_q, d_k, d_v, d_f, d_z, gates) = parts
    zeros = lambda n: jnp.zeros((d, n), w.dtype)
    qi = []
    for h in range(IDX_HEADS):
        qi += [b_qi[:, h * IDX_DIM:(h + 1) * IDX_DIM], zeros(LANE - IDX_DIM)]
    w16 = jnp.concatenate([d_q, d_k, b_q] + qi + [b_ki, zeros(LANE - IDX_DIM)], axis=1)
    w32 = jnp.concatenate([gates, a_u, a_v, a_z, b_z, c_b, c_c, c_x, c_z, d_z, b_c,
                           d_f, b_wi, zeros(LANE - HEADS - IDX_HEADS)], axis=1)
    return w16.astype(jnp.bfloat16), w32.astype(jnp.bfloat16), jnp.transpose(d_v).astype(jnp.bfloat16)


def _matmul_t_kernel(wt_ref, a_ref, o_ref):
    o_ref[...] = lax.dot_general(wt_ref[...], a_ref[...], _NT,
                                 preferred_element_type=jnp.float32).astype(o_ref.dtype)


def _matmul_t(a, wt, tm, name):
    m, k = a.shape
    n = wt.shape[0]
    return pl.pallas_call(
        _matmul_t_kernel,
        grid=(m // tm,),
        in_specs=[pl.BlockSpec((n, k), lambda i: (0, 0)), pl.BlockSpec((tm, k), lambda i: (i, 0))],
        out_specs=pl.BlockSpec((None, n, tm), lambda i: (i, 0, 0)),
        out_shape=jax.ShapeDtypeStruct((m // tm, n, tm), jnp.bfloat16),
        compiler_params=_cparams(("arbitrary",)),
        name=name,
    )(wt, a)


def _mix_ac_kernel(au_ref, av_ref, az_ref, cb_ref, cc_ref, cx_ref, cz_ref,
                   lng_ref, lnb_ref, ws_ref, bs_ref, cw_ref, ya_ref, yc_ref, halo_ref, *, tt):
    t = pl.program_id(1)

    v = av_ref[...]
    mu = jnp.mean(v, axis=-1, keepdims=True)
    vc = v - mu
    var = jnp.mean(vc * vc, axis=-1, keepdims=True)
    vn = (vc * lax.rsqrt(var + EPS) * lng_ref[...] + lnb_ref[...]).astype(jnp.bfloat16)
    row = lax.broadcasted_iota(jnp.int32, (GM_CHUNK, GM_CHUNK), 0)
    col = lax.broadcasted_iota(jnp.int32, (GM_CHUNK, GM_CHUNK), 1)
    tril = row >= col
    wg = [jnp.where(tril, ws_ref[g], 0.0).astype(jnp.bfloat16) for g in range(GM_GROUPS)]
    for ch in range(tt // GM_CHUNK):
        rows = slice(ch * GM_CHUNK, (ch + 1) * GM_CHUNK)
        for g in range(GM_GROUPS):
            cols = slice(g * LANE, (g + 1) * LANE)
            mixed = jnp.dot(wg[g], vn[rows, cols], preferred_element_type=jnp.float32) + bs_ref[:, cols]
            ya_ref[rows, cols] = (au_ref[rows, cols] * mixed * _silu(az_ref[rows, cols])).astype(ya_ref.dtype)

    @pl.when(t == 0)
    def _():
        halo_ref[...] = jnp.zeros_like(halo_ref)

    y = cc_ref[...] * cx_ref[...]
    ext = jnp.concatenate([halo_ref[...], y], axis=0)
    conv = cw_ref[2:3, :] * y
    for j in range(CONV_WIDTH - 1):
        shift = CONV_WIDTH - 1 - j
        conv = conv + cw_ref[j:j + 1, :] * ext[8 - shift:8 - shift + tt, :]
    yc_ref[...] = (cb_ref[...] * conv * _silu(cz_ref[...])).astype(yc_ref.dtype)
    halo_ref[...] = y[tt - 8:, :]


def _mix_ac(p32b, ln_g, ln_b, w_s, b_s, conv_w, tt=256):
    b, s, _ = p32b.shape
    blk = lambda idx: pl.BlockSpec((None, tt, W), lambda bi, ti, idx=idx: (bi, ti, idx))
    full2 = lambda shp: pl.BlockSpec(shp, lambda bi, ti: (0,) * len(shp))
    bs_full = jnp.repeat(jnp.transpose(b_s), LANE, axis=1)
    out = jax.ShapeDtypeStruct((b, s, W), jnp.bfloat16)
    return pl.pallas_call(
        functools.partial(_mix_ac_kernel, tt=tt),
        grid=(b, s // tt),
        in_specs=[blk(P32_AU), blk(P32_AV), blk(P32_AZ), blk(P32_CB), blk(P32_CC), blk(P32_CX), blk(P32_CZ),
                  full2((1, W)), full2((1, W)), full2((GM_GROUPS, GM_CHUNK, GM_CHUNK)),
                  full2((GM_CHUNK, W)), full2((CONV_WIDTH, W))],
        out_specs=[pl.BlockSpec((None, tt, W), lambda bi, ti: (bi, ti, 0))] * 2,
        out_shape=[out, out],
        scratch_shapes=[pltpu.VMEM((8, W), jnp.float32)],
        compiler_params=_cparams(("arbitrary", "arbitrary")),
        name="mix_ac",
    )(p32b, p32b, p32b, p32b, p32b, p32b, p32b,
      ln_g.reshape(1, W), ln_b.reshape(1, W), w_s, bs_full, conv_w)


def _split3(x):
    hi = x.astype(jnp.bfloat16)
    r1 = x - hi.astype(jnp.float32)
    mid = r1.astype(jnp.bfloat16)
    lo = (r1 - mid.astype(jnp.float32)).astype(jnp.bfloat16)
    return hi, mid, lo


def _aug_placement():
    pq = np.zeros((3 * LANE, W), np.float32)
    pk = np.zeros((3 * LANE, W), np.float32)
    cq = np.zeros((1, W), np.float32)
    ck = np.zeros((1, W), np.float32)
    for h in range(HEADS):
        for j in range(3):
            pq[j * LANE + h, h * HEAD_DIM + j] = 1.0
            cq[0, h * HEAD_DIM + 3 + j] = 1.0
            pk[j * LANE + h, h * HEAD_DIM + 3 + j] = -1.0
            ck[0, h * HEAD_DIM + j] = 1.0
    return (jnp.asarray(pq, jnp.bfloat16), jnp.asarray(pk, jnp.bfloat16), jnp.asarray(cq), jnp.asarray(ck))


def _fox_cum_kernel(f_ref, bias_ref, pq_ref, pk_ref, cq_ref, ck_ref, qa_ref, ka_ref, carry_ref, *, s):
    rows = min(CUM_ROWS, s)
    row = lax.broadcasted_iota(jnp.int32, (rows, rows), 0)
    col = lax.broadcasted_iota(jnp.int32, (rows, rows), 1)
    ones_tril = jnp.where(row >= col, 1.0, 0.0).astype(jnp.bfloat16)
    carry_ref[...] = jnp.zeros_like(carry_ref)

    def body(c, _):
        off = pl.multiple_of(c * rows, rows)
        x = f_ref[pl.ds(off, rows), :] + bias_ref[...]
        ls = jnp.minimum(x, 0.0) - jnp.log1p(jnp.exp(-jnp.abs(x)))
        hi, mid, lo = _split3(ls * LOG2E)
        dot = lambda p: jnp.dot(ones_tril, p, preferred_element_type=jnp.float32)
        cs = (dot(hi) + dot(mid)) + dot(lo) + carry_ref[0:1, :]
        carry_ref[0:1, :] = cs[rows - 1:rows, :]
        parts = jnp.concatenate(_split3(cs), axis=1)
        qa = jnp.dot(parts, pq_ref[...], preferred_element_type=jnp.float32) + cq_ref[...]
        ka = jnp.dot(parts, pk_ref[...], preferred_element_type=jnp.float32) + ck_ref[...]
        qa_ref[pl.ds(off, rows), :] = qa.astype(qa_ref.dtype)
        ka_ref[pl.ds(off, rows), :] = ka.astype(ka_ref.dtype)
        return 0

    lax.fori_loop(0, s // rows, body, 0)


def _fox_cum(p32b, b_f):
    b, s, _ = p32b.shape
    bias = jnp.zeros((1, LANE), jnp.float32).at[0, :HEADS].set(b_f)
    pq, pk, cq, ck = _aug_placement()
    const = lambda shp: pl.BlockSpec(shp, lambda bi: (0, 0))
    out = jax.ShapeDtypeStruct((b, s, W), jnp.bfloat16)
    return pl.pallas_call(
        functools.partial(_fox_cum_kernel, s=s),
        grid=(b,),
        in_specs=[pl.BlockSpec((None, s, LANE), lambda bi: (bi, 0, P32_SM)), const((1, LANE)),
                  const((3 * LANE, W)), const((3 * LANE, W)), const((1, W)), const((1, W))],
        out_specs=[pl.BlockSpec((None, s, W), lambda bi: (bi, 0, 0))] * 2,
        out_shape=[out, out],
        scratch_shapes=[pltpu.VMEM((8, LANE), jnp.float32)],
        compiler_params=_cparams(("arbitrary",)),
        name="fox_cum",
    )(p32b, bias, pq, pk, cq, ck)


def _num_chunks(i, kc):
    return ((i + 1) * QB + kc - 1) // kc


def _fox_kernel(q_ref, qa_ref, k_ref, ka_ref, vt_ref, z_ref, o_ref,
                qf_ref, sa_ref, sb_ref, m_ref, l_ref, acc_ref, *, kc):
    i = pl.program_id(1)
    ndiag = (i * QBE) // kc
    scale = HEAD_DIM ** -0.5 * LOG2E
    for h in range(HEADS):
        hs = slice(h * HEAD_DIM, (h + 1) * HEAD_DIM)
        qs = (q_ref[:, hs].astype(jnp.float32) * scale).astype(jnp.bfloat16)
        qf_ref[h] = jnp.concatenate([qs, qa_ref[:, hs]], axis=1)
    m_ref[...] = jnp.full_like(m_ref, NEG_BIG)
    l_ref[...] = jnp.zeros_like(l_ref)
    acc_ref[...] = jnp.zeros_like(acc_ref)

    def scores_into(buf_ref, c):
        off = pl.multiple_of(c * kc, kc)
        for h in range(HEADS):
            hs = slice(h * HEAD_DIM, (h + 1) * HEAD_DIM)
            kf = jnp.concatenate([k_ref[pl.ds(off, kc), hs], ka_ref[pl.ds(off, kc), hs]], axis=1)
            buf_ref[:, h * QBE:(h + 1) * QBE] = lax.dot_general(
                kf, qf_ref[h], _NT, preferred_element_type=jnp.float32)

    def softmax_pv(buf_ref, c, masked):
        if masked:
            causal = (c * kc + lax.broadcasted_iota(jnp.int32, (kc, QBE), 0) <=
                      i * QBE + lax.broadcasted_iota(jnp.int32, (kc, QBE), 1))
        for h in range(HEADS):
            hs = slice(h * HEAD_DIM, (h + 1) * HEAD_DIM)
            st = buf_ref[:, h * QBE:(h + 1) * QBE]
            if masked:
                st = jnp.where(causal, st, NEG_BIG)
            m_old = m_ref[h]
            m_new = jnp.maximum(m_old, jnp.max(st, axis=0, keepdims=True))
            alpha = jnp.exp2(m_old - m_new)
            p = jnp.exp2(st - m_new)
            l_ref[h] = alpha * l_ref[h] + jnp.sum(p, axis=0, keepdims=True)
            pv = jnp.dot(vt_ref[c, hs, :], p.astype(jnp.bfloat16), preferred_element_type=jnp.float32)
            acc_ref[h] = alpha * acc_ref[h] + pv
            m_ref[h] = m_new

    scores_into(sa_ref, 0)

    def pair_body(j, _):
        scores_into(sb_ref, 2 * j + 1)
        softmax_pv(sa_ref, 2 * j, False)
        scores_into(sa_ref, 2 * j + 2)
        softmax_pv(sb_ref, 2 * j + 1, False)
        return 0

    lax.fori_loop(0, ndiag // 2, pair_body, 0)

    @pl.when(ndiag % 2 == 1)
    def _():
        scores_into(sb_ref, ndiag)
        softmax_pv(sa_ref, ndiag - 1, False)
        softmax_pv(sb_ref, ndiag, True)

    @pl.when(ndiag % 2 == 0)
    def _():
        softmax_pv(sa_ref, ndiag, True)

    for h in range(HEADS):
        hs = slice(h * HEAD_DIM, (h + 1) * HEAD_DIM)
        o = jnp.transpose(acc_ref[h] / l_ref[h])
        o_ref[:, hs] = (o * _silu(z_ref[:, hs])).astype(o_ref.dtype)


def _fox(p16b, p32b, qaug, kaug, vt):
    b, s, _ = p16b.shape
    kc = vt.shape[-1]
    assert s % kc == 0 and kc % QBE == 0
    nck = s // kc
    vt = vt.reshape(b, nck, W, kc)
    row = lambda: pltpu.VMEM((HEADS, 1, QBE), jnp.float32)
    return pl.pallas_call(
        functools.partial(_fox_kernel, kc=kc),
        grid=(b, s // QBE),
        in_specs=[pl.BlockSpec((None, QBE, W), lambda bi, i: (bi, i, P16_DQ)),
                  pl.BlockSpec((None, QBE, W), lambda bi, i: (bi, i, 0)),
                  pl.BlockSpec((None, s, W), lambda bi, i: (bi, 0, P16_DK)),
                  pl.BlockSpec((None, s, W), lambda bi, i: (bi, 0, 0)),
                  pl.BlockSpec((None, nck, W, kc), lambda bi, i: (bi, 0, 0, 0)),
                  pl.BlockSpec((None, QBE, W), lambda bi, i: (bi, i, P32_DZ))],
        out_specs=pl.BlockSpec((None, QBE, W), lambda bi, i: (bi, i, 0)),
        out_shape=jax.ShapeDtypeStruct((b, s, W), jnp.bfloat16),
        scratch_shapes=[pltpu.VMEM((HEADS, QBE, 2 * HEAD_DIM), jnp.bfloat16),
                        pltpu.VMEM((kc, HEADS * QBE), jnp.float32),
                        pltpu.VMEM((kc, HEADS * QBE), jnp.float32),
                        row(), row(),
                        pltpu.VMEM((HEADS, HEAD_DIM, QBE), jnp.float32)],
        compiler_params=_cparams(("arbitrary", "arbitrary")),
        name="fox_attn",
    )(p16b, qaug, p16b, kaug, vt, p32b)


def _dsa_kernel(q_ref, qi_ref, sm_ref, z_ref, ki_ref, c_ref, kvg_ref, wuk_ref, wuv_ref, o_ref,
                cn_ref, sc_ref, thr_ref, jst_ref, lo_ref, hib_ref, hi_ref, clo_ref,
                m_ref, l_ref, acc_ref, *, s_len, kc, ksel, n_bisect):
    i = pl.program_id(1)
    nch = _num_chunks(i, kc)
    qpos = i * QB + lax.broadcasted_iota(jnp.int32, (QB, 1), 0)
    kf = float(ksel)
    inf = float("inf")

    @pl.when(i == 0)
    def _():
        def nbody(c, _):
            off = pl.multiple_of(c * kc, kc)
            x = c_ref[pl.ds(off, kc), :]
            y = x * lax.rsqrt(jnp.mean(x * x, axis=-1, keepdims=True) + EPS) * kvg_ref[...]
            cn_ref[pl.ds(off, kc), :] = y.astype(cn_ref.dtype)
            return 0
        lax.fori_loop(0, s_len // kc, nbody, 0)

    wi = sm_ref[:, HEADS:HEADS + IDX_HEADS] * (IDX_HEADS ** -0.5 * IDX_DIM ** -0.5)

    def sbody(c, _):
        off = pl.multiple_of(c * kc, kc)
        kk = ki_ref[pl.ds(off, kc), :]
        sc = jnp.zeros((QB, kc), jnp.float32)
        for h in range(IDX_HEADS):
            d = lax.dot_general(qi_ref[:, h * LANE:(h + 1) * LANE], kk, _NT, preferred_element_type=jnp.float32)
            sc = sc + wi[:, h:h + 1] * jnp.maximum(d, 0.0)
        kpos = off + lax.broadcasted_iota(jnp.int32, (1, kc), 1)
        sc_ref[c] = jnp.where(kpos <= qpos, sc, -inf)
        return 0

    lax.fori_loop(0, nch, sbody, 0)

    def fold_chunks(fn, init):
        def cbody(c, acc):
            off = pl.multiple_of(c * kc, kc)
            sc = sc_ref[c]
            for j in range(kc // LANE):
                kpos = off + j * LANE + lax.broadcasted_iota(jnp.int32, (1, LANE), 1)
                acc = fn(acc, sc[:, j * LANE:(j + 1) * LANE], kpos)
            return acc
        return lax.fori_loop(0, nch, cbody, init)

    def count(pred, t):
        tb = jnp.broadcast_to(t, (QB, LANE))
        acc = fold_chunks(lambda a, x, kp: a + jnp.where(pred(x, tb, kp), 1.0, 0.0),
                          jnp.zeros((QB, LANE), jnp.float32))
        return jnp.sum(acc, axis=-1, keepdims=True)

    def any_row(flag):
        return (jnp.max(jnp.where(flag, 1.0, 0.0)) > 0.5).astype(jnp.int32)

    thr_ref[...] = jnp.full_like(thr_ref, -inf)
    jst_ref[...] = jnp.full_like(jst_ref, -1)

    @pl.when(i * QB >= ksel)
    def _():
        mn0 = jnp.full((QB, LANE), inf, jnp.float32)
        mx0 = jnp.full((QB, LANE), -inf, jnp.float32)
        mn, mx = fold_chunks(
            lambda a, x, kp: (jnp.minimum(a[0], jnp.where(x == -inf, inf, x)), jnp.maximum(a[1], x)),
            (mn0, mx0))
        lo_ref[...] = jnp.min(mn, axis=-1, keepdims=True)
        hib_ref[...] = jnp.max(mx, axis=-1, keepdims=True)
        hi_ref[...] = jnp.full_like(hi_ref, inf)
        clo_ref[...] = (qpos + 1).astype(jnp.float32)

        def bis_cond(st):
            it, go = st
            return jnp.logical_and(it < n_bisect, go > 0)

        def bis_body(st):
            it, _ = st
            lo, hib = lo_ref[...], hib_ref[...]
            mid = 0.5 * lo + 0.5 * hib
            cnt = count(lambda x, tb, kp: x >= tb, mid)
            ge = cnt >= kf
            lo_ref[...] = jnp.where(ge, mid, lo)
            clo = jnp.where(ge, cnt, clo_ref[...])
            clo_ref[...] = clo
            hib_ref[...] = jnp.where(ge, hib, mid)
            hi_ref[...] = jnp.where(ge, hi_ref[...], mid)
            return it + 1, any_row(clo != kf)

        lax.while_loop(bis_cond, bis_body, (jnp.int32(0), jnp.int32(1)))

        resolved = clo_ref[...] == kf
        thr_ref[...] = jnp.where(resolved, lo_ref[...], inf)
        clo_ref[...] = jnp.where(resolved, kf, 0.0)

        def peel_body(go):
            done = clo_ref[...] >= kf
            hi = hi_ref[...]
            hb = jnp.broadcast_to(hi, (QB, LANE))
            nxt = fold_chunks(lambda a, x, kp: jnp.maximum(a, jnp.where(x < hb, x, -inf)),
                              jnp.full((QB, LANE), -inf, jnp.float32))
            t = jnp.where(done, thr_ref[...], jnp.max(nxt, axis=-1, keepdims=True))
            cnt = count(lambda x, tb, kp: x >= tb, t)
            thr_ref[...] = t
            clo_ref[...] = cnt
            hi_ref[...] = jnp.where(done, hi, t)
            return any_row(cnt < kf)

        lax.while_loop(lambda go: go > 0, peel_body, any_row(clo_ref[...] < kf))

        jst_ref[...] = jnp.full_like(jst_ref, s_len)

        @pl.when(any_row(clo_ref[...] > kf) > 0)
        def _():
            thr = thr_ref[...]
            need = kf - count(lambda x, tb, kp: x > tb, thr)
            excess = clo_ref[...] > kf

            def tie_body(bi, j):
                cand = j + lax.shift_left(jnp.int32(1), (s_len.bit_length() - 1) - bi)
                cb = jnp.broadcast_to(cand, (QB, LANE))
                tb = jnp.broadcast_to(thr, (QB, LANE))
                acc = fold_chunks(
                    lambda a, x, kp: a + jnp.where(jnp.logical_and(x == tb, kp < cb), 1.0, 0.0),
                    jnp.zeros((QB, LANE), jnp.float32))
                below = jnp.sum(acc, axis=-1, keepdims=True)
                return jnp.where(below < need, cand, j)

            j = lax.fori_loop(0, s_len.bit_length(), tie_body, jnp.zeros((QB, 1), jnp.int32))
            jst_ref[...] = jnp.where(excess, j, s_len)

    qlat = []
    for h in range(HEADS):
        hs = slice(h * HEAD_DIM, (h + 1) * HEAD_DIM)
        ql = lax.dot_general(q_ref[:, hs], wuk_ref[h], _NT, preferred_element_type=jnp.float32)
        qlat.append((ql * HEAD_DIM ** -0.5).astype(jnp.bfloat16))
    qlat = jnp.concatenate(qlat, axis=0)
    m_ref[...] = jnp.full_like(m_ref, NEG_BIG)
    l_ref[...] = jnp.zeros_like(l_ref)
    acc_ref[...] = jnp.zeros_like(acc_ref)
    thr = thr_ref[...]
    jst = jst_ref[...]

    def abody(c, _):
        off = pl.multiple_of(c * kc, kc)
        cn = cn_ref[pl.ds(off, kc), :]
        sc = sc_ref[c]
        kpos = off + lax.broadcasted_iota(jnp.int32, (1, kc), 1)
        sel = jnp.logical_or(sc > thr, jnp.logical_and(sc == thr, kpos <= jst))
        logits = lax.dot_general(qlat, cn, _NT, preferred_element_type=jnp.float32)
        for h in range(HEADS):
            rs = slice(h * QB, (h + 1) * QB)
            s = jnp.where(sel, logits[rs, :], NEG_BIG)
            m_old = m_ref[rs, :]
            m_new = jnp.maximum(m_old, jnp.max(s, axis=-1, keepdims=True))
            alpha = jnp.exp(m_old - m_new)
            p = jnp.exp(s - m_new)
            l_ref[rs, :] = alpha * l_ref[rs, :] + jnp.sum(p, axis=-1, keepdims=True)
            pv = jnp.dot(p.astype(jnp.bfloat16), cn, preferred_element_type=jnp.float32)
            acc_ref[rs, :] = alpha * acc_ref[rs, :] + pv
            m_ref[rs, :] = m_new
        return 0

    lax.fori_loop(0, nch, abody, 0)

    for h in range(HEADS):
        hs = slice(h * HEAD_DIM, (h + 1) * HEAD_DIM)
        rs = slice(h * QB, (h + 1) * QB)
        o_lat = (acc_ref[rs, :] / l_ref[rs, :]).astype(jnp.bfloat16)
        o = jnp.dot(o_lat, wuv_ref[h], preferred_element_type=jnp.float32)
        o_ref[:, hs] = (o * _silu(z_ref[:, hs])).astype(o_ref.dtype)


def _dsa(p16b, p32b, kv_g, w_uk, w_uv):
    b, s, _ = p16b.shape
    kc = min(512, s)
    ksel = min(TOPK_MAX, s // 4)
    assert ksel % QB == 0 and s % kc == 0
    col = lambda: pltpu.VMEM((QB, 1), jnp.float32)
    full3 = pl.BlockSpec((HEADS, LATENT, HEAD_DIM), lambda bi, i: (0, 0, 0))
    return pl.pallas_call(
        functools.partial(_dsa_kernel, s_len=s, kc=kc, ksel=ksel, n_bisect=24),
        grid=(b, s // QB),
        in_specs=[pl.BlockSpec((None, QB, W), lambda bi, i: (bi, i, P16_BQ)),
                  pl.BlockSpec((None, QB, W), lambda bi, i: (bi, i, P16_QI)),
                  pl.BlockSpec((None, QB, LANE), lambda bi, i: (bi, i, P32_SM)),
                  pl.BlockSpec((None, QB, W), lambda bi, i: (bi, i, P32_BZ)),
                  pl.BlockSpec((None, s, LANE), lambda bi, i: (bi, 0, P16_KI)),
                  pl.BlockSpec((None, s, LANE), lambda bi, i: (bi, 0, P32_BC)),
                  pl.BlockSpec((1, LATENT), lambda bi, i: (0, 0)),
                  full3, full3],
        out_specs=pl.BlockSpec((None, QB, W), lambda bi, i: (bi, i, 0)),
        out_shape=jax.ShapeDtypeStruct((b, s, W), jnp.bfloat16),
        scratch_shapes=[pltpu.VMEM((s, LATENT), jnp.bfloat16),
                        pltpu.VMEM((s // kc, QB, kc), jnp.float32),
                        col(),
                        pltpu.VMEM((QB, 1), jnp.int32),
                        col(), col(), col(), col(),
                        pltpu.VMEM((HEADS * QB, 1), jnp.float32),
                        pltpu.VMEM((HEADS * QB, 1), jnp.float32),
                        pltpu.VMEM((HEADS * QB, LATENT), jnp.float32)],
        compiler_params=_cparams(("arbitrary", "arbitrary")),
        name="dsa_attn",
    )(p16b, p16b, p32b, p32b, p16b, p32b, kv_g.reshape(1, LATENT),
      w_uk.astype(jnp.bfloat16), w_uv.astype(jnp.bfloat16))


def _dsat_kernel(q_ref, qi_ref, sm_ref, z_ref, ki_ref, c_ref, kvg_ref, wuk_ref, wuv_ref, o_ref,
                 cn_ref, cnt_ref, sc_ref, qia_ref, qlat_ref, lga_ref, lgb_ref, bias_ref, p_ref,
                 thr_ref, jst_ref, lo_ref, hib_ref, hi_ref, clo_ref,
                 m_ref, l_ref, acc_ref, *, s_len, kc, ksel, n_bisect):
    i = pl.program_id(1)
    ndiag = (i * QBE) // kc
    nch = ndiag + 1
    qpos = i * QBE + lax.broadcasted_iota(jnp.int32, (1, QBE), 1)
    kf = float(ksel)
    inf = float("inf")

    def key_pos(c):
        return c * kc + lax.broadcasted_iota(jnp.int32, (kc, QBE), 0)

    @pl.when(i == 0)
    def _():
        def nbody(c, _):
            off = pl.multiple_of(c * kc, kc)
            x = c_ref[pl.ds(off, kc), :]
            y = x * lax.rsqrt(jnp.mean(x * x, axis=-1, keepdims=True) + EPS) * kvg_ref[...]
            cn_ref[pl.ds(off, kc), :] = y.astype(cn_ref.dtype)
            cnt_ref[c] = jnp.transpose(y).astype(cnt_ref.dtype)
            return 0
        lax.fori_loop(0, s_len // kc, nbody, 0)

    wit = jnp.transpose(sm_ref[...])[HEADS:HEADS + IDX_HEADS, :] * (IDX_HEADS ** -0.5 * IDX_DIM ** -0.5)
    for h in range(IDX_HEADS):
        qia_ref[h * QBE:(h + 1) * QBE, :] = qi_ref[:, h * LANE:(h + 1) * LANE]

    def dots_into(buf_ref, c):
        off = pl.multiple_of(c * kc, kc)
        buf_ref[...] = lax.dot_general(ki_ref[pl.ds(off, kc), :], qia_ref[...], _NT,
                                       preferred_element_type=jnp.float32)

    def scores_from(buf_ref, c):
        sc = None
        for h in range(IDX_HEADS):
            term = wit[h:h + 1, :] * jnp.maximum(buf_ref[:, h * QBE:(h + 1) * QBE], 0.0)
            sc = term if sc is None else sc + term
        sc_ref[c] = sc

    def run_pairs(produce, consume):
        produce(lga_ref, 0)

        def pair_body(j, _):
            produce(lgb_ref, 2 * j + 1)
            consume(lga_ref, 2 * j)
            produce(lga_ref, jnp.minimum(2 * j + 2, ndiag))
            consume(lgb_ref, 2 * j + 1)
            return 0

        lax.fori_loop(0, nch // 2, pair_body, 0)

        @pl.when(nch % 2 == 1)
        def _():
            consume(lga_ref, ndiag)

    run_pairs(dots_into, scores_from)
    sc_ref[ndiag] = jnp.where(key_pos(ndiag) <= qpos, sc_ref[ndiag], -inf)

    def fold_slabs(op, fn, init_val):
        def cbody(c, acc):
            x = sc_ref[c]
            for j in range(kc // FOLD):
                acc = op(acc, fn(x[j * FOLD:(j + 1) * FOLD, :], c, j * FOLD))
            return acc
        return lax.fori_loop(0, nch, cbody, jnp.full((FOLD, QBE), init_val, jnp.float32))

    def count(pred, t):
        acc = fold_slabs(lambda a, m: jnp.where(m, a + 1.0, a), lambda x, c, r0: pred(x, t), 0.0)
        return jnp.sum(acc, axis=0, keepdims=True)

    def any_lane(flag):
        return (jnp.max(jnp.where(flag, 1.0, 0.0)) > 0.5).astype(jnp.int32)

    searched = qpos + 1 > ksel
    thr_ref[...] = jnp.full_like(thr_ref, -inf)
    jst_ref[...] = jnp.full_like(jst_ref, -1)

    @pl.when((i + 1) * QBE > ksel)
    def _():
        mn = fold_slabs(jnp.minimum, lambda x, c, r0: jnp.where(x == -inf, inf, x), inf)
        mx = fold_slabs(jnp.maximum, lambda x, c, r0: x, -inf)
        lo_ref[...] = jnp.min(mn, axis=0, keepdims=True)
        hib_ref[...] = jnp.max(mx, axis=0, keepdims=True)
        hi_ref[...] = jnp.full_like(hi_ref, inf)
        clo_ref[...] = (qpos + 1).astype(jnp.float32)

        def bis_body(_, carry):
            lo, hib = lo_ref[...], hib_ref[...]
            mid = 0.5 * lo + 0.5 * hib
            cnt = count(lambda x, t: x >= t, mid)
            ge = cnt >= kf
            lo_ref[...] = jnp.where(ge, mid, lo)
            clo_ref[...] = jnp.where(ge, cnt, clo_ref[...])
            hib_ref[...] = jnp.where(ge, hib, mid)
            hi_ref[...] = jnp.where(ge, hi_ref[...], mid)
            return carry

        lax.fori_loop(0, n_bisect, bis_body, 0)

        resolved = jnp.logical_or(clo_ref[...] == kf, jnp.logical_not(searched))
        thr_ref[...] = jnp.where(resolved, lo_ref[...], inf)
        clo_ref[...] = jnp.where(resolved, kf, 0.0)

        def peel_body(go):
            done = clo_ref[...] >= kf
            hi = hi_ref[...]
            nxt = fold_slabs(jnp.maximum, lambda x, c, r0: jnp.where(x < hi, x, -inf), -inf)
            t = jnp.where(done, thr_ref[...], jnp.max(nxt, axis=0, keepdims=True))
            cnt = count(lambda x, tt: x >= tt, t)
            thr_ref[...] = t
            clo_ref[...] = jnp.where(done, clo_ref[...], cnt)
            hi_ref[...] = jnp.where(done, hi, t)
            return any_lane(jnp.logical_and(jnp.logical_not(done), cnt < kf))

        lax.while_loop(lambda go: go > 0, peel_body, any_lane(clo_ref[...] < kf))

        jst_ref[...] = jnp.where(searched, s_len, -1)
        thr_ref[...] = jnp.where(searched, thr_ref[...], -inf)
        excess = jnp.logical_and(searched, clo_ref[...] > kf)

        @pl.when(any_lane(excess) > 0)
        def _():
            thr = thr_ref[...]
            need = kf - count(lambda x, t: x > t, thr)
            tril = jnp.where(lax.broadcasted_iota(jnp.int32, (kc, kc), 0) >=
                             lax.broadcasted_iota(jnp.int32, (kc, kc), 1), 1.0, 0.0).astype(jnp.bfloat16)

            def tie_body(c, carry):
                seen, last = carry
                tie = sc_ref[c] == thr
                rank = jnp.dot(tril, jnp.where(tie, 1.0, 0.0).astype(jnp.bfloat16),
                               preferred_element_type=jnp.float32) + seen
                keep = jnp.logical_and(tie, rank <= need)
                kept_pos = jnp.where(keep, key_pos(c).astype(jnp.float32), -1.0)
                return rank[kc - 1:kc, :], jnp.maximum(last, jnp.max(kept_pos, axis=0, keepdims=True))

            _, last = lax.fori_loop(0, nch, tie_body, (jnp.zeros((1, QBE), jnp.float32),
                                                       jnp.full((1, QBE), -1.0, jnp.float32)))
            jst_ref[...] = jnp.where(excess, last.astype(jnp.int32), jst_ref[...])

    for h in range(HEADS):
        hs = slice(h * HEAD_DIM, (h + 1) * HEAD_DIM)
        ql = lax.dot_general(q_ref[:, hs], wuk_ref[h], _NT, preferred_element_type=jnp.float32)
        qlat_ref[h * QBE:(h + 1) * QBE, :] = (ql * (HEAD_DIM ** -0.5 * LOG2E)).astype(qlat_ref.dtype)
    m_ref[...] = jnp.full_like(m_ref, NEG_BIG)
    l_ref[...] = jnp.zeros_like(l_ref)
    acc_ref[...] = jnp.zeros_like(acc_ref)
    thr = thr_ref[...]
    jst = jst_ref[...]

    def logits_into(buf_ref, c):
        off = pl.multiple_of(c * kc, kc)
        buf_ref[...] = lax.dot_general(cn_ref[pl.ds(off, kc), :], qlat_ref[...], _NT,
                                       preferred_element_type=jnp.float32)

    def softmax_pv(buf_ref, c):
        sc = sc_ref[c]
        sel = jnp.logical_or(sc > thr, jnp.logical_and(sc == thr, key_pos(c) <= jst))
        bias_ref[...] = jnp.where(sel, 0.0, NEG_BIG)
        m_old = m_ref[...]
        m_new = jnp.maximum(m_old, jnp.concatenate(
            [jnp.max(buf_ref[:, h * QBE:(h + 1) * QBE] + bias_ref[...], axis=0, keepdims=True)
             for h in range(HEADS)], axis=1))
        alpha = jnp.exp2(m_old - m_new)
        psum = []
        for h in range(HEADS):
            qs = slice(h * QBE, (h + 1) * QBE)
            p = jnp.exp2(buf_ref[:, qs] + bias_ref[...] - m_new[:, qs])
            psum.append(jnp.sum(p, axis=0, keepdims=True))
            p_ref[:, qs] = p.astype(p_ref.dtype)
        l_ref[...] = alpha * l_ref[...] + jnp.concatenate(psum, axis=1)
        pv = jnp.dot(cnt_ref[c], p_ref[...], preferred_element_type=jnp.float32)
        acc_ref[...] = alpha * acc_ref[...] + pv
        m_ref[...] = m_new

    run_pairs(logits_into, softmax_pv)

    for h in range(HEADS):
        hs = slice(h * HEAD_DIM, (h + 1) * HEAD_DIM)
        qs = slice(h * QBE, (h + 1) * QBE)
        o_lat = jnp.transpose(acc_ref[:, qs] / l_ref[:, qs]).astype(jnp.bfloat16)
        o = jnp.dot(o_lat, wuv_ref[h], preferred_element_type=jnp.float32)
        o_ref[:, hs] = (o * _silu(z_ref[:, hs])).astype(o_ref.dtype)


def _dsat(p16b, p32b, kv_g, w_uk, w_uv):
    b, s, _ = p16b.shape
    kc = min(KC, s)
    ksel = min(TOPK_MAX, s // 4)
    assert s % kc == 0 and kc % QBE == 0
    nq = HEADS * QBE
    row = lambda n: pltpu.VMEM((1, n), jnp.float32)
    full3 = pl.BlockSpec((HEADS, LATENT, HEAD_DIM), lambda bi, i: (0, 0, 0))
    return pl.pallas_call(
        functools.partial(_dsat_kernel, s_len=s, kc=kc, ksel=ksel, n_bisect=16),
        grid=(b, s // QBE),
        in_specs=[pl.BlockSpec((None, QBE, W), lambda bi, i: (bi, i, P16_BQ)),
                  pl.BlockSpec((None, QBE, W), lambda bi, i: (bi, i, P16_QI)),
                  pl.BlockSpec((None, QBE, LANE), lambda bi, i: (bi, i, P32_SM)),
                  pl.BlockSpec((None, QBE, W), lambda bi, i: (bi, i, P32_BZ)),
                  pl.BlockSpec((None, s, LANE), lambda bi, i: (bi, 0, P16_KI)),
                  pl.BlockSpec((None, s, LANE), lambda bi, i: (bi, 0, P32_BC)),
                  pl.BlockSpec((1, LATENT), lambda bi, i: (0, 0)),
                  full3, full3],
        out_specs=pl.BlockSpec((None, QBE, W), lambda bi, i: (bi, i, 0)),
        out_shape=jax.ShapeDtypeStruct((b, s, W), jnp.bfloat16),
        scratch_shapes=[pltpu.VMEM((s, LATENT), jnp.bfloat16),
                        pltpu.VMEM((s // kc, LATENT, kc), jnp.bfloat16),
                        pltpu.VMEM((s // kc, kc, QBE), jnp.float32),
                        pltpu.VMEM((IDX_HEADS * QBE, LANE), jnp.bfloat16),
                        pltpu.VMEM((nq, LATENT), jnp.bfloat16),
                        pltpu.VMEM((kc, nq), jnp.float32),
                        pltpu.VMEM((kc, nq), jnp.float32),
                        pltpu.VMEM((kc, QBE), jnp.float32),
                        pltpu.VMEM((kc, nq), jnp.bfloat16),
                        row(QBE),
                        pltpu.VMEM((1, QBE), jnp.int32),
                        row(QBE), row(QBE), row(QBE), row(QBE),
                        row(nq), row(nq),
                        pltpu.VMEM((LATENT, nq), jnp.float32)],
        compiler_params=_cparams(("arbitrary", "arbitrary")),
        name="dsa_attn",
    )(p16b, p16b, p32b, p32b, p16b, p32b, kv_g.reshape(1, LATENT),
      w_uk.astype(jnp.bfloat16), w_uv.astype(jnp.bfloat16))


def _merge_kernel(ya_ref, yb_ref, yc_ref, yd_ref, g_ref, x_ref, wb_ref, wo_ref, gn_ref, h_ref, hn_ref):
    merged = None
    for n, y_ref in enumerate((ya_ref, yb_ref, yc_ref, yd_ref)):
        lifted = jnp.dot(y_ref[...], wb_ref[n], preferred_element_type=jnp.float32)
        term = _sigmoid(g_ref[:, n * D_MODEL:(n + 1) * D_MODEL]) * lifted
        merged = term if merged is None else merged + term
    h = x_ref[...] + jnp.dot(merged.astype(jnp.bfloat16), wo_ref[...], preferred_element_type=jnp.float32)
    h_ref[...] = h
    hn = h * lax.rsqrt(jnp.mean(h * h, axis=-1, keepdims=True) + EPS) * gn_ref[...]
    hn_ref[...] = hn.astype(hn_ref.dtype)


def _merge(ya, yb, yc, yd, p32, h, wb, wo, g_next, hn_dtype, tm=256):
    m, d = h.shape
    yblk = pl.BlockSpec((tm, W), lambda i: (i, 0))
    hblk = pl.BlockSpec((tm, d), lambda i: (i, 0))
    return pl.pallas_call(
        _merge_kernel,
        grid=(m // tm,),
        in_specs=[yblk, yblk, yblk, yblk,
                  pl.BlockSpec((tm, N_BRANCH * d), lambda i: (i, 0)),
                  hblk,
                  pl.BlockSpec((N_BRANCH, W, d), lambda i: (0, 0, 0)),
                  pl.BlockSpec((d, d), lambda i: (0, 0)),
                  pl.BlockSpec((1, d), lambda i: (0, 0))],
        out_specs=[hblk, hblk],
        out_shape=[jax.ShapeDtypeStruct((m, d), jnp.float32), jax.ShapeDtypeStruct((m, d), hn_dtype)],
        compiler_params=_cparams(("arbitrary",)),
        name="merge",
    )(ya, yb, yc, yd, p32, h, wb, wo, g_next.reshape(1, d))


def kernel(x, norm_g, w_in, gm_ln_g, gm_ln_b, gm_w_s, gm_b_s, dsa_kv_g, dsa_w_uk, dsa_w_uv,
           conv_w, fox_b_f, w_branch, w_out, final_g):
    b, s, d = x.shape
    depth = w_in.shape[0]
    m = b * s
    h = x.reshape(m, d)
    hn = _rmsnorm(h, norm_g[0], jnp.bfloat16)
    for l in range(depth):
        w16, w32, wvt = _prep_w_in(w_in[l])
        p16 = _matmul(hn, w16, jnp.bfloat16, tm=512, tn=N16, name="in_proj16")
        p32 = _matmul(hn, w32, jnp.float32, tm=512, tn=N32 // 5, name="in_proj32")
        vt = _matmul_t(hn, wvt, tm=min(KC, s), name="in_proj_vt")
        p16b = p16.reshape(b, s, N16)
        p32b = p32.reshape(b, s, N32)
        ya, yc = _mix_ac(p32b, gm_ln_g[l], gm_ln_b[l], gm_w_s[l], gm_b_s[l], conv_w[l])
        qaug, kaug = _fox_cum(p32b, fox_b_f[l])
        yd = _fox(p16b, p32b, qaug, kaug, vt)
        yb = _dsat(p16b, p32b, dsa_kv_g[l], dsa_w_uk[l], dsa_w_uv[l])
        last = l == depth - 1
        g_next = final_g if last else norm_g[l + 1]
        h, hn = _merge(ya.reshape(m, W), yb.reshape(m, W), yc.reshape(m, W), yd.reshape(m, W), p32, h,
                       w_branch[l].astype(jnp.bfloat16), w_out[l].astype(jnp.bfloat16), g_next,
                       jnp.float32 if last else jnp.bfloat16)
    return hn.reshape(b, s, d)
```

```python
import functools

import numpy as np
import jax
import jax.numpy as jnp
from jax import lax
from jax.experimental import pallas as pl
from jax.experimental.pallas import tpu as pltpu

D_MODEL = 1024
N_BRANCH = 4
W = 512
EPS = 1e-6
QB = 128
QBE = 256
KC = 512
FOLD = 64
CUM_ROWS = 512
GM_GROUPS = 4
GM_CHUNK = 128
HEADS = 4
HEAD_DIM = W // HEADS
LATENT = 128
IDX_HEADS = 4
IDX_DIM = 64
TOPK_MAX = 256
CONV_WIDTH = 3
LANE = 128
NEG_BIG = -1e30
LOG2E = 1.4426950408889634
VMEM_LIMIT = 56 * 1024 * 1024

P16_DQ, P16_DK, P16_BQ = 0, 1, 2
P16_QI = 3
P16_KI = 16
N16 = 17 * LANE
P32_AU, P32_AV, P32_AZ, P32_BZ = 0, 1, 2, 3
P32_CB, P32_CC, P32_CX, P32_CZ, P32_DZ = 4, 5, 6, 7, 8
P32_BC = 36
P32_SM = 37
N32 = 38 * LANE

_NT = (((1,), (1,)), ((), ()))


def _cparams(sem):
    return pltpu.CompilerParams(dimension_semantics=sem, vmem_limit_bytes=VMEM_LIMIT)


def _sigmoid(z):
    return 1.0 / (1.0 + jnp.exp(-z))


def _silu(z):
    return z * _sigmoid(z)


def _rmsnorm_kernel(x_ref, g_ref, o_ref):
    x = x_ref[...]
    y = x * lax.rsqrt(jnp.mean(x * x, axis=-1, keepdims=True) + EPS)
    o_ref[...] = (y * g_ref[...]).astype(o_ref.dtype)


def _rmsnorm(x2, g, out_dtype, tm=512):
    m, d = x2.shape
    return pl.pallas_call(
        _rmsnorm_kernel,
        grid=(m // tm,),
        in_specs=[pl.BlockSpec((tm, d), lambda i: (i, 0)), pl.BlockSpec((1, d), lambda i: (0, 0))],
        out_specs=pl.BlockSpec((tm, d), lambda i: (i, 0)),
        out_shape=jax.ShapeDtypeStruct((m, d), out_dtype),
        compiler_params=_cparams(("arbitrary",)),
        name="rmsnorm",
    )(x2, g.reshape(1, d))


def _matmul_kernel(a_ref, w_ref, o_ref):
    o_ref[...] = jnp.dot(a_ref[...], w_ref[...], preferred_element_type=jnp.float32).astype(o_ref.dtype)


def _matmul(a, w, out_dtype, tm, tn, name):
    m, k = a.shape
    n = w.shape[1]
    return pl.pallas_call(
        _matmul_kernel,
        grid=(n // tn, m // tm),
        in_specs=[pl.BlockSpec((tm, k), lambda j, i: (i, 0)), pl.BlockSpec((k, tn), lambda j, i: (0, j))],
        out_specs=pl.BlockSpec((tm, tn), lambda j, i: (i, j)),
        out_shape=jax.ShapeDtypeStruct((m, n), out_dtype),
        compiler_params=_cparams(("arbitrary", "arbitrary")),
        name=name,
    )(a, w)


def _prep_w_in(w):
    d = w.shape[0]
    sizes = (W, W, W,
             W, LATENT, IDX_HEADS * IDX_DIM, IDX_DIM, IDX_HEADS, W,
             W, W, W, W,
             W, W, W, HEADS, W,
             N_BRANCH * D_MODEL)
    parts, off = [], 0
    for s in sizes:
        parts.append(w[:, off:off + s])
        off += s
    (a_u, a_v, a_z, b_q, b_c, b_qi, b_ki, b_wi, b_z,
     c_b, c_c, c_x, c_z, d_q, d_k, d_v, d_f, d_z, gates) = parts
    zeros = lambda n: jnp.zeros((d, n), w.dtype)
    qi = []
    for h in range(IDX_HEADS):
        qi += [b_qi[:, h * IDX_DIM:(h + 1) * IDX_DIM], zeros(LANE - IDX_DIM)]
    w16 = jnp.concatenate([d_q, d_k, b_q] + qi + [b_ki, zeros(LANE - IDX_DIM)], axis=1)
    w32 = jnp.concatenate([a_u, a_v, a_z, b_z, c_b, c_c, c_x, c_z, d_z, b_c,
                           d_f, b_wi, zeros(LANE - HEADS - IDX_HEADS)], axis=1)
    bf = jnp.bfloat16
    return w16.astype(bf), w32.astype(bf), jnp.transpose(d_v).astype(bf), gates.astype(bf)


def _matmul_t_kernel(wt_ref, a_ref, o_ref):
    o_ref[...] = lax.dot_general(wt_ref[...], a_ref[...], _NT,
                                 preferred_element_type=jnp.float32).astype(o_ref.dtype)


def _matmul_t(a, wt, tm, name):
    m, k = a.shape
    n = wt.shape[0]
    return pl.pallas_call(
        _matmul_t_kernel,
        grid=(m // tm,),
        in_specs=[pl.BlockSpec((n, k), lambda i: (0, 0)), pl.BlockSpec((tm, k), lambda i: (i, 0))],
        out_specs=pl.BlockSpec((None, n, tm), lambda i: (i, 0, 0)),
        out_shape=jax.ShapeDtypeStruct((m // tm, n, tm), jnp.bfloat16),
        compiler_params=_cparams(("arbitrary",)),
        name=name,
    )(wt, a)


def _mix_ac_tile(au_ref, av_ref, az_ref, cb_ref, cc_ref, cx_ref, cz_ref,
                 lng_ref, lnb_ref, ws_ref, bs_ref, cw_ref, ya_ref, yc_ref, halo_ref, first_tile, tt):
    v = av_ref[...]
    mu = jnp.mean(v, axis=-1, keepdims=True)
    vc = v - mu
    var = jnp.mean(vc * vc, axis=-1, keepdims=True)
    vn = (vc * lax.rsqrt(var + EPS) * lng_ref[...] + lnb_ref[...]).astype(jnp.bfloat16)
    row = lax.broadcasted_iota(jnp.int32, (GM_CHUNK, GM_CHUNK), 0)
    col = lax.broadcasted_iota(jnp.int32, (GM_CHUNK, GM_CHUNK), 1)
    tril = row >= col
    wg = [jnp.where(tril, ws_ref[g], 0.0).astype(jnp.bfloat16) for g in range(GM_GROUPS)]
    for ch in range(tt // GM_CHUNK):
        rows = slice(ch * GM_CHUNK, (ch + 1) * GM_CHUNK)
        for g in range(GM_GROUPS):
            cols = slice(g * LANE, (g + 1) * LANE)
            mixed = jnp.dot(wg[g], vn[rows, cols], preferred_element_type=jnp.float32) + bs_ref[:, cols]
            ya_ref[rows, cols] = (au_ref[rows, cols] * mixed * _silu(az_ref[rows, cols])).astype(ya_ref.dtype)

    @pl.when(first_tile)
    def _():
        halo_ref[...] = jnp.zeros_like(halo_ref)

    y = cc_ref[...] * cx_ref[...]
    ext = jnp.concatenate([halo_ref[...], y], axis=0)
    conv = cw_ref[2:3, :] * y
    for j in range(CONV_WIDTH - 1):
        shift = CONV_WIDTH - 1 - j
        conv = conv + cw_ref[j:j + 1, :] * ext[8 - shift:8 - shift + tt, :]
    yc_ref[...] = (cb_ref[...] * conv * _silu(cz_ref[...])).astype(yc_ref.dtype)
    halo_ref[...] = y[tt - 8:, :]


def _split3(x):
    hi = x.astype(jnp.bfloat16)
    r1 = x - hi.astype(jnp.float32)
    mid = r1.astype(jnp.bfloat16)
    lo = (r1 - mid.astype(jnp.float32)).astype(jnp.bfloat16)
    return hi, mid, lo


def _aug_placement():
    pq = np.zeros((3 * LANE, W), np.float32)
    pk = np.zeros((3 * LANE, W), np.float32)
    cq = np.zeros((1, W), np.float32)
    ck = np.zeros((1, W), np.float32)
    for h in range(HEADS):
        for j in range(3):
            pq[j * LANE + h, h * HEAD_DIM + j] = 1.0
            cq[0, h * HEAD_DIM + 3 + j] = 1.0
            pk[j * LANE + h, h * HEAD_DIM + 3 + j] = -1.0
            ck[0, h * HEAD_DIM + j] = 1.0
    return (jnp.asarray(pq, jnp.bfloat16), jnp.asarray(pk, jnp.bfloat16), jnp.asarray(cq), jnp.asarray(ck))


def _fox_cum_kernel(f_ref, bias_ref, pq_ref, pk_ref, cq_ref, ck_ref, qa_ref, ka_ref, carry_ref, *, s):
    rows = min(CUM_ROWS, s)
    row = lax.broadcasted_iota(jnp.int32, (rows, rows), 0)
    col = lax.broadcasted_iota(jnp.int32, (rows, rows), 1)
    ones_tril = jnp.where(row >= col, 1.0, 0.0).astype(jnp.bfloat16)
    carry_ref[...] = jnp.zeros_like(carry_ref)

    def body(c, _):
        off = pl.multiple_of(c * rows, rows)
        x = f_ref[pl.ds(off, rows), :] + bias_ref[...]
        ls = jnp.minimum(x, 0.0) - jnp.log1p(jnp.exp(-jnp.abs(x)))
        hi, mid, lo = _split3(ls * LOG2E)
        dot = lambda p: jnp.dot(ones_tril, p, preferred_element_type=jnp.float32)
        cs = (dot(hi) + dot(mid)) + dot(lo) + carry_ref[0:1, :]
        carry_ref[0:1, :] = cs[rows - 1:rows, :]
        parts = jnp.concatenate(_split3(cs), axis=1)
        qa = jnp.dot(parts, pq_ref[...], preferred_element_type=jnp.float32) + cq_ref[...]
        ka = jnp.dot(parts, pk_ref[...], preferred_element_type=jnp.float32) + ck_ref[...]
        qa_ref[pl.ds(off, rows), :] = qa.astype(qa_ref.dtype)
        ka_ref[pl.ds(off, rows), :] = ka.astype(ka_ref.dtype)
        return 0

    lax.fori_loop(0, s // rows, body, 0)


def _fox_cum(p32b, b_f):
    b, s, _ = p32b.shape
    bias = jnp.zeros((1, LANE), jnp.float32).at[0, :HEADS].set(b_f)
    pq, pk, cq, ck = _aug_placement()
    const = lambda shp: pl.BlockSpec(shp, lambda bi: (0, 0))
    out = jax.ShapeDtypeStruct((b, s, W), jnp.bfloat16)
    return pl.pallas_call(
        functools.partial(_fox_cum_kernel, s=s),
        grid=(b,),
        in_specs=[pl.BlockSpec((None, s, LANE), lambda bi: (bi, 0, P32_SM)), const((1, LANE)),
                  const((3 * LANE, W)), const((3 * LANE, W)), const((1, W)), const((1, W))],
        out_specs=[pl.BlockSpec((None, s, W), lambda bi: (bi, 0, 0))] * 2,
        out_shape=[out, out],
        scratch_shapes=[pltpu.VMEM((8, LANE), jnp.float32)],
        compiler_params=_cparams(("arbitrary",)),
        name="fox_cum",
    )(p32b, bias, pq, pk, cq, ck)


def _num_chunks(i, kc):
    return ((i + 1) * QB + kc - 1) // kc


def _fox_kernel(q_ref, qa_ref, k_ref, ka_ref, vt_ref, z_ref, o_ref,
                qf_ref, sa_ref, sb_ref, m_ref, l_ref, acc_ref, *, kc):
    i = pl.program_id(1)
    ndiag = (i * QBE) // kc
    scale = HEAD_DIM ** -0.5 * LOG2E
    for h in range(HEADS):
        hs = slice(h * HEAD_DIM, (h + 1) * HEAD_DIM)
        qs = (q_ref[:, hs].astype(jnp.float32) * scale).astype(jnp.bfloat16)
        qf_ref[h] = jnp.concatenate([qs, qa_ref[:, hs]], axis=1)
    m_ref[...] = jnp.full_like(m_ref, NEG_BIG)
    l_ref[...] = jnp.zeros_like(l_ref)
    acc_ref[...] = jnp.zeros_like(acc_ref)

    def scores_into(buf_ref, c):
        off = pl.multiple_of(c * kc, kc)
        for h in range(HEADS):
            hs = slice(h * HEAD_DIM, (h + 1) * HEAD_DIM)
            kf = jnp.concatenate([k_ref[pl.ds(off, kc), hs], ka_ref[pl.ds(off, kc), hs]], axis=1)
            buf_ref[:, h * QBE:(h + 1) * QBE] = lax.dot_general(
                kf, qf_ref[h], _NT, preferred_element_type=jnp.float32)

    def softmax_pv(buf_ref, c, masked):
        if masked:
            causal = (c * kc + lax.broadcasted_iota(jnp.int32, (kc, QBE), 0) <=
                      i * QBE + lax.broadcasted_iota(jnp.int32, (kc, QBE), 1))
        for h in range(HEADS):
            hs = slice(h * HEAD_DIM, (h + 1) * HEAD_DIM)
            st = buf_ref[:, h * QBE:(h + 1) * QBE]
            if masked:
                st = jnp.where(causal, st, NEG_BIG)
            m_old = m_ref[h]
            m_new = jnp.maximum(m_old, jnp.max(st, axis=0, keepdims=True))
            alpha = jnp.exp2(m_old - m_new)
            p = jnp.exp2(st - m_new)
            l_ref[h] = alpha * l_ref[h] + jnp.sum(p, axis=0, keepdims=True)
            pv = jnp.dot(vt_ref[c, hs, :], p.astype(jnp.bfloat16), preferred_element_type=jnp.float32)
            acc_ref[h] = alpha * acc_ref[h] + pv
            m_ref[h] = m_new

    scores_into(sa_ref, 0)

    def pair_body(j, _):
        scores_into(sb_ref, 2 * j + 1)
        softmax_pv(sa_ref, 2 * j, False)
        scores_into(sa_ref, 2 * j + 2)
        softmax_pv(sb_ref, 2 * j + 1, False)
        return 0

    lax.fori_loop(0, ndiag // 2, pair_body, 0)

    @pl.when(ndiag % 2 == 1)
    def _():
        scores_into(sb_ref, ndiag)
        softmax_pv(sa_ref, ndiag - 1, False)
        softmax_pv(sb_ref, ndiag, True)

    @pl.when(ndiag % 2 == 0)
    def _():
        softmax_pv(sa_ref, ndiag, True)

    for h in range(HEADS):
        hs = slice(h * HEAD_DIM, (h + 1) * HEAD_DIM)
        o = jnp.transpose(acc_ref[h] / l_ref[h])
        o_ref[:, hs] = (o * _silu(z_ref[:, hs])).astype(o_ref.dtype)


def _fox(p16b, p32b, qaug, kaug, vt):
    b, s, _ = p16b.shape
    kc = vt.shape[-1]
    assert s % kc == 0 and kc % QBE == 0
    nck = s // kc
    vt = vt.reshape(b, nck, W, kc)
    row = lambda: pltpu.VMEM((HEADS, 1, QBE), jnp.float32)
    return pl.pallas_call(
        functools.partial(_fox_kernel, kc=kc),
        grid=(b, s // QBE),
        in_specs=[pl.BlockSpec((None, QBE, W), lambda bi, i: (bi, i, P16_DQ)),
                  pl.BlockSpec((None, QBE, W), lambda bi, i: (bi, i, 0)),
                  pl.BlockSpec((None, s, W), lambda bi, i: (bi, 0, P16_DK)),
                  pl.BlockSpec((None, s, W), lambda bi, i: (bi, 0, 0)),
                  pl.BlockSpec((None, nck, W, kc), lambda bi, i: (bi, 0, 0, 0)),
                  pl.BlockSpec((None, QBE, W), lambda bi, i: (bi, i, P32_DZ))],
        out_specs=pl.BlockSpec((None, QBE, W), lambda bi, i: (bi, i, 0)),
        out_shape=jax.ShapeDtypeStruct((b, s, W), jnp.bfloat16),
        scratch_shapes=[pltpu.VMEM((HEADS, QBE, 2 * HEAD_DIM), jnp.bfloat16),
                        pltpu.VMEM((kc, HEADS * QBE), jnp.float32),
                        pltpu.VMEM((kc, HEADS * QBE), jnp.float32),
                        row(), row(),
                        pltpu.VMEM((HEADS, HEAD_DIM, QBE), jnp.float32)],
        compiler_params=_cparams(("arbitrary", "arbitrary")),
        name="fox_attn",
    )(p16b, qaug, p16b, kaug, vt, p32b)


def _dsa_kernel(q_ref, qi_ref, sm_ref, z_ref, ki_ref, c_ref, kvg_ref, wuk_ref, wuv_ref, o_ref,
                cn_ref, sc_ref, thr_ref, jst_ref, lo_ref, hib_ref, hi_ref, clo_ref,
                m_ref, l_ref, acc_ref, *, s_len, kc, ksel, n_bisect):
    i = pl.program_id(1)
    nch = _num_chunks(i, kc)
    qpos = i * QB + lax.broadcasted_iota(jnp.int32, (QB, 1), 0)
    kf = float(ksel)
    inf = float("inf")

    @pl.when(i == 0)
    def _():
        def nbody(c, _):
            off = pl.multiple_of(c * kc, kc)
            x = c_ref[pl.ds(off, kc), :]
            y = x * lax.rsqrt(jnp.mean(x * x, axis=-1, keepdims=True) + EPS) * kvg_ref[...]
            cn_ref[pl.ds(off, kc), :] = y.astype(cn_ref.dtype)
            return 0
        lax.fori_loop(0, s_len // kc, nbody, 0)

    wi = sm_ref[:, HEADS:HEADS + IDX_HEADS] * (IDX_HEADS ** -0.5 * IDX_DIM ** -0.5)

    def sbody(c, _):
        off = pl.multiple_of(c * kc, kc)
        kk = ki_ref[pl.ds(off, kc), :]
        sc = jnp.zeros((QB, kc), jnp.float32)
        for h in range(IDX_HEADS):
            d = lax.dot_general(qi_ref[:, h * LANE:(h + 1) * LANE], kk, _NT, preferred_element_type=jnp.float32)
            sc = sc + wi[:, h:h + 1] * jnp.maximum(d, 0.0)
        kpos = off + lax.broadcasted_iota(jnp.int32, (1, kc), 1)
        sc_ref[c] = jnp.where(kpos <= qpos, sc, -inf)
        return 0

    lax.fori_loop(0, nch, sbody, 0)

    def fold_chunks(fn, init):
        def cbody(c, acc):
            off = pl.multiple_of(c * kc, kc)
            sc = sc_ref[c]
            for j in range(kc // LANE):
                kpos = off + j * LANE + lax.broadcasted_iota(jnp.int32, (1, LANE), 1)
                acc = fn(acc, sc[:, j * LANE:(j + 1) * LANE], kpos)
            return acc
        return lax.fori_loop(0, nch, cbody, init)

    def count(pred, t):
        tb = jnp.broadcast_to(t, (QB, LANE))
        acc = fold_chunks(lambda a, x, kp: a + jnp.where(pred(x, tb, kp), 1.0, 0.0),
                          jnp.zeros((QB, LANE), jnp.float32))
        return jnp.sum(acc, axis=-1, keepdims=True)

    def any_row(flag):
        return (jnp.max(jnp.where(flag, 1.0, 0.0)) > 0.5).astype(jnp.int32)

    thr_ref[...] = jnp.full_like(thr_ref, -inf)
    jst_ref[...] = jnp.full_like(jst_ref, -1)

    @pl.when(i * QB >= ksel)
    def _():
        mn0 = jnp.full((QB, LANE), inf, jnp.float32)
        mx0 = jnp.full((QB, LANE), -inf, jnp.float32)
        mn, mx = fold_chunks(
            lambda a, x, kp: (jnp.minimum(a[0], jnp.where(x == -inf, inf, x)), jnp.maximum(a[1], x)),
            (mn0, mx0))
        lo_ref[...] = jnp.min(mn, axis=-1, keepdims=True)
        hib_ref[...] = jnp.max(mx, axis=-1, keepdims=True)
        hi_ref[...] = jnp.full_like(hi_ref, inf)
        clo_ref[...] = (qpos + 1).astype(jnp.float32)

        def bis_cond(st):
            it, go = st
            return jnp.logical_and(it < n_bisect, go > 0)

        def bis_body(st):
            it, _ = st
            lo, hib = lo_ref[...], hib_ref[...]
            mid = 0.5 * lo + 0.5 * hib
            cnt = count(lambda x, tb, kp: x >= tb, mid)
            ge = cnt >= kf
            lo_ref[...] = jnp.where(ge, mid, lo)
            clo = jnp.where(ge, cnt, clo_ref[...])
            clo_ref[...] = clo
            hib_ref[...] = jnp.where(ge, hib, mid)
            hi_ref[...] = jnp.where(ge, hi_ref[...], mid)
            return it + 1, any_row(clo != kf)

        lax.while_loop(bis_cond, bis_body, (jnp.int32(0), jnp.int32(1)))

        resolved = clo_ref[...] == kf
        thr_ref[...] = jnp.where(resolved, lo_ref[...], inf)
        clo_ref[...] = jnp.where(resolved, kf, 0.0)

        def peel_body(go):
            done = clo_ref[...] >= kf
            hi = hi_ref[...]
            hb = jnp.broadcast_to(hi, (QB, LANE))
            nxt = fold_chunks(lambda a, x, kp: jnp.maximum(a, jnp.where(x < hb, x, -inf)),
                              jnp.full((QB, LANE), -inf, jnp.float32))
            t = jnp.where(done, thr_ref[...], jnp.max(nxt, axis=-1, keepdims=True))
            cnt = count(lambda x, tb, kp: x >= tb, t)
            thr_ref[...] = t
            clo_ref[...] = cnt
            hi_ref[...] = jnp.where(done, hi, t)
            return any_row(cnt < kf)

        lax.while_loop(lambda go: go > 0, peel_body, any_row(clo_ref[...] < kf))

        jst_ref[...] = jnp.full_like(jst_ref, s_len)

        @pl.when(any_row(clo_ref[...] > kf) > 0)
        def _():
            thr = thr_ref[...]
            need = kf - count(lambda x, tb, kp: x > tb, thr)
            excess = clo_ref[...] > kf

            def tie_body(bi, j):
                cand = j + lax.shift_left(jnp.int32(1), (s_len.bit_length() - 1) - bi)
                cb = jnp.broadcast_to(cand, (QB, LANE))
                tb = jnp.broadcast_to(thr, (QB, LANE))
                acc = fold_chunks(
                    lambda a, x, kp: a + jnp.where(jnp.logical_and(x == tb, kp < cb), 1.0, 0.0),
                    jnp.zeros((QB, LANE), jnp.float32))
                below = jnp.sum(acc, axis=-1, keepdims=True)
                return jnp.where(below < need, cand, j)

            j = lax.fori_loop(0, s_len.bit_length(), tie_body, jnp.zeros((QB, 1), jnp.int32))
            jst_ref[...] = jnp.where(excess, j, s_len)

    qlat = []
    for h in range(HEADS):
        hs = slice(h * HEAD_DIM, (h + 1) * HEAD_DIM)
        ql = lax.dot_general(q_ref[:, hs], wuk_ref[h], _NT, preferred_element_type=jnp.float32)
        qlat.append((ql * HEAD_DIM ** -0.5).astype(jnp.bfloat16))
    qlat = jnp.concatenate(qlat, axis=0)
    m_ref[...] = jnp.full_like(m_ref, NEG_BIG)
    l_ref[...] = jnp.zeros_like(l_ref)
    acc_ref[...] = jnp.zeros_like(acc_ref)
    thr = thr_ref[...]
    jst = jst_ref[...]

    def abody(c, _):
        off = pl.multiple_of(c * kc, kc)
        cn = cn_ref[pl.ds(off, kc), :]
        sc = sc_ref[c]
        kpos = off + lax.broadcasted_iota(jnp.int32, (1, kc), 1)
        sel = jnp.logical_or(sc > thr, jnp.logical_and(sc == thr, kpos <= jst))
        logits = lax.dot_general(qlat, cn, _NT, preferred_element_type=jnp.float32)
        for h in range(HEADS):
            rs = slice(h * QB, (h + 1) * QB)
            s = jnp.where(sel, logits[rs, :], NEG_BIG)
            m_old = m_ref[rs, :]
            m_new = jnp.maximum(m_old, jnp.max(s, axis=-1, keepdims=True))
            alpha = jnp.exp(m_old - m_new)
            p = jnp.exp(s - m_new)
            l_ref[rs, :] = alpha * l_ref[rs, :] + jnp.sum(p, axis=-1, keepdims=True)
            pv = jnp.dot(p.astype(jnp.bfloat16), cn, preferred_element_type=jnp.float32)
            acc_ref[rs, :] = alpha * acc_ref[rs, :] + pv
            m_ref[rs, :] = m_new
        return 0

    lax.fori_loop(0, nch, abody, 0)

    for h in range(HEADS):
        hs = slice(h * HEAD_DIM, (h + 1) * HEAD_DIM)
        rs = slice(h * QB, (h + 1) * QB)
        o_lat = (acc_ref[rs, :] / l_ref[rs, :]).astype(jnp.bfloat16)
        o = jnp.dot(o_lat, wuv_ref[h], preferred_element_type=jnp.float32)
        o_ref[:, hs] = (o * _silu(z_ref[:, hs])).astype(o_ref.dtype)


def _dsa(p16b, p32b, kv_g, w_uk, w_uv):
    b, s, _ = p16b.shape
    kc = min(512, s)
    ksel = min(TOPK_MAX, s // 4)
    assert ksel % QB == 0 and s % kc == 0
    col = lambda: pltpu.VMEM((QB, 1), jnp.float32)
    full3 = pl.BlockSpec((HEADS, LATENT, HEAD_DIM), lambda bi, i: (0, 0, 0))
    return pl.pallas_call(
        functools.partial(_dsa_kernel, s_len=s, kc=kc, ksel=ksel, n_bisect=24),
        grid=(b, s // QB),
        in_specs=[pl.BlockSpec((None, QB, W), lambda bi, i: (bi, i, P16_BQ)),
                  pl.BlockSpec((None, QB, W), lambda bi, i: (bi, i, P16_QI)),
                  pl.BlockSpec((None, QB, LANE), lambda bi, i: (bi, i, P32_SM)),
                  pl.BlockSpec((None, QB, W), lambda bi, i: (bi, i, P32_BZ)),
                  pl.BlockSpec((None, s, LANE), lambda bi, i: (bi, 0, P16_KI)),
                  pl.BlockSpec((None, s, LANE), lambda bi, i: (bi, 0, P32_BC)),
                  pl.BlockSpec((1, LATENT), lambda bi, i: (0, 0)),
                  full3, full3],
        out_specs=pl.BlockSpec((None, QB, W), lambda bi, i: (bi, i, 0)),
        out_shape=jax.ShapeDtypeStruct((b, s, W), jnp.bfloat16),
        scratch_shapes=[pltpu.VMEM((s, LATENT), jnp.bfloat16),
                        pltpu.VMEM((s // kc, QB, kc), jnp.float32),
                        col(),
                        pltpu.VMEM((QB, 1), jnp.int32),
                        col(), col(), col(), col(),
                        pltpu.VMEM((HEADS * QB, 1), jnp.float32),
                        pltpu.VMEM((HEADS * QB, 1), jnp.float32),
                        pltpu.VMEM((HEADS * QB, LATENT), jnp.float32)],
        compiler_params=_cparams(("arbitrary", "arbitrary")),
        name="dsa_attn",
    )(p16b, p16b, p32b, p32b, p16b, p32b, kv_g.reshape(1, LATENT),
      w_uk.astype(jnp.bfloat16), w_uv.astype(jnp.bfloat16))


def _dsat_kernel(q_ref, qi_ref, sm_ref, z_ref, ki_ref, c_ref, kvg_ref, wuk_ref, wuv_ref, o_ref,
                 cn_ref, cnt_ref, sc_ref, qia_ref, qlat_ref, lga_ref, lgb_ref, bias_ref, p_ref,
                 thr_ref, jst_ref, lo_ref, hib_ref, hi_ref, clo_ref,
                 m_ref, l_ref, acc_ref, *, s_len, kc, ksel, n_bisect):
    i = pl.program_id(1)
    ndiag = (i * QBE) // kc
    nch = ndiag + 1
    qpos = i * QBE + lax.broadcasted_iota(jnp.int32, (1, QBE), 1)
    kf = float(ksel)
    inf = float("inf")

    def key_pos(c):
        return c * kc + lax.broadcasted_iota(jnp.int32, (kc, QBE), 0)

    @pl.when(i == 0)
    def _():
        def nbody(c, _):
            off = pl.multiple_of(c * kc, kc)
            x = c_ref[pl.ds(off, kc), :]
            y = x * lax.rsqrt(jnp.mean(x * x, axis=-1, keepdims=True) + EPS) * kvg_ref[...]
            cn_ref[pl.ds(off, kc), :] = y.astype(cn_ref.dtype)
            cnt_ref[c] = jnp.transpose(y).astype(cnt_ref.dtype)
            return 0
        lax.fori_loop(0, s_len // kc, nbody, 0)

    wit = jnp.transpose(sm_ref[...])[HEADS:HEADS + IDX_HEADS, :] * (IDX_HEADS ** -0.5 * IDX_DIM ** -0.5)
    for h in range(IDX_HEADS):
        qia_ref[h * QBE:(h + 1) * QBE, :] = qi_ref[:, h * LANE:(h + 1) * LANE]

    def dots_into(buf_ref, c):
        off = pl.multiple_of(c * kc, kc)
        buf_ref[...] = lax.dot_general(ki_ref[pl.ds(off, kc), :], qia_ref[...], _NT,
                                       preferred_element_type=jnp.float32)

    def scores_from(buf_ref, c):
        sc = None
        for h in range(IDX_HEADS):
            term = wit[h:h + 1, :] * jnp.maximum(buf_ref[:, h * QBE:(h + 1) * QBE], 0.0)
            sc = term if sc is None else sc + term
        sc_ref[c] = sc

    def run_pairs(produce, consume):
        produce(lga_ref, 0)

        def pair_body(j, _):
            produce(lgb_ref, 2 * j + 1)
            consume(lga_ref, 2 * j)
            produce(lga_ref, jnp.minimum(2 * j + 2, ndiag))
            consume(lgb_ref, 2 * j + 1)
            return 0

        lax.fori_loop(0, nch // 2, pair_body, 0)

        @pl.when(nch % 2 == 1)
        def _():
            consume(lga_ref, ndiag)

    run_pairs(dots_into, scores_from)
    sc_ref[ndiag] = jnp.where(key_pos(ndiag) <= qpos, sc_ref[ndiag], -inf)

    def fold_slabs(op, fn, init_val):
        def cbody(c, acc):
            x = sc_ref[c]
            for j in range(kc // FOLD):
                acc = op(acc, fn(x[j * FOLD:(j + 1) * FOLD, :], c, j * FOLD))
            return acc
        return lax.fori_loop(0, nch, cbody, jnp.full((FOLD, QBE), init_val, jnp.float32))

    def count(pred, t):
        acc = fold_slabs(lambda a, m: jnp.where(m, a + 1.0, a), lambda x, c, r0: pred(x, t), 0.0)
        return jnp.sum(acc, axis=0, keepdims=True)

    def any_lane(flag):
        return (jnp.max(jnp.where(flag, 1.0, 0.0)) > 0.5).astype(jnp.int32)

    searched = qpos + 1 > ksel
    thr_ref[...] = jnp.full_like(thr_ref, -inf)
    jst_ref[...] = jnp.full_like(jst_ref, -1)

    @pl.when((i + 1) * QBE > ksel)
    def _():
        mn = fold_slabs(jnp.minimum, lambda x, c, r0: jnp.where(x == -inf, inf, x), inf)
        mx = fold_slabs(jnp.maximum, lambda x, c, r0: x, -inf)
        lo_ref[...] = jnp.min(mn, axis=0, keepdims=True)
        hib_ref[...] = jnp.max(mx, axis=0, keepdims=True)
        hi_ref[...] = jnp.full_like(hi_ref, inf)
        clo_ref[...] = (qpos + 1).astype(jnp.float32)

        def bis_body(_, carry):
            lo, hib = lo_ref[...], hib_ref[...]
            mid = 0.5 * lo + 0.5 * hib
            cnt = count(lambda x, t: x >= t, mid)
            ge = cnt >= kf
            lo_ref[...] = jnp.where(ge, mid, lo)
            clo_ref[...] = jnp.where(ge, cnt, clo_ref[...])
            hib_ref[...] = jnp.where(ge, hib, mid)
            hi_ref[...] = jnp.where(ge, hi_ref[...], mid)
            return carry

        lax.fori_loop(0, n_bisect, bis_body, 0)

        resolved = jnp.logical_or(clo_ref[...] == kf, jnp.logical_not(searched))
        thr_ref[...] = jnp.where(resolved, lo_ref[...], inf)
        clo_ref[...] = jnp.where(resolved, kf, 0.0)

        def peel_body(go):
            done = clo_ref[...] >= kf
            hi = hi_ref[...]
            nxt = fold_slabs(jnp.maximum, lambda x, c, r0: jnp.where(x < hi, x, -inf), -inf)
            t = jnp.where(done, thr_ref[...], jnp.max(nxt, axis=0, keepdims=True))
            cnt = count(lambda x, tt: x >= tt, t)
            thr_ref[...] = t
            clo_ref[...] = jnp.where(done, clo_ref[...], cnt)
            hi_ref[...] = jnp.where(done, hi, t)
            return any_lane(jnp.logical_and(jnp.logical_not(done), cnt < kf))

        lax.while_loop(lambda go: go > 0, peel_body, any_lane(clo_ref[...] < kf))

        jst_ref[...] = jnp.where(searched, s_len, -1)
        thr_ref[...] = jnp.where(searched, thr_ref[...], -inf)
        excess = jnp.logical_and(searched, clo_ref[...] > kf)

        @pl.when(any_lane(excess) > 0)
        def _():
            thr = thr_ref[...]
            need = kf - count(lambda x, t: x > t, thr)
            tril = jnp.where(lax.broadcasted_iota(jnp.int32, (kc, kc), 0) >=
                             lax.broadcasted_iota(jnp.int32, (kc, kc), 1), 1.0, 0.0).astype(jnp.bfloat16)

            def tie_body(c, carry):
                seen, last = carry
                tie = sc_ref[c] == thr
                rank = jnp.dot(tril, jnp.where(tie, 1.0, 0.0).astype(jnp.bfloat16),
                               preferred_element_type=jnp.float32) + seen
                keep = jnp.logical_and(tie, rank <= need)
                kept_pos = jnp.where(keep, key_pos(c).astype(jnp.float32), -1.0)
                return rank[kc - 1:kc, :], jnp.maximum(last, jnp.max(kept_pos, axis=0, keepdims=True))

            _, last = lax.fori_loop(0, nch, tie_body, (jnp.zeros((1, QBE), jnp.float32),
                                                       jnp.full((1, QBE), -1.0, jnp.float32)))
            jst_ref[...] = jnp.where(excess, last.astype(jnp.int32), jst_ref[...])

    for h in range(HEADS):
        hs = slice(h * HEAD_DIM, (h + 1) * HEAD_DIM)
        ql = lax.dot_general(q_ref[:, hs], wuk_ref[h], _NT, preferred_element_type=jnp.float32)
        qlat_ref[h * QBE:(h + 1) * QBE, :] = (ql * (HEAD_DIM ** -0.5 * LOG2E)).astype(qlat_ref.dtype)
    m_ref[...] = jnp.full_like(m_ref, NEG_BIG)
    l_ref[...] = jnp.zeros_like(l_ref)
    acc_ref[...] = jnp.zeros_like(acc_ref)
    thr = thr_ref[...]
    jst = jst_ref[...]

    def logits_into(buf_ref, c):
        off = pl.multiple_of(c * kc, kc)
        buf_ref[...] = lax.dot_general(cn_ref[pl.ds(off, kc), :], qlat_ref[...], _NT,
                                       preferred_element_type=jnp.float32)

    def softmax_pv(buf_ref, c):
        sc = sc_ref[c]
        sel = jnp.logical_or(sc > thr, jnp.logical_and(sc == thr, key_pos(c) <= jst))
        bias_ref[...] = jnp.where(sel, 0.0, NEG_BIG)
        m_old = m_ref[...]
        m_new = jnp.maximum(m_old, jnp.concatenate(
            [jnp.max(buf_ref[:, h * QBE:(h + 1) * QBE] + bias_ref[...], axis=0, keepdims=True)
             for h in range(HEADS)], axis=1))
        alpha = jnp.exp2(m_old - m_new)
        psum = []
        for h in range(HEADS):
            qs = slice(h * QBE, (h + 1) * QBE)
            p = jnp.exp2(buf_ref[:, qs] + bias_ref[...] - m_new[:, qs])
            psum.append(jnp.sum(p, axis=0, keepdims=True))
            p_ref[:, qs] = p.astype(p_ref.dtype)
        l_ref[...] = alpha * l_ref[...] + jnp.concatenate(psum, axis=1)
        pv = jnp.dot(cnt_ref[c], p_ref[...], preferred_element_type=jnp.float32)
        acc_ref[...] = alpha * acc_ref[...] + pv
        m_ref[...] = m_new

    run_pairs(logits_into, softmax_pv)

    for h in range(HEADS):
        hs = slice(h * HEAD_DIM, (h + 1) * HEAD_DIM)
        qs = slice(h * QBE, (h + 1) * QBE)
        o_lat = jnp.transpose(acc_ref[:, qs] / l_ref[:, qs]).astype(jnp.bfloat16)
        o = jnp.dot(o_lat, wuv_ref[h], preferred_element_type=jnp.float32)
        o_ref[:, hs] = (o * _silu(z_ref[:, hs])).astype(o_ref.dtype)


def _dsat(p16b, p32b, kv_g, w_uk, w_uv):
    b, s, _ = p16b.shape
    kc = min(KC, s)
    ksel = min(TOPK_MAX, s // 4)
    assert s % kc == 0 and kc % QBE == 0
    nq = HEADS * QBE
    row = lambda n: pltpu.VMEM((1, n), jnp.float32)
    full3 = pl.BlockSpec((HEADS, LATENT, HEAD_DIM), lambda bi, i: (0, 0, 0))
    return pl.pallas_call(
        functools.partial(_dsat_kernel, s_len=s, kc=kc, ksel=ksel, n_bisect=16),
        grid=(b, s // QBE),
        in_specs=[pl.BlockSpec((None, QBE, W), lambda bi, i: (bi, i, P16_BQ)),
                  pl.BlockSpec((None, QBE, W), lambda bi, i: (bi, i, P16_QI)),
                  pl.BlockSpec((None, QBE, LANE), lambda bi, i: (bi, i, P32_SM)),
                  pl.BlockSpec((None, QBE, W), lambda bi, i: (bi, i, P32_BZ)),
                  pl.BlockSpec((None, s, LANE), lambda bi, i: (bi, 0, P16_KI)),
                  pl.BlockSpec((None, s, LANE), lambda bi, i: (bi, 0, P32_BC)),
                  pl.BlockSpec((1, LATENT), lambda bi, i: (0, 0)),
                  full3, full3],
        out_specs=pl.BlockSpec((None, QBE, W), lambda bi, i: (bi, i, 0)),
        out_shape=jax.ShapeDtypeStruct((b, s, W), jnp.bfloat16),
        scratch_shapes=[pltpu.VMEM((s, LATENT), jnp.bfloat16),
                        pltpu.VMEM((s // kc, LATENT, kc), jnp.bfloat16),
                        pltpu.VMEM((s // kc, kc, QBE), jnp.float32),
                        pltpu.VMEM((IDX_HEADS * QBE, LANE), jnp.bfloat16),
                        pltpu.VMEM((nq, LATENT), jnp.bfloat16),
                        pltpu.VMEM((kc, nq), jnp.float32),
                        pltpu.VMEM((kc, nq), jnp.float32),
                        pltpu.VMEM((kc, QBE), jnp.float32),
                        pltpu.VMEM((kc, nq), jnp.bfloat16),
                        row(QBE),
                        pltpu.VMEM((1, QBE), jnp.int32),
                        row(QBE), row(QBE), row(QBE), row(QBE),
                        row(nq), row(nq),
                        pltpu.VMEM((LATENT, nq), jnp.float32)],
        compiler_params=_cparams(("arbitrary", "arbitrary")),
        name="dsa_attn",
    )(p16b, p16b, p32b, p32b, p16b, p32b, kv_g.reshape(1, LATENT),
      w_uk.astype(jnp.bfloat16), w_uv.astype(jnp.bfloat16))


def _merge_kernel(au_ref, av_ref, az_ref, cb_ref, cc_ref, cx_ref, cz_ref,
                  lng_ref, lnb_ref, ws_ref, bs_ref, cw_ref,
                  yb_ref, yd_ref, hn_in_ref, x_ref, wg_ref, wb_ref, wo_ref, gn_ref,
                  h_ref, hn_ref, ya_ref, yc_ref, halo_ref, *, tm, tiles_per_seq):
    first_tile = pl.program_id(0) % tiles_per_seq == 0
    _mix_ac_tile(au_ref, av_ref, az_ref, cb_ref, cc_ref, cx_ref, cz_ref,
                 lng_ref, lnb_ref, ws_ref, bs_ref, cw_ref, ya_ref, yc_ref, halo_ref, first_tile, tm)
    hn_in = hn_in_ref[...]
    merged = None
    for n, y_ref in enumerate((ya_ref, yb_ref, yc_ref, yd_ref)):
        cols = slice(n * D_MODEL, (n + 1) * D_MODEL)
        gate = _sigmoid(jnp.dot(hn_in, wg_ref[:, cols], preferred_element_type=jnp.float32))
        term = gate * jnp.dot(y_ref[...], wb_ref[n], preferred_element_type=jnp.float32)
        merged = term if merged is None else merged + term
    h = x_ref[...] + jnp.dot(merged.astype(jnp.bfloat16), wo_ref[...], preferred_element_type=jnp.float32)
    h_ref[...] = h
    hn = h * lax.rsqrt(jnp.mean(h * h, axis=-1, keepdims=True) + EPS) * gn_ref[...]
    hn_ref[...] = hn.astype(hn_ref.dtype)


def _merge(p32, yb, yd, hn, h, ln_g, ln_b, w_s, b_s, conv_w, wg, wb, wo, g_next, hn_dtype, s_len, tm=256):
    m, d = h.shape
    assert s_len % tm == 0 and tm % GM_CHUNK == 0
    pblk = lambda idx: pl.BlockSpec((tm, W), lambda i, idx=idx: (i, idx))
    yblk = pl.BlockSpec((tm, W), lambda i: (i, 0))
    hblk = pl.BlockSpec((tm, d), lambda i: (i, 0))
    const = lambda shp: pl.BlockSpec(shp, lambda i: (0,) * len(shp))
    bs_full = jnp.repeat(jnp.transpose(b_s), LANE, axis=1)
    return pl.pallas_call(
        functools.partial(_merge_kernel, tm=tm, tiles_per_seq=s_len // tm),
        grid=(m // tm,),
        in_specs=[pblk(P32_AU), pblk(P32_AV), pblk(P32_AZ), pblk(P32_CB), pblk(P32_CC), pblk(P32_CX), pblk(P32_CZ),
                  const((1, W)), const((1, W)), const((GM_GROUPS, GM_CHUNK, GM_CHUNK)),
                  const((GM_CHUNK, W)), const((CONV_WIDTH, W)),
                  yblk, yblk, hblk, hblk,
                  const((d, N_BRANCH * d)), const((N_BRANCH, W, d)), const((d, d)), const((1, d))],
        out_specs=[hblk, hblk],
        out_shape=[jax.ShapeDtypeStruct((m, d), jnp.float32), jax.ShapeDtypeStruct((m, d), hn_dtype)],
        scratch_shapes=[pltpu.VMEM((tm, W), jnp.bfloat16), pltpu.VMEM((tm, W), jnp.bfloat16),
                        pltpu.VMEM((8, W), jnp.float32)],
        compiler_params=_cparams(("arbitrary",)),
        name="merge",
    )(p32, p32, p32, p32, p32, p32, p32,
      ln_g.reshape(1, W), ln_b.reshape(1, W), w_s, bs_full, conv_w,
      yb, yd, hn, h, wg, wb, wo, g_next.reshape(1, d))


def kernel(x, norm_g, w_in, gm_ln_g, gm_ln_b, gm_w_s, gm_b_s, dsa_kv_g, dsa_w_uk, dsa_w_uv,
           conv_w, fox_b_f, w_branch, w_out, final_g):
    b, s, d = x.shape
    depth = w_in.shape[0]
    m = b * s
    h = x.reshape(m, d)
    hn = _rmsnorm(h, norm_g[0], jnp.bfloat16)
    for l in range(depth):
        w16, w32, wvt, wg = _prep_w_in(w_in[l])
        p16 = _matmul(hn, w16, jnp.bfloat16, tm=512, tn=N16, name="in_proj16")
        p32 = _matmul(hn, w32, jnp.float32, tm=512, tn=N32 // 2, name="in_proj32")
        vt = _matmul_t(hn, wvt, tm=min(KC, s), name="in_proj_vt")
        p16b = p16.reshape(b, s, N16)
        p32b = p32.reshape(b, s, N32)
        qaug, kaug = _fox_cum(p32b, fox_b_f[l])
        yd = _fox(p16b, p32b, qaug, kaug, vt)
        yb = _dsat(p16b, p32b, dsa_kv_g[l], dsa_w_uk[l], dsa_w_uv[l])
        last = l == depth - 1
        g_next = final_g if last else norm_g[l + 1]
        h, hn = _merge(p32, yb.reshape(m, W), yd.reshape(m, W), hn, h,
                       gm_ln_g[l], gm_ln_b[l], gm_w_s[l], gm_b_s[l], conv_w[l],
                       wg, w_branch[l].astype(jnp.bfloat16), w_out[l].astype(jnp.bfloat16), g_next,
                       jnp.float32 if last else jnp.bfloat16, s)
    return hn.reshape(b, s, d)
```

```python
import functools

import numpy as np
import jax
import jax.numpy as jnp
from jax import lax
from jax.experimental import pallas as pl
from jax.experimental.pallas import tpu as pltpu

D_MODEL = 1024
N_BRANCH = 4
W = 512
EPS = 1e-6
QB = 128
QBE = 256
KC = 512
FOLD = 64
CUM_ROWS = 512
GM_GROUPS = 4
GM_CHUNK = 128
HEADS = 4
HEAD_DIM = W // HEADS
LATENT = 128
IDX_HEADS = 4
IDX_DIM = 64
TOPK_MAX = 256
CONV_WIDTH = 3
LANE = 128
NEG_BIG = -1e30
LOG2E = 1.4426950408889634
VMEM_LIMIT = 56 * 1024 * 1024

P16_DQ, P16_DK, P16_BQ = 0, 1, 2
P16_QI = 3
P16_KI = 16
N16 = 17 * LANE
P32_AU, P32_AV, P32_AZ, P32_BZ = 0, 1, 2, 3
P32_CB, P32_CC, P32_CX, P32_CZ, P32_DZ = 4, 5, 6, 7, 8
P32_BC = 36
P32_SM = 37
N32 = 38 * LANE

_NT = (((1,), (1,)), ((), ()))


def _cparams(sem):
    return pltpu.CompilerParams(dimension_semantics=sem, vmem_limit_bytes=VMEM_LIMIT)


def _sigmoid(z):
    return 1.0 / (1.0 + jnp.exp(-z))


def _silu(z):
    return z * _sigmoid(z)


def _rmsnorm_kernel(x_ref, g_ref, o_ref):
    x = x_ref[...]
    y = x * lax.rsqrt(jnp.mean(x * x, axis=-1, keepdims=True) + EPS)
    o_ref[...] = (y * g_ref[...]).astype(o_ref.dtype)


def _rmsnorm(x2, g, out_dtype, tm=512):
    m, d = x2.shape
    return pl.pallas_call(
        _rmsnorm_kernel,
        grid=(m // tm,),
        in_specs=[pl.BlockSpec((tm, d), lambda i: (i, 0)), pl.BlockSpec((1, d), lambda i: (0, 0))],
        out_specs=pl.BlockSpec((tm, d), lambda i: (i, 0)),
        out_shape=jax.ShapeDtypeStruct((m, d), out_dtype),
        compiler_params=_cparams(("arbitrary",)),
        name="rmsnorm",
    )(x2, g.reshape(1, d))


def _matmul_kernel(a_ref, w_ref, o_ref):
    o_ref[...] = jnp.dot(a_ref[...], w_ref[...], preferred_element_type=jnp.float32).astype(o_ref.dtype)


def _matmul(a, w, out_dtype, tm, tn, name):
    m, k = a.shape
    n = w.shape[1]
    return pl.pallas_call(
        _matmul_kernel,
        grid=(n // tn, m // tm),
        in_specs=[pl.BlockSpec((tm, k), lambda j, i: (i, 0)), pl.BlockSpec((k, tn), lambda j, i: (0, j))],
        out_specs=pl.BlockSpec((tm, tn), lambda j, i: (i, j)),
        out_shape=jax.ShapeDtypeStruct((m, n), out_dtype),
        compiler_params=_cparams(("arbitrary", "arbitrary")),
        name=name,
    )(a, w)


def _prep_w_in(w):
    d = w.shape[0]
    sizes = (W, W, W,
             W, LATENT, IDX_HEADS * IDX_DIM, IDX_DIM, IDX_HEADS, W,
             W, W, W, W,
             W, W, W, HEADS, W,
             N_BRANCH * D_MODEL)
    parts, off = [], 0
    for s in sizes:
        parts.append(w[:, off:off + s])
        off += s
    (a_u, a_v, a_z, b_q, b_c, b_qi, b_ki, b_wi, b_z,
     c_b, c_c, c_x, c_z, d_q, d_k, d_v, d_f, d_z, gates) = parts
    zeros = lambda n: jnp.zeros((d, n), w.dtype)
    qi = []
    for h in range(IDX_HEADS):
        qi += [b_qi[:, h * IDX_DIM:(h + 1) * IDX_DIM], zeros(LANE - IDX_DIM)]
    w16 = jnp.concatenate([d_q, d_k, b_q] + qi + [b_ki, zeros(LANE - IDX_DIM)], axis=1)
    w32 = jnp.concatenate([a_u, a_v, a_z, b_z, c_b, c_c, c_x, c_z, d_z, b_c,
                           d_f, b_wi, zeros(LANE - HEADS - IDX_HEADS)], axis=1)
    bf = jnp.bfloat16
    return w16.astype(bf), w32.astype(bf), jnp.transpose(d_v).astype(bf), gates.astype(bf)


def _matmul_t_kernel(wt_ref, a_ref, o_ref):
    o_ref[...] = lax.dot_general(wt_ref[...], a_ref[...], _NT,
                                 preferred_element_type=jnp.float32).astype(o_ref.dtype)


def _matmul_t(a, wt, tm, name):
    m, k = a.shape
    n = wt.shape[0]
    return pl.pallas_call(
        _matmul_t_kernel,
        grid=(m // tm,),
        in_specs=[pl.BlockSpec((n, k), lambda i: (0, 0)), pl.BlockSpec((tm, k), lambda i: (i, 0))],
        out_specs=pl.BlockSpec((None, n, tm), lambda i: (i, 0, 0)),
        out_shape=jax.ShapeDtypeStruct((m // tm, n, tm), jnp.bfloat16),
        compiler_params=_cparams(("arbitrary",)),
        name=name,
    )(wt, a)


def _mix_ac_tile(au_ref, av_ref, az_ref, cb_ref, cc_ref, cx_ref, cz_ref,
                 lng_ref, lnb_ref, ws_ref, bs_ref, cw_ref, ya_ref, yc_ref, halo_ref, first_tile, tt):
    v = av_ref[...]
    mu = jnp.mean(v, axis=-1, keepdims=True)
    vc = v - mu
    var = jnp.mean(vc * vc, axis=-1, keepdims=True)
    vn = (vc * lax.rsqrt(var + EPS) * lng_ref[...] + lnb_ref[...]).astype(jnp.bfloat16)
    row = lax.broadcasted_iota(jnp.int32, (GM_CHUNK, GM_CHUNK), 0)
    col = lax.broadcasted_iota(jnp.int32, (GM_CHUNK, GM_CHUNK), 1)
    tril = row >= col
    wg = [jnp.where(tril, ws_ref[g], 0.0).astype(jnp.bfloat16) for g in range(GM_GROUPS)]
    for ch in range(tt // GM_CHUNK):
        rows = slice(ch * GM_CHUNK, (ch + 1) * GM_CHUNK)
        for g in range(GM_GROUPS):
            cols = slice(g * LANE, (g + 1) * LANE)
            mixed = jnp.dot(wg[g], vn[rows, cols], preferred_element_type=jnp.float32) + bs_ref[:, cols]
            ya_ref[rows, cols] = (au_ref[rows, cols] * mixed * _silu(az_ref[rows, cols])).astype(ya_ref.dtype)

    @pl.when(first_tile)
    def _():
        halo_ref[...] = jnp.zeros_like(halo_ref)

    y = cc_ref[...] * cx_ref[...]
    ext = jnp.concatenate([halo_ref[...], y], axis=0)
    conv = cw_ref[2:3, :] * y
    for j in range(CONV_WIDTH - 1):
        shift = CONV_WIDTH - 1 - j
        conv = conv + cw_ref[j:j + 1, :] * ext[8 - shift:8 - shift + tt, :]
    yc_ref[...] = (cb_ref[...] * conv * _silu(cz_ref[...])).astype(yc_ref.dtype)
    halo_ref[...] = y[tt - 8:, :]


def _split3(x):
    hi = x.astype(jnp.bfloat16)
    r1 = x - hi.astype(jnp.float32)
    mid = r1.astype(jnp.bfloat16)
    lo = (r1 - mid.astype(jnp.float32)).astype(jnp.bfloat16)
    return hi, mid, lo


def _aug_placement():
    pq = np.zeros((3 * LANE, W), np.float32)
    pk = np.zeros((3 * LANE, W), np.float32)
    cq = np.zeros((1, W), np.float32)
    ck = np.zeros((1, W), np.float32)
    for h in range(HEADS):
        for j in range(3):
            pq[j * LANE + h, h * HEAD_DIM + j] = 1.0
            cq[0, h * HEAD_DIM + 3 + j] = 1.0
            pk[j * LANE + h, h * HEAD_DIM + 3 + j] = -1.0
            ck[0, h * HEAD_DIM + j] = 1.0
    return (jnp.asarray(pq, jnp.bfloat16), jnp.asarray(pk, jnp.bfloat16), jnp.asarray(cq), jnp.asarray(ck))


def _fox_cum_kernel(f_ref, bias_ref, pq_ref, pk_ref, cq_ref, ck_ref, qa_ref, ka_ref, carry_ref, *, s):
    rows = min(CUM_ROWS, s)
    row = lax.broadcasted_iota(jnp.int32, (rows, rows), 0)
    col = lax.broadcasted_iota(jnp.int32, (rows, rows), 1)
    ones_tril = jnp.where(row >= col, 1.0, 0.0).astype(jnp.bfloat16)
    carry_ref[...] = jnp.zeros_like(carry_ref)

    def body(c, _):
        off = pl.multiple_of(c * rows, rows)
        x = f_ref[pl.ds(off, rows), :] + bias_ref[...]
        ls = jnp.minimum(x, 0.0) - jnp.log1p(jnp.exp(-jnp.abs(x)))
        hi, mid, lo = _split3(ls * LOG2E)
        dot = lambda p: jnp.dot(ones_tril, p, preferred_element_type=jnp.float32)
        cs = (dot(hi) + dot(mid)) + dot(lo) + carry_ref[0:1, :]
        carry_ref[0:1, :] = cs[rows - 1:rows, :]
        parts = jnp.concatenate(_split3(cs), axis=1)
        qa = jnp.dot(parts, pq_ref[...], preferred_element_type=jnp.float32) + cq_ref[...]
        ka = jnp.dot(parts, pk_ref[...], preferred_element_type=jnp.float32) + ck_ref[...]
        qa_ref[pl.ds(off, rows), :] = qa.astype(qa_ref.dtype)
        ka_ref[pl.ds(off, rows), :] = ka.astype(ka_ref.dtype)
        return 0

    lax.fori_loop(0, s // rows, body, 0)


def _fox_cum(p32b, b_f):
    b, s, _ = p32b.shape
    bias = jnp.zeros((1, LANE), jnp.float32).at[0, :HEADS].set(b_f)
    pq, pk, cq, ck = _aug_placement()
    const = lambda shp: pl.BlockSpec(shp, lambda bi: (0, 0))
    out = jax.ShapeDtypeStruct((b, s, W), jnp.bfloat16)
    return pl.pallas_call(
        functools.partial(_fox_cum_kernel, s=s),
        grid=(b,),
        in_specs=[pl.BlockSpec((None, s, LANE), lambda bi: (bi, 0, P32_SM)), const((1, LANE)),
                  const((3 * LANE, W)), const((3 * LANE, W)), const((1, W)), const((1, W))],
        out_specs=[pl.BlockSpec((None, s, W), lambda bi: (bi, 0, 0))] * 2,
        out_shape=[out, out],
        scratch_shapes=[pltpu.VMEM((8, LANE), jnp.float32)],
        compiler_params=_cparams(("arbitrary",)),
        name="fox_cum",
    )(p32b, bias, pq, pk, cq, ck)


def _num_chunks(i, kc):
    return ((i + 1) * QB + kc - 1) // kc


def _fox_kernel(q_ref, qa_ref, k_ref, ka_ref, vt_ref, z_ref, o_ref,
                qf_ref, sa_ref, sb_ref, m_ref, l_ref, acc_ref, *, kc):
    i = pl.program_id(1)
    ndiag = (i * QBE) // kc
    scale = HEAD_DIM ** -0.5 * LOG2E
    for h in range(HEADS):
        hs = slice(h * HEAD_DIM, (h + 1) * HEAD_DIM)
        qs = (q_ref[:, hs].astype(jnp.float32) * scale).astype(jnp.bfloat16)
        qf_ref[h] = jnp.concatenate([qs, qa_ref[:, hs]], axis=1)
    m_ref[...] = jnp.full_like(m_ref, NEG_BIG)
    l_ref[...] = jnp.zeros_like(l_ref)
    acc_ref[...] = jnp.zeros_like(acc_ref)

    def scores_into(buf_ref, c):
        off = pl.multiple_of(c * kc, kc)
        for h in range(HEADS):
            hs = slice(h * HEAD_DIM, (h + 1) * HEAD_DIM)
            kf = jnp.concatenate([k_ref[pl.ds(off, kc), hs], ka_ref[pl.ds(off, kc), hs]], axis=1)
            buf_ref[:, h * QBE:(h + 1) * QBE] = lax.dot_general(
                kf, qf_ref[h], _NT, preferred_element_type=jnp.float32)

    def softmax_pv(buf_ref, c, masked):
        if masked:
            causal = (c * kc + lax.broadcasted_iota(jnp.int32, (kc, QBE), 0) <=
                      i * QBE + lax.broadcasted_iota(jnp.int32, (kc, QBE), 1))
        for h in range(HEADS):
            hs = slice(h * HEAD_DIM, (h + 1) * HEAD_DIM)
            st = buf_ref[:, h * QBE:(h + 1) * QBE]
            if masked:
                st = jnp.where(causal, st, NEG_BIG)
            m_old = m_ref[h]
            m_new = jnp.maximum(m_old, jnp.max(st, axis=0, keepdims=True))
            alpha = jnp.exp2(m_old - m_new)
            p = jnp.exp2(st - m_new)
            l_ref[h] = alpha * l_ref[h] + jnp.sum(p, axis=0, keepdims=True)
            pv = jnp.dot(vt_ref[c, hs, :], p.astype(jnp.bfloat16), preferred_element_type=jnp.float32)
            acc_ref[h] = alpha * acc_ref[h] + pv
            m_ref[h] = m_new

    scores_into(sa_ref, 0)

    def pair_body(j, _):
        scores_into(sb_ref, 2 * j + 1)
        softmax_pv(sa_ref, 2 * j, False)
        scores_into(sa_ref, 2 * j + 2)
        softmax_pv(sb_ref, 2 * j + 1, False)
        return 0

    lax.fori_loop(0, ndiag // 2, pair_body, 0)

    @pl.when(ndiag % 2 == 1)
    def _():
        scores_into(sb_ref, ndiag)
        softmax_pv(sa_ref, ndiag - 1, False)
        softmax_pv(sb_ref, ndiag, True)

    @pl.when(ndiag % 2 == 0)
    def _():
        softmax_pv(sa_ref, ndiag, True)

    for h in range(HEADS):
        hs = slice(h * HEAD_DIM, (h + 1) * HEAD_DIM)
        o = jnp.transpose(acc_ref[h] / l_ref[h])
        o_ref[:, hs] = (o * _silu(z_ref[:, hs])).astype(o_ref.dtype)


def _fox(p16b, p32b, qaug, kaug, vt):
    b, s, _ = p16b.shape
    kc = vt.shape[-1]
    assert s % kc == 0 and kc % QBE == 0
    nck = s // kc
    vt = vt.reshape(b, nck, W, kc)
    row = lambda: pltpu.VMEM((HEADS, 1, QBE), jnp.float32)
    return pl.pallas_call(
        functools.partial(_fox_kernel, kc=kc),
        grid=(b, s // QBE),
        in_specs=[pl.BlockSpec((None, QBE, W), lambda bi, i: (bi, i, P16_DQ)),
                  pl.BlockSpec((None, QBE, W), lambda bi, i: (bi, i, 0)),
                  pl.BlockSpec((None, s, W), lambda bi, i: (bi, 0, P16_DK)),
                  pl.BlockSpec((None, s, W), lambda bi, i: (bi, 0, 0)),
                  pl.BlockSpec((None, nck, W, kc), lambda bi, i: (bi, 0, 0, 0)),
                  pl.BlockSpec((None, QBE, W), lambda bi, i: (bi, i, P32_DZ))],
        out_specs=pl.BlockSpec((None, QBE, W), lambda bi, i: (bi, i, 0)),
        out_shape=jax.ShapeDtypeStruct((b, s, W), jnp.bfloat16),
        scratch_shapes=[pltpu.VMEM((HEADS, QBE, 2 * HEAD_DIM), jnp.bfloat16),
                        pltpu.VMEM((kc, HEADS * QBE), jnp.float32),
                        pltpu.VMEM((kc, HEADS * QBE), jnp.float32),
                        row(), row(),
                        pltpu.VMEM((HEADS, HEAD_DIM, QBE), jnp.float32)],
        compiler_params=_cparams(("arbitrary", "arbitrary")),
        name="fox_attn",
    )(p16b, qaug, p16b, kaug, vt, p32b)


def _dsa_kernel(q_ref, qi_ref, sm_ref, z_ref, ki_ref, c_ref, kvg_ref, wuk_ref, wuv_ref, o_ref,
                cn_ref, sc_ref, thr_ref, jst_ref, lo_ref, hib_ref, hi_ref, clo_ref,
                m_ref, l_ref, acc_ref, *, s_len, kc, ksel, n_bisect):
    i = pl.program_id(1)
    nch = _num_chunks(i, kc)
    qpos = i * QB + lax.broadcasted_iota(jnp.int32, (QB, 1), 0)
    kf = float(ksel)
    inf = float("inf")

    @pl.when(i == 0)
    def _():
        def nbody(c, _):
            off = pl.multiple_of(c * kc, kc)
            x = c_ref[pl.ds(off, kc), :]
            y = x * lax.rsqrt(jnp.mean(x * x, axis=-1, keepdims=True) + EPS) * kvg_ref[...]
            cn_ref[pl.ds(off, kc), :] = y.astype(cn_ref.dtype)
            return 0
        lax.fori_loop(0, s_len // kc, nbody, 0)

    wi = sm_ref[:, HEADS:HEADS + IDX_HEADS] * (IDX_HEADS ** -0.5 * IDX_DIM ** -0.5)

    def sbody(c, _):
        off = pl.multiple_of(c * kc, kc)
        kk = ki_ref[pl.ds(off, kc), :]
        sc = jnp.zeros((QB, kc), jnp.float32)
        for h in range(IDX_HEADS):
            d = lax.dot_general(qi_ref[:, h * LANE:(h + 1) * LANE], kk, _NT, preferred_element_type=jnp.float32)
            sc = sc + wi[:, h:h + 1] * jnp.maximum(d, 0.0)
        kpos = off + lax.broadcasted_iota(jnp.int32, (1, kc), 1)
        sc_ref[c] = jnp.where(kpos <= qpos, sc, -inf)
        return 0

    lax.fori_loop(0, nch, sbody, 0)

    def fold_chunks(fn, init):
        def cbody(c, acc):
            off = pl.multiple_of(c * kc, kc)
            sc = sc_ref[c]
            for j in range(kc // LANE):
                kpos = off + j * LANE + lax.broadcasted_iota(jnp.int32, (1, LANE), 1)
                acc = fn(acc, sc[:, j * LANE:(j + 1) * LANE], kpos)
            return acc
        return lax.fori_loop(0, nch, cbody, init)

    def count(pred, t):
        tb = jnp.broadcast_to(t, (QB, LANE))
        acc = fold_chunks(lambda a, x, kp: a + jnp.where(pred(x, tb, kp), 1.0, 0.0),
                          jnp.zeros((QB, LANE), jnp.float32))
        return jnp.sum(acc, axis=-1, keepdims=True)

    def any_row(flag):
        return (jnp.max(jnp.where(flag, 1.0, 0.0)) > 0.5).astype(jnp.int32)

    thr_ref[...] = jnp.full_like(thr_ref, -inf)
    jst_ref[...] = jnp.full_like(jst_ref, -1)

    @pl.when(i * QB >= ksel)
    def _():
        mn0 = jnp.full((QB, LANE), inf, jnp.float32)
        mx0 = jnp.full((QB, LANE), -inf, jnp.float32)
        mn, mx = fold_chunks(
            lambda a, x, kp: (jnp.minimum(a[0], jnp.where(x == -inf, inf, x)), jnp.maximum(a[1], x)),
            (mn0, mx0))
        lo_ref[...] = jnp.min(mn, axis=-1, keepdims=True)
        hib_ref[...] = jnp.max(mx, axis=-1, keepdims=True)
        hi_ref[...] = jnp.full_like(hi_ref, inf)
        clo_ref[...] = (qpos + 1).astype(jnp.float32)

        def bis_cond(st):
            it, go = st
            return jnp.logical_and(it < n_bisect, go > 0)

        def bis_body(st):
            it, _ = st
            lo, hib = lo_ref[...], hib_ref[...]
            mid = 0.5 * lo + 0.5 * hib
            cnt = count(lambda x, tb, kp: x >= tb, mid)
            ge = cnt >= kf
            lo_ref[...] = jnp.where(ge, mid, lo)
            clo = jnp.where(ge, cnt, clo_ref[...])
            clo_ref[...] = clo
            hib_ref[...] = jnp.where(ge, hib, mid)
            hi_ref[...] = jnp.where(ge, hi_ref[...], mid)
            return it + 1, any_row(clo != kf)

        lax.while_loop(bis_cond, bis_body, (jnp.int32(0), jnp.int32(1)))

        resolved = clo_ref[...] == kf
        thr_ref[...] = jnp.where(resolved, lo_ref[...], inf)
        clo_ref[...] = jnp.where(resolved, kf, 0.0)

        def peel_body(go):
            done = clo_ref[...] >= kf
            hi = hi_ref[...]
            hb = jnp.broadcast_to(hi, (QB, LANE))
            nxt = fold_chunks(lambda a, x, kp: jnp.maximum(a, jnp.where(x < hb, x, -inf)),
                              jnp.full((QB, LANE), -inf, jnp.float32))
            t = jnp.where(done, thr_ref[...], jnp.max(nxt, axis=-1, keepdims=True))
            cnt = count(lambda x, tb, kp: x >= tb, t)
            thr_ref[...] = t
            clo_ref[...] = cnt
            hi_ref[...] = jnp.where(done, hi, t)
            return any_row(cnt < kf)

        lax.while_loop(lambda go: go > 0, peel_body, any_row(clo_ref[...] < kf))

        jst_ref[...] = jnp.full_like(jst_ref, s_len)

        @pl.when(any_row(clo_ref[...] > kf) > 0)
        def _():
            thr = thr_ref[...]
            need = kf - count(lambda x, tb, kp: x > tb, thr)
            excess = clo_ref[...] > kf

            def tie_body(bi, j):
                cand = j + lax.shift_left(jnp.int32(1), (s_len.bit_length() - 1) - bi)
                cb = jnp.broadcast_to(cand, (QB, LANE))
                tb = jnp.broadcast_to(thr, (QB, LANE))
                acc = fold_chunks(
                    lambda a, x, kp: a + jnp.where(jnp.logical_and(x == tb, kp < cb), 1.0, 0.0),
                    jnp.zeros((QB, LANE), jnp.float32))
                below = jnp.sum(acc, axis=-1, keepdims=True)
                return jnp.where(below < need, cand, j)

            j = lax.fori_loop(0, s_len.bit_length(), tie_body, jnp.zeros((QB, 1), jnp.int32))
            jst_ref[...] = jnp.where(excess, j, s_len)

    qlat = []
    for h in range(HEADS):
        hs = slice(h * HEAD_DIM, (h + 1) * HEAD_DIM)
        ql = lax.dot_general(q_ref[:, hs], wuk_ref[h], _NT, preferred_element_type=jnp.float32)
        qlat.append((ql * HEAD_DIM ** -0.5).astype(jnp.bfloat16))
    qlat = jnp.concatenate(qlat, axis=0)
    m_ref[...] = jnp.full_like(m_ref, NEG_BIG)
    l_ref[...] = jnp.zeros_like(l_ref)
    acc_ref[...] = jnp.zeros_like(acc_ref)
    thr = thr_ref[...]
    jst = jst_ref[...]

    def abody(c, _):
        off = pl.multiple_of(c * kc, kc)
        cn = cn_ref[pl.ds(off, kc), :]
        sc = sc_ref[c]
        kpos = off + lax.broadcasted_iota(jnp.int32, (1, kc), 1)
        sel = jnp.logical_or(sc > thr, jnp.logical_and(sc == thr, kpos <= jst))
        logits = lax.dot_general(qlat, cn, _NT, preferred_element_type=jnp.float32)
        for h in range(HEADS):
            rs = slice(h * QB, (h + 1) * QB)
            s = jnp.where(sel, logits[rs, :], NEG_BIG)
            m_old = m_ref[rs, :]
            m_new = jnp.maximum(m_old, jnp.max(s, axis=-1, keepdims=True))
            alpha = jnp.exp(m_old - m_new)
            p = jnp.exp(s - m_new)
            l_ref[rs, :] = alpha * l_ref[rs, :] + jnp.sum(p, axis=-1, keepdims=True)
            pv = jnp.dot(p.astype(jnp.bfloat16), cn, preferred_element_type=jnp.float32)
            acc_ref[rs, :] = alpha * acc_ref[rs, :] + pv
            m_ref[rs, :] = m_new
        return 0

    lax.fori_loop(0, nch, abody, 0)

    for h in range(HEADS):
        hs = slice(h * HEAD_DIM, (h + 1) * HEAD_DIM)
        rs = slice(h * QB, (h + 1) * QB)
        o_lat = (acc_ref[rs, :] / l_ref[rs, :]).astype(jnp.bfloat16)
        o = jnp.dot(o_lat, wuv_ref[h], preferred_element_type=jnp.float32)
        o_ref[:, hs] = (o * _silu(z_ref[:, hs])).astype(o_ref.dtype)


def _dsa(p16b, p32b, kv_g, w_uk, w_uv):
    b, s, _ = p16b.shape
    kc = min(512, s)
    ksel = min(TOPK_MAX, s // 4)
    assert ksel % QB == 0 and s % kc == 0
    col = lambda: pltpu.VMEM((QB, 1), jnp.float32)
    full3 = pl.BlockSpec((HEADS, LATENT, HEAD_DIM), lambda bi, i: (0, 0, 0))
    return pl.pallas_call(
        functools.partial(_dsa_kernel, s_len=s, kc=kc, ksel=ksel, n_bisect=24),
        grid=(b, s // QB),
        in_specs=[pl.BlockSpec((None, QB, W), lambda bi, i: (bi, i, P16_BQ)),
                  pl.BlockSpec((None, QB, W), lambda bi, i: (bi, i, P16_QI)),
                  pl.BlockSpec((None, QB, LANE), lambda bi, i: (bi, i, P32_SM)),
                  pl.BlockSpec((None, QB, W), lambda bi, i: (bi, i, P32_BZ)),
                  pl.BlockSpec((None, s, LANE), lambda bi, i: (bi, 0, P16_KI)),
                  pl.BlockSpec((None, s, LANE), lambda bi, i: (bi, 0, P32_BC)),
                  pl.BlockSpec((1, LATENT), lambda bi, i: (0, 0)),
                  full3, full3],
        out_specs=pl.BlockSpec((None, QB, W), lambda bi, i: (bi, i, 0)),
        out_shape=jax.ShapeDtypeStruct((b, s, W), jnp.bfloat16),
        scratch_shapes=[pltpu.VMEM((s, LATENT), jnp.bfloat16),
                        pltpu.VMEM((s // kc, QB, kc), jnp.float32),
                        col(),
                        pltpu.VMEM((QB, 1), jnp.int32),
                        col(), col(), col(), col(),
                        pltpu.VMEM((HEADS * QB, 1), jnp.float32),
                        pltpu.VMEM((HEADS * QB, 1), jnp.float32),
                        pltpu.VMEM((HEADS * QB, LATENT), jnp.float32)],
        compiler_params=_cparams(("arbitrary", "arbitrary")),
        name="dsa_attn",
    )(p16b, p16b, p32b, p32b, p16b, p32b, kv_g.reshape(1, LATENT),
      w_uk.astype(jnp.bfloat16), w_uv.astype(jnp.bfloat16))


def _dsat_kernel(q_ref, qi_ref, sm_ref, z_ref, ki_ref, c_ref, kvg_ref, wuk_ref, wuv_ref, o_ref,
                 cn_ref, cnt_ref, sc_ref, qia_ref, qlat_ref, lga_ref, lgb_ref, bias_ref, p_ref,
                 thr_ref, jst_ref, lo_ref, hib_ref, hi_ref, clo_ref,
                 m_ref, l_ref, acc_ref, *, s_len, kc, ksel, n_bisect):
    i = pl.program_id(1)
    ndiag = (i * QBE) // kc
    nch = ndiag + 1
    qpos = i * QBE + lax.broadcasted_iota(jnp.int32, (1, QBE), 1)
    kf = float(ksel)
    inf = float("inf")

    def key_pos(c):
        return c * kc + lax.broadcasted_iota(jnp.int32, (kc, QBE), 0)

    @pl.when(i == 0)
    def _():
        def nbody(c, _):
            off = pl.multiple_of(c * kc, kc)
            x = c_ref[pl.ds(off, kc), :]
            y = x * lax.rsqrt(jnp.mean(x * x, axis=-1, keepdims=True) + EPS) * kvg_ref[...]
            cn_ref[pl.ds(off, kc), :] = y.astype(cn_ref.dtype)
            cnt_ref[c] = jnp.transpose(y).astype(cnt_ref.dtype)
            return 0
        lax.fori_loop(0, s_len // kc, nbody, 0)

    wit = jnp.transpose(sm_ref[...])[HEADS:HEADS + IDX_HEADS, :] * (IDX_HEADS ** -0.5 * IDX_DIM ** -0.5)
    for h in range(IDX_HEADS):
        qia_ref[h * QBE:(h + 1) * QBE, :] = qi_ref[:, h * LANE:(h + 1) * LANE]

    def dots_into(buf_ref, c):
        off = pl.multiple_of(c * kc, kc)
        buf_ref[...] = lax.dot_general(ki_ref[pl.ds(off, kc), :], qia_ref[...], _NT,
                                       preferred_element_type=jnp.float32)

    def scores_from(buf_ref, c):
        sc = None
        for h in range(IDX_HEADS):
            term = wit[h:h + 1, :] * jnp.maximum(buf_ref[:, h * QBE:(h + 1) * QBE], 0.0)
            sc = term if sc is None else sc + term
        sc_ref[c] = sc

    def run_pairs(produce, consume):
        produce(lga_ref, 0)

        def pair_body(j, _):
            produce(lgb_ref, 2 * j + 1)
            consume(lga_ref, 2 * j)
            produce(lga_ref, jnp.minimum(2 * j + 2, ndiag))
            consume(lgb_ref, 2 * j + 1)
            return 0

        lax.fori_loop(0, nch // 2, pair_body, 0)

        @pl.when(nch % 2 == 1)
        def _():
            consume(lga_ref, ndiag)

    run_pairs(dots_into, scores_from)
    sc_ref[ndiag] = jnp.where(key_pos(ndiag) <= qpos, sc_ref[ndiag], -inf)

    def fold_slabs(op, fn, init_val):
        def cbody(c, acc):
            x = sc_ref[c]
            for j in range(kc // FOLD):
                acc = op(acc, fn(x[j * FOLD:(j + 1) * FOLD, :], c, j * FOLD))
            return acc
        return lax.fori_loop(0, nch, cbody, jnp.full((FOLD, QBE), init_val, jnp.float32))

    def count(pred, t):
        acc = fold_slabs(lambda a, m: jnp.where(m, a + 1.0, a), lambda x, c, r0: pred(x, t), 0.0)
        return jnp.sum(acc, axis=0, keepdims=True)

    def any_lane(flag):
        return (jnp.max(jnp.where(flag, 1.0, 0.0)) > 0.5).astype(jnp.int32)

    searched = qpos + 1 > ksel
    thr_ref[...] = jnp.full_like(thr_ref, -inf)
    jst_ref[...] = jnp.full_like(jst_ref, -1)

    @pl.when((i + 1) * QBE > ksel)
    def _():
        mn = fold_slabs(jnp.minimum, lambda x, c, r0: jnp.where(x == -inf, inf, x), inf)
        mx = fold_slabs(jnp.maximum, lambda x, c, r0: x, -inf)
        lo_ref[...] = jnp.min(mn, axis=0, keepdims=True)
        hib_ref[...] = jnp.max(mx, axis=0, keepdims=True)
        hi_ref[...] = jnp.full_like(hi_ref, inf)
        clo_ref[...] = (qpos + 1).astype(jnp.float32)

        def bis_body(_, carry):
            lo, hib = lo_ref[...], hib_ref[...]
            mid = 0.5 * lo + 0.5 * hib
            cnt = count(lambda x, t: x >= t, mid)
            ge = cnt >= kf
            lo_ref[...] = jnp.where(ge, mid, lo)
            clo_ref[...] = jnp.where(ge, cnt, clo_ref[...])
            hib_ref[...] = jnp.where(ge, hib, mid)
            hi_ref[...] = jnp.where(ge, hi_ref[...], mid)
            return carry

        lax.fori_loop(0, n_bisect, bis_body, 0)

        resolved = jnp.logical_or(clo_ref[...] == kf, jnp.logical_not(searched))
        thr_ref[...] = jnp.where(resolved, lo_ref[...], inf)
        clo_ref[...] = jnp.where(resolved, kf, 0.0)

        def peel_body(go):
            done = clo_ref[...] >= kf
            hi = hi_ref[...]
            nxt = fold_slabs(jnp.maximum, lambda x, c, r0: jnp.where(x < hi, x, -inf), -inf)
            t = jnp.where(done, thr_ref[...], jnp.max(nxt, axis=0, keepdims=True))
            cnt = count(lambda x, tt: x >= tt, t)
            thr_ref[...] = t
            clo_ref[...] = jnp.where(done, clo_ref[...], cnt)
            hi_ref[...] = jnp.where(done, hi, t)
            return any_lane(jnp.logical_and(jnp.logical_not(done), cnt < kf))

        lax.while_loop(lambda go: go > 0, peel_body, any_lane(clo_ref[...] < kf))

        jst_ref[...] = jnp.where(searched, s_len, -1)
        thr_ref[...] = jnp.where(searched, thr_ref[...], -inf)
        excess = jnp.logical_and(searched, clo_ref[...] > kf)

        @pl.when(any_lane(excess) > 0)
        def _():
            thr = thr_ref[...]
            need = kf - count(lambda x, t: x > t, thr)
            tril = jnp.where(lax.broadcasted_iota(jnp.int32, (kc, kc), 0) >=
                             lax.broadcasted_iota(jnp.int32, (kc, kc), 1), 1.0, 0.0).astype(jnp.bfloat16)

            def tie_body(c, carry):
                seen, last = carry
                tie = sc_ref[c] == thr
                rank = jnp.dot(tril, jnp.where(tie, 1.0, 0.0).astype(jnp.bfloat16),
                               preferred_element_type=jnp.float32) + seen
                keep = jnp.logical_and(tie, rank <= need)
                kept_pos = jnp.where(keep, key_pos(c).astype(jnp.float32), -1.0)
                return rank[kc - 1:kc, :], jnp.maximum(last, jnp.max(kept_pos, axis=0, keepdims=True))

            _, last = lax.fori_loop(0, nch, tie_body, (jnp.zeros((1, QBE), jnp.float32),
                                                       jnp.full((1, QBE), -1.0, jnp.float32)))
            jst_ref[...] = jnp.where(excess, last.astype(jnp.int32), jst_ref[...])

    for h in range(HEADS):
        hs = slice(h * HEAD_DIM, (h + 1) * HEAD_DIM)
        ql = lax.dot_general(q_ref[:, hs], wuk_ref[h], _NT, preferred_element_type=jnp.float32)
        qlat_ref[h * QBE:(h + 1) * QBE, :] = (ql * (HEAD_DIM ** -0.5 * LOG2E)).astype(qlat_ref.dtype)
    m_ref[...] = jnp.full_like(m_ref, NEG_BIG)
    l_ref[...] = jnp.zeros_like(l_ref)
    acc_ref[...] = jnp.zeros_like(acc_ref)
    thr = thr_ref[...]
    jst = jst_ref[...]

    def logits_into(buf_ref, c):
        off = pl.multiple_of(c * kc, kc)
        buf_ref[...] = lax.dot_general(cn_ref[pl.ds(off, kc), :], qlat_ref[...], _NT,
                                       preferred_element_type=jnp.float32)

    def softmax_pv(buf_ref, c):
        sc = sc_ref[c]
        sel = jnp.logical_or(sc > thr, jnp.logical_and(sc == thr, key_pos(c) <= jst))
        bias_ref[...] = jnp.where(sel, 0.0, NEG_BIG)
        for h in range(HEADS):
            qs = slice(h * QBE, (h + 1) * QBE)
            st = buf_ref[:, qs] + bias_ref[...]
            m_old = m_ref[:, qs]
            m_new = jnp.maximum(m_old, jnp.max(st, axis=0, keepdims=True))
            alpha = jnp.exp2(m_old - m_new)
            p = jnp.exp2(st - m_new)
            l_ref[:, qs] = alpha * l_ref[:, qs] + jnp.sum(p, axis=0, keepdims=True)
            pv = jnp.dot(cnt_ref[c], p.astype(jnp.bfloat16), preferred_element_type=jnp.float32)
            acc_ref[:, qs] = alpha * acc_ref[:, qs] + pv
            m_ref[:, qs] = m_new

    run_pairs(logits_into, softmax_pv)

    for h in range(HEADS):
        hs = slice(h * HEAD_DIM, (h + 1) * HEAD_DIM)
        qs = slice(h * QBE, (h + 1) * QBE)
        o_lat = jnp.transpose(acc_ref[:, qs] / l_ref[:, qs]).astype(jnp.bfloat16)
        o = jnp.dot(o_lat, wuv_ref[h], preferred_element_type=jnp.float32)
        o_ref[:, hs] = (o * _silu(z_ref[:, hs])).astype(o_ref.dtype)


def _dsat(p16b, p32b, kv_g, w_uk, w_uv):
    b, s, _ = p16b.shape
    kc = min(KC, s)
    ksel = min(TOPK_MAX, s // 4)
    assert s % kc == 0 and kc % QBE == 0
    nq = HEADS * QBE
    row = lambda n: pltpu.VMEM((1, n), jnp.float32)
    full3 = pl.BlockSpec((HEADS, LATENT, HEAD_DIM), lambda bi, i: (0, 0, 0))
    return pl.pallas_call(
        functools.partial(_dsat_kernel, s_len=s, kc=kc, ksel=ksel, n_bisect=16),
        grid=(b, s // QBE),
        in_specs=[pl.BlockSpec((None, QBE, W), lambda bi, i: (bi, i, P16_BQ)),
                  pl.BlockSpec((None, QBE, W), lambda bi, i: (bi, i, P16_QI)),
                  pl.BlockSpec((None, QBE, LANE), lambda bi, i: (bi, i, P32_SM)),
                  pl.BlockSpec((None, QBE, W), lambda bi, i: (bi, i, P32_BZ)),
                  pl.BlockSpec((None, s, LANE), lambda bi, i: (bi, 0, P16_KI)),
                  pl.BlockSpec((None, s, LANE), lambda bi, i: (bi, 0, P32_BC)),
                  pl.BlockSpec((1, LATENT), lambda bi, i: (0, 0)),
                  full3, full3],
        out_specs=pl.BlockSpec((None, QBE, W), lambda bi, i: (bi, i, 0)),
        out_shape=jax.ShapeDtypeStruct((b, s, W), jnp.bfloat16),
        scratch_shapes=[pltpu.VMEM((s, LATENT), jnp.bfloat16),
                        pltpu.VMEM((s // kc, LATENT, kc), jnp.bfloat16),
                        pltpu.VMEM((s // kc, kc, QBE), jnp.float32),
                        pltpu.VMEM((IDX_HEADS * QBE, LANE), jnp.bfloat16),
                        pltpu.VMEM((nq, LATENT), jnp.bfloat16),
                        pltpu.VMEM((kc, nq), jnp.float32),
                        pltpu.VMEM((kc, nq), jnp.float32),
                        pltpu.VMEM((kc, QBE), jnp.float32),
                        pltpu.VMEM((kc, nq), jnp.bfloat16),
                        row(QBE),
                        pltpu.VMEM((1, QBE), jnp.int32),
                        row(QBE), row(QBE), row(QBE), row(QBE),
                        row(nq), row(nq),
                        pltpu.VMEM((LATENT, nq), jnp.float32)],
        compiler_params=_cparams(("arbitrary", "arbitrary")),
        name="dsa_attn",
    )(p16b, p16b, p32b, p32b, p16b, p32b, kv_g.reshape(1, LATENT),
      w_uk.astype(jnp.bfloat16), w_uv.astype(jnp.bfloat16))


def _merge_kernel(au_ref, av_ref, az_ref, cb_ref, cc_ref, cx_ref, cz_ref,
                  lng_ref, lnb_ref, ws_ref, bs_ref, cw_ref,
                  yb_ref, yd_ref, hn_in_ref, x_ref, wg_ref, wb_ref, wo_ref, gn_ref,
                  h_ref, hn_ref, ya_ref, yc_ref, halo_ref, *, tm, tiles_per_seq):
    first_tile = pl.program_id(0) % tiles_per_seq == 0
    _mix_ac_tile(au_ref, av_ref, az_ref, cb_ref, cc_ref, cx_ref, cz_ref,
                 lng_ref, lnb_ref, ws_ref, bs_ref, cw_ref, ya_ref, yc_ref, halo_ref, first_tile, tm)
    hn_in = hn_in_ref[...]
    merged = None
    for n, y_ref in enumerate((ya_ref, yb_ref, yc_ref, yd_ref)):
        cols = slice(n * D_MODEL, (n + 1) * D_MODEL)
        gate = _sigmoid(jnp.dot(hn_in, wg_ref[:, cols], preferred_element_type=jnp.float32))
        term = gate * jnp.dot(y_ref[...], wb_ref[n], preferred_element_type=jnp.float32)
        merged = term if merged is None else merged + term
    h = x_ref[...] + jnp.dot(merged.astype(jnp.bfloat16), wo_ref[...], preferred_element_type=jnp.float32)
    h_ref[...] = h
    hn = h * lax.rsqrt(jnp.mean(h * h, axis=-1, keepdims=True) + EPS) * gn_ref[...]
    hn_ref[...] = hn.astype(hn_ref.dtype)


def _merge(p32, yb, yd, hn, h, ln_g, ln_b, w_s, b_s, conv_w, wg, wb, wo, g_next, hn_dtype, s_len, tm=256):
    m, d = h.shape
    assert s_len % tm == 0 and tm % GM_CHUNK == 0
    pblk = lambda idx: pl.BlockSpec((tm, W), lambda i, idx=idx: (i, idx))
    yblk = pl.BlockSpec((tm, W), lambda i: (i, 0))
    hblk = pl.BlockSpec((tm, d), lambda i: (i, 0))
    const = lambda shp: pl.BlockSpec(shp, lambda i: (0,) * len(shp))
    bs_full = jnp.repeat(jnp.transpose(b_s), LANE, axis=1)
    return pl.pallas_call(
        functools.partial(_merge_kernel, tm=tm, tiles_per_seq=s_len // tm),
        grid=(m // tm,),
        in_specs=[pblk(P32_AU), pblk(P32_AV), pblk(P32_AZ), pblk(P32_CB), pblk(P32_CC), pblk(P32_CX), pblk(P32_CZ),
                  const((1, W)), const((1, W)), const((GM_GROUPS, GM_CHUNK, GM_CHUNK)),
                  const((GM_CHUNK, W)), const((CONV_WIDTH, W)),
                  yblk, yblk, hblk, hblk,
                  const((d, N_BRANCH * d)), const((N_BRANCH, W, d)), const((d, d)), const((1, d))],
        out_specs=[hblk, hblk],
        out_shape=[jax.ShapeDtypeStruct((m, d), jnp.float32), jax.ShapeDtypeStruct((m, d), hn_dtype)],
        scratch_shapes=[pltpu.VMEM((tm, W), jnp.bfloat16), pltpu.VMEM((tm, W), jnp.bfloat16),
                        pltpu.VMEM((8, W), jnp.float32)],
        compiler_params=_cparams(("arbitrary",)),
        name="merge",
    )(p32, p32, p32, p32, p32, p32, p32,
      ln_g.reshape(1, W), ln_b.reshape(1, W), w_s, bs_full, conv_w,
      yb, yd, hn, h, wg, wb, wo, g_next.reshape(1, d))


def kernel(x, norm_g, w_in, gm_ln_g, gm_ln_b, gm_w_s, gm_b_s, dsa_kv_g, dsa_w_uk, dsa_w_uv,
           conv_w, fox_b_f, w_branch, w_out, final_g):
    b, s, d = x.shape
    depth = w_in.shape[0]
    m = b * s
    h = x.reshape(m, d)
    hn = _rmsnorm(h, norm_g[0], jnp.bfloat16)
    for l in range(depth):
        w16, w32, wvt, wg = _prep_w_in(w_in[l])
        p16 = _matmul(hn, w16, jnp.bfloat16, tm=512, tn=N16, name="in_proj16")
        p32 = _matmul(hn, w32, jnp.float32, tm=512, tn=N32 // 2, name="in_proj32")
        vt = _matmul_t(hn, wvt, tm=min(KC, s), name="in_proj_vt")
        p16b = p16.reshape(b, s, N16)
        p32b = p32.reshape(b, s, N32)
        qaug, kaug = _fox_cum(p32b, fox_b_f[l])
        yd = _fox(p16b, p32b, qaug, kaug, vt)
        yb = _dsat(p16b, p32b, dsa_kv_g[l], dsa_w_uk[l], dsa_w_uv[l])
        last = l == depth - 1
        g_next = final_g if last else norm_g[l + 1]
        h, hn = _merge(p32, yb.reshape(m, W), yd.reshape(m, W), hn, h,
                       gm_ln_g[l], gm_ln_b[l], gm_w_s[l], gm_b_s[l], conv_w[l],
                       wg, w_branch[l].astype(jnp.bfloat16), w_out[l].astype(jnp.bfloat16), g_next,
                       jnp.float32 if last else jnp.bfloat16, s)
    return hn.reshape(b, s, d)
```

```python
import functools

import numpy as np
import jax
import jax.numpy as jnp
from jax import lax
from jax.experimental import pallas as pl
from jax.experimental.pallas import tpu as pltpu

D_MODEL = 1024
N_BRANCH = 4
W = 512
EPS = 1e-6
QB = 128
QBE = 256
KC = 512
FOLD = 64
CUM_ROWS = 512
GM_GROUPS = 4
GM_CHUNK = 128
HEADS = 4
HEAD_DIM = W // HEADS
LATENT = 128
IDX_HEADS = 4
IDX_DIM = 64
TOPK_MAX = 256
CONV_WIDTH = 3
LANE = 128
NEG_BIG = -1e30
LOG2E = 1.4426950408889634
VMEM_LIMIT = 56 * 1024 * 1024

P16_DQ, P16_DK, P16_BQ = 0, 1, 2
P16_QI = 3
P16_KI = 16
N16 = 17 * LANE
P32_AU, P32_AV, P32_AZ, P32_BZ = 0, 1, 2, 3
P32_CB, P32_CC, P32_CX, P32_CZ, P32_DZ = 4, 5, 6, 7, 8
P32_BC = 36
P32_SM = 37
N32 = 38 * LANE

_NT = (((1,), (1,)), ((), ()))


def _cparams(sem):
    return pltpu.CompilerParams(dimension_semantics=sem, vmem_limit_bytes=VMEM_LIMIT)


def _sigmoid(z):
    return 1.0 / (1.0 + jnp.exp(-z))


def _silu(z):
    return z * _sigmoid(z)


def _rmsnorm_kernel(x_ref, g_ref, o_ref):
    x = x_ref[...]
    y = x * lax.rsqrt(jnp.mean(x * x, axis=-1, keepdims=True) + EPS)
    o_ref[...] = (y * g_ref[...]).astype(o_ref.dtype)


def _rmsnorm(x2, g, out_dtype, tm=512):
    m, d = x2.shape
    return pl.pallas_call(
        _rmsnorm_kernel,
        grid=(m // tm,),
        in_specs=[pl.BlockSpec((tm, d), lambda i: (i, 0)), pl.BlockSpec((1, d), lambda i: (0, 0))],
        out_specs=pl.BlockSpec((tm, d), lambda i: (i, 0)),
        out_shape=jax.ShapeDtypeStruct((m, d), out_dtype),
        compiler_params=_cparams(("arbitrary",)),
        name="rmsnorm",
    )(x2, g.reshape(1, d))


def _matmul_kernel(a_ref, w_ref, o_ref):
    o_ref[...] = jnp.dot(a_ref[...], w_ref[...], preferred_element_type=jnp.float32).astype(o_ref.dtype)


def _matmul(a, w, out_dtype, tm, tn, name):
    m, k = a.shape
    n = w.shape[1]
    return pl.pallas_call(
        _matmul_kernel,
        grid=(n // tn, m // tm),
        in_specs=[pl.BlockSpec((tm, k), lambda j, i: (i, 0)), pl.BlockSpec((k, tn), lambda j, i: (0, j))],
        out_specs=pl.BlockSpec((tm, tn), lambda j, i: (i, j)),
        out_shape=jax.ShapeDtypeStruct((m, n), out_dtype),
        compiler_params=_cparams(("arbitrary", "arbitrary")),
        name=name,
    )(a, w)


def _prep_w_in(w):
    d = w.shape[0]
    sizes = (W, W, W,
             W, LATENT, IDX_HEADS * IDX_DIM, IDX_DIM, IDX_HEADS, W,
             W, W, W, W,
             W, W, W, HEADS, W,
             N_BRANCH * D_MODEL)
    parts, off = [], 0
    for s in sizes:
        parts.append(w[:, off:off + s])
        off += s
    (a_u, a_v, a_z, b_q, b_c, b_qi, b_ki, b_wi, b_z,
     c_b, c_c, c_x, c_z, d_q, d_k, d_v, d_f, d_z, gates) = parts
    zeros = lambda n: jnp.zeros((d, n), w.dtype)
    qi = []
    for h in range(IDX_HEADS):
        qi += [b_qi[:, h * IDX_DIM:(h + 1) * IDX_DIM], zeros(LANE - IDX_DIM)]
    w16 = jnp.concatenate([d_q, d_k, b_q] + qi + [b_ki, zeros(LANE - IDX_DIM)], axis=1)
    w32 = jnp.concatenate([a_u, a_v, a_z, b_z, c_b, c_c, c_x, c_z, d_z, b_c,
                           d_f, b_wi, zeros(LANE - HEADS - IDX_HEADS)], axis=1)
    bf = jnp.bfloat16
    return w16.astype(bf), w32.astype(bf), jnp.transpose(d_v).astype(bf), gates.astype(bf)


def _matmul_t_kernel(wt_ref, a_ref, o_ref):
    o_ref[...] = lax.dot_general(wt_ref[...], a_ref[...], _NT,
                                 preferred_element_type=jnp.float32).astype(o_ref.dtype)


def _matmul_t(a, wt, tm, name):
    m, k = a.shape
    n = wt.shape[0]
    return pl.pallas_call(
        _matmul_t_kernel,
        grid=(m // tm,),
        in_specs=[pl.BlockSpec((n, k), lambda i: (0, 0)), pl.BlockSpec((tm, k), lambda i: (i, 0))],
        out_specs=pl.BlockSpec((None, n, tm), lambda i: (i, 0, 0)),
        out_shape=jax.ShapeDtypeStruct((m // tm, n, tm), jnp.bfloat16),
        compiler_params=_cparams(("arbitrary",)),
        name=name,
    )(wt, a)


def _mix_ac_tile(au_ref, av_ref, az_ref, cb_ref, cc_ref, cx_ref, cz_ref,
                 lng_ref, lnb_ref, ws_ref, bs_ref, cw_ref, ya_ref, yc_ref, halo_ref, first_tile, tt):
    v = av_ref[...]
    mu = jnp.mean(v, axis=-1, keepdims=True)
    vc = v - mu
    var = jnp.mean(vc * vc, axis=-1, keepdims=True)
    vn = (vc * lax.rsqrt(var + EPS) * lng_ref[...] + lnb_ref[...]).astype(jnp.bfloat16)
    row = lax.broadcasted_iota(jnp.int32, (GM_CHUNK, GM_CHUNK), 0)
    col = lax.broadcasted_iota(jnp.int32, (GM_CHUNK, GM_CHUNK), 1)
    tril = row >= col
    wg = [jnp.where(tril, ws_ref[g], 0.0).astype(jnp.bfloat16) for g in range(GM_GROUPS)]
    for ch in range(tt // GM_CHUNK):
        rows = slice(ch * GM_CHUNK, (ch + 1) * GM_CHUNK)
        for g in range(GM_GROUPS):
            cols = slice(g * LANE, (g + 1) * LANE)
            mixed = jnp.dot(wg[g], vn[rows, cols], preferred_element_type=jnp.float32) + bs_ref[:, cols]
            ya_ref[rows, cols] = (au_ref[rows, cols] * mixed * _silu(az_ref[rows, cols])).astype(ya_ref.dtype)

    @pl.when(first_tile)
    def _():
        halo_ref[...] = jnp.zeros_like(halo_ref)

    y = cc_ref[...] * cx_ref[...]
    ext = jnp.concatenate([halo_ref[...], y], axis=0)
    conv = cw_ref[2:3, :] * y
    for j in range(CONV_WIDTH - 1):
        shift = CONV_WIDTH - 1 - j
        conv = conv + cw_ref[j:j + 1, :] * ext[8 - shift:8 - shift + tt, :]
    yc_ref[...] = (cb_ref[...] * conv * _silu(cz_ref[...])).astype(yc_ref.dtype)
    halo_ref[...] = y[tt - 8:, :]


def _split3(x):
    hi = x.astype(jnp.bfloat16)
    r1 = x - hi.astype(jnp.float32)
    mid = r1.astype(jnp.bfloat16)
    lo = (r1 - mid.astype(jnp.float32)).astype(jnp.bfloat16)
    return hi, mid, lo


def _aug_placement():
    pq = np.zeros((3 * LANE, W), np.float32)
    pk = np.zeros((3 * LANE, W), np.float32)
    cq = np.zeros((1, W), np.float32)
    ck = np.zeros((1, W), np.float32)
    for h in range(HEADS):
        for j in range(3):
            pq[j * LANE + h, h * HEAD_DIM + j] = 1.0
            cq[0, h * HEAD_DIM + 3 + j] = 1.0
            pk[j * LANE + h, h * HEAD_DIM + 3 + j] = -1.0
            ck[0, h * HEAD_DIM + j] = 1.0
    return (jnp.asarray(pq, jnp.bfloat16), jnp.asarray(pk, jnp.bfloat16), jnp.asarray(cq), jnp.asarray(ck))


def _fox_cum_kernel(f_ref, bias_ref, pq_ref, pk_ref, cq_ref, ck_ref, qa_ref, ka_ref, carry_ref, *, s):
    rows = min(CUM_ROWS, s)
    row = lax.broadcasted_iota(jnp.int32, (rows, rows), 0)
    col = lax.broadcasted_iota(jnp.int32, (rows, rows), 1)
    ones_tril = jnp.where(row >= col, 1.0, 0.0).astype(jnp.bfloat16)
    carry_ref[...] = jnp.zeros_like(carry_ref)

    def body(c, _):
        off = pl.multiple_of(c * rows, rows)
        x = f_ref[pl.ds(off, rows), :] + bias_ref[...]
        ls = jnp.minimum(x, 0.0) - jnp.log1p(jnp.exp(-jnp.abs(x)))
        hi, mid, lo = _split3(ls * LOG2E)
        dot = lambda p: jnp.dot(ones_tril, p, preferred_element_type=jnp.float32)
        cs = (dot(hi) + dot(mid)) + dot(lo) + carry_ref[0:1, :]
        carry_ref[0:1, :] = cs[rows - 1:rows, :]
        parts = jnp.concatenate(_split3(cs), axis=1)
        qa = jnp.dot(parts, pq_ref[...], preferred_element_type=jnp.float32) + cq_ref[...]
        ka = jnp.dot(parts, pk_ref[...], preferred_element_type=jnp.float32) + ck_ref[...]
        qa_ref[pl.ds(off, rows), :] = qa.astype(qa_ref.dtype)
        ka_ref[pl.ds(off, rows), :] = ka.astype(ka_ref.dtype)
        return 0

    lax.fori_loop(0, s // rows, body, 0)


def _fox_cum(p32b, b_f):
    b, s, _ = p32b.shape
    bias = jnp.zeros((1, LANE), jnp.float32).at[0, :HEADS].set(b_f)
    pq, pk, cq, ck = _aug_placement()
    const = lambda shp: pl.BlockSpec(shp, lambda bi: (0, 0))
    out = jax.ShapeDtypeStruct((b, s, W), jnp.bfloat16)
    return pl.pallas_call(
        functools.partial(_fox_cum_kernel, s=s),
        grid=(b,),
        in_specs=[pl.BlockSpec((None, s, LANE), lambda bi: (bi, 0, P32_SM)), const((1, LANE)),
                  const((3 * LANE, W)), const((3 * LANE, W)), const((1, W)), const((1, W))],
        out_specs=[pl.BlockSpec((None, s, W), lambda bi: (bi, 0, 0))] * 2,
        out_shape=[out, out],
        scratch_shapes=[pltpu.VMEM((8, LANE), jnp.float32)],
        compiler_params=_cparams(("arbitrary",)),
        name="fox_cum",
    )(p32b, bias, pq, pk, cq, ck)


def _num_chunks(i, kc):
    return ((i + 1) * QB + kc - 1) // kc


def _fox_kernel(q_ref, qa_ref, k_ref, ka_ref, vt_ref, z_ref, o_ref,
                qf_ref, sa_ref, sb_ref, m_ref, l_ref, acc_ref, *, kc):
    i = pl.program_id(1)
    ndiag = (i * QBE) // kc
    scale = HEAD_DIM ** -0.5 * LOG2E
    for h in range(HEADS):
        hs = slice(h * HEAD_DIM, (h + 1) * HEAD_DIM)
        qs = (q_ref[:, hs].astype(jnp.float32) * scale).astype(jnp.bfloat16)
        qf_ref[h] = jnp.concatenate([qs, qa_ref[:, hs]], axis=1)
    m_ref[...] = jnp.full_like(m_ref, NEG_BIG)
    l_ref[...] = jnp.zeros_like(l_ref)
    acc_ref[...] = jnp.zeros_like(acc_ref)

    def scores_into(buf_ref, c):
        off = pl.multiple_of(c * kc, kc)
        for h in range(HEADS):
            hs = slice(h * HEAD_DIM, (h + 1) * HEAD_DIM)
            kf = jnp.concatenate([k_ref[pl.ds(off, kc), hs], ka_ref[pl.ds(off, kc), hs]], axis=1)
            buf_ref[:, h * QBE:(h + 1) * QBE] = lax.dot_general(
                kf, qf_ref[h], _NT, preferred_element_type=jnp.float32)

    def softmax_pv(buf_ref, c, masked):
        if masked:
            causal = (c * kc + lax.broadcasted_iota(jnp.int32, (kc, QBE), 0) <=
                      i * QBE + lax.broadcasted_iota(jnp.int32, (kc, QBE), 1))
        for h in range(HEADS):
            hs = slice(h * HEAD_DIM, (h + 1) * HEAD_DIM)
            st = buf_ref[:, h * QBE:(h + 1) * QBE]
            if masked:
                st = jnp.where(causal, st, NEG_BIG)
            m_old = m_ref[h]
            m_new = jnp.maximum(m_old, jnp.max(st, axis=0, keepdims=True))
            alpha = jnp.exp2(m_old - m_new)
            p = jnp.exp2(st - m_new)
            l_ref[h] = alpha * l_ref[h] + jnp.sum(p, axis=0, keepdims=True)
            pv = jnp.dot(vt_ref[c, hs, :], p.astype(jnp.bfloat16), preferred_element_type=jnp.float32)
            acc_ref[h] = alpha * acc_ref[h] + pv
            m_ref[h] = m_new

    scores_into(sa_ref, 0)

    def pair_body(j, _):
        scores_into(sb_ref, 2 * j + 1)
        softmax_pv(sa_ref, 2 * j, False)
        scores_into(sa_ref, 2 * j + 2)
        softmax_pv(sb_ref, 2 * j + 1, False)
        return 0

    lax.fori_loop(0, ndiag // 2, pair_body, 0)

    @pl.when(ndiag % 2 == 1)
    def _():
        scores_into(sb_ref, ndiag)
        softmax_pv(sa_ref, ndiag - 1, False)
        softmax_pv(sb_ref, ndiag, True)

    @pl.when(ndiag % 2 == 0)
    def _():
        softmax_pv(sa_ref, ndiag, True)

    for h in range(HEADS):
        hs = slice(h * HEAD_DIM, (h + 1) * HEAD_DIM)
        o = jnp.transpose(acc_ref[h] / l_ref[h])
        o_ref[:, hs] = (o * _silu(z_ref[:, hs])).astype(o_ref.dtype)


def _fox(p16b, p32b, qaug, kaug, vt):
    b, s, _ = p16b.shape
    kc = vt.shape[-1]
    assert s % kc == 0 and kc % QBE == 0
    nck = s // kc
    vt = vt.reshape(b, nck, W, kc)
    row = lambda: pltpu.VMEM((HEADS, 1, QBE), jnp.float32)
    return pl.pallas_call(
        functools.partial(_fox_kernel, kc=kc),
        grid=(b, s // QBE),
        in_specs=[pl.BlockSpec((None, QBE, W), lambda bi, i: (bi, i, P16_DQ)),
                  pl.BlockSpec((None, QBE, W), lambda bi, i: (bi, i, 0)),
                  pl.BlockSpec((None, s, W), lambda bi, i: (bi, 0, P16_DK)),
                  pl.BlockSpec((None, s, W), lambda bi, i: (bi, 0, 0)),
                  pl.BlockSpec((None, nck, W, kc), lambda bi, i: (bi, 0, 0, 0)),
                  pl.BlockSpec((None, QBE, W), lambda bi, i: (bi, i, P32_DZ))],
        out_specs=pl.BlockSpec((None, QBE, W), lambda bi, i: (bi, i, 0)),
        out_shape=jax.ShapeDtypeStruct((b, s, W), jnp.bfloat16),
        scratch_shapes=[pltpu.VMEM((HEADS, QBE, 2 * HEAD_DIM), jnp.bfloat16),
                        pltpu.VMEM((kc, HEADS * QBE), jnp.float32),
                        pltpu.VMEM((kc, HEADS * QBE), jnp.float32),
                        row(), row(),
                        pltpu.VMEM((HEADS, HEAD_DIM, QBE), jnp.float32)],
        compiler_params=_cparams(("arbitrary", "arbitrary")),
        name="fox_attn",
    )(p16b, qaug, p16b, kaug, vt, p32b)


def _dsa_kernel(q_ref, qi_ref, sm_ref, z_ref, ki_ref, c_ref, kvg_ref, wuk_ref, wuv_ref, o_ref,
                cn_ref, sc_ref, thr_ref, jst_ref, lo_ref, hib_ref, hi_ref, clo_ref,
                m_ref, l_ref, acc_ref, *, s_len, kc, ksel, n_bisect):
    i = pl.program_id(1)
    nch = _num_chunks(i, kc)
    qpos = i * QB + lax.broadcasted_iota(jnp.int32, (QB, 1), 0)
    kf = float(ksel)
    inf = float("inf")

    @pl.when(i == 0)
    def _():
        def nbody(c, _):
            off = pl.multiple_of(c * kc, kc)
            x = c_ref[pl.ds(off, kc), :]
            y = x * lax.rsqrt(jnp.mean(x * x, axis=-1, keepdims=True) + EPS) * kvg_ref[...]
            cn_ref[pl.ds(off, kc), :] = y.astype(cn_ref.dtype)
            return 0
        lax.fori_loop(0, s_len // kc, nbody, 0)

    wi = sm_ref[:, HEADS:HEADS + IDX_HEADS] * (IDX_HEADS ** -0.5 * IDX_DIM ** -0.5)

    def sbody(c, _):
        off = pl.multiple_of(c * kc, kc)
        kk = ki_ref[pl.ds(off, kc), :]
        sc = jnp.zeros((QB, kc), jnp.float32)
        for h in range(IDX_HEADS):
            d = lax.dot_general(qi_ref[:, h * LANE:(h + 1) * LANE], kk, _NT, preferred_element_type=jnp.float32)
            sc = sc + wi[:, h:h + 1] * jnp.maximum(d, 0.0)
        kpos = off + lax.broadcasted_iota(jnp.int32, (1, kc), 1)
        sc_ref[c] = jnp.where(kpos <= qpos, sc, -inf)
        return 0

    lax.fori_loop(0, nch, sbody, 0)

    def fold_chunks(fn, init):
        def cbody(c, acc):
            off = pl.multiple_of(c * kc, kc)
            sc = sc_ref[c]
            for j in range(kc // LANE):
                kpos = off + j * LANE + lax.broadcasted_iota(jnp.int32, (1, LANE), 1)
                acc = fn(acc, sc[:, j * LANE:(j + 1) * LANE], kpos)
            return acc
        return lax.fori_loop(0, nch, cbody, init)

    def count(pred, t):
        tb = jnp.broadcast_to(t, (QB, LANE))
        acc = fold_chunks(lambda a, x, kp: a + jnp.where(pred(x, tb, kp), 1.0, 0.0),
                          jnp.zeros((QB, LANE), jnp.float32))
        return jnp.sum(acc, axis=-1, keepdims=True)

    def any_row(flag):
        return (jnp.max(jnp.where(flag, 1.0, 0.0)) > 0.5).astype(jnp.int32)

    thr_ref[...] = jnp.full_like(thr_ref, -inf)
    jst_ref[...] = jnp.full_like(jst_ref, -1)

    @pl.when(i * QB >= ksel)
    def _():
        mn0 = jnp.full((QB, LANE), inf, jnp.float32)
        mx0 = jnp.full((QB, LANE), -inf, jnp.float32)
        mn, mx = fold_chunks(
            lambda a, x, kp: (jnp.minimum(a[0], jnp.where(x == -inf, inf, x)), jnp.maximum(a[1], x)),
            (mn0, mx0))
        lo_ref[...] = jnp.min(mn, axis=-1, keepdims=True)
        hib_ref[...] = jnp.max(mx, axis=-1, keepdims=True)
        hi_ref[...] = jnp.full_like(hi_ref, inf)
        clo_ref[...] = (qpos + 1).astype(jnp.float32)

        def bis_cond(st):
            it, go = st
            return jnp.logical_and(it < n_bisect, go > 0)

        def bis_body(st):
            it, _ = st
            lo, hib = lo_ref[...], hib_ref[...]
            mid = 0.5 * lo + 0.5 * hib
            cnt = count(lambda x, tb, kp: x >= tb, mid)
            ge = cnt >= kf
            lo_ref[...] = jnp.where(ge, mid, lo)
            clo = jnp.where(ge, cnt, clo_ref[...])
            clo_ref[...] = clo
            hib_ref[...] = jnp.where(ge, hib, mid)
            hi_ref[...] = jnp.where(ge, hi_ref[...], mid)
            return it + 1, any_row(clo != kf)

        lax.while_loop(bis_cond, bis_body, (jnp.int32(0), jnp.int32(1)))

        resolved = clo_ref[...] == kf
        thr_ref[...] = jnp.where(resolved, lo_ref[...], inf)
        clo_ref[...] = jnp.where(resolved, kf, 0.0)

        def peel_body(go):
            done = clo_ref[...] >= kf
            hi = hi_ref[...]
            hb = jnp.broadcast_to(hi, (QB, LANE))
            nxt = fold_chunks(lambda a, x, kp: jnp.maximum(a, jnp.where(x < hb, x, -inf)),
                              jnp.full((QB, LANE), -inf, jnp.float32))
            t = jnp.where(done, thr_ref[...], jnp.max(nxt, axis=-1, keepdims=True))
            cnt = count(lambda x, tb, kp: x >= tb, t)
            thr_ref[...] = t
            clo_ref[...] = cnt
            hi_ref[...] = jnp.where(done, hi, t)
            return any_row(cnt < kf)

        lax.while_loop(lambda go: go > 0, peel_body, any_row(clo_ref[...] < kf))

        jst_ref[...] = jnp.full_like(jst_ref, s_len)

        @pl.when(any_row(clo_ref[...] > kf) > 0)
        def _():
            thr = thr_ref[...]
            need = kf - count(lambda x, tb, kp: x > tb, thr)
            excess = clo_ref[...] > kf

            def tie_body(bi, j):
                cand = j + lax.shift_left(jnp.int32(1), (s_len.bit_length() - 1) - bi)
                cb = jnp.broadcast_to(cand, (QB, LANE))
                tb = jnp.broadcast_to(thr, (QB, LANE))
                acc = fold_chunks(
                    lambda a, x, kp: a + jnp.where(jnp.logical_and(x == tb, kp < cb), 1.0, 0.0),
                    jnp.zeros((QB, LANE), jnp.float32))
                below = jnp.sum(acc, axis=-1, keepdims=True)
                return jnp.where(below < need, cand, j)

            j = lax.fori_loop(0, s_len.bit_length(), tie_body, jnp.zeros((QB, 1), jnp.int32))
            jst_ref[...] = jnp.where(excess, j, s_len)

    qlat = []
    for h in range(HEADS):
        hs = slice(h * HEAD_DIM, (h + 1) * HEAD_DIM)
        ql = lax.dot_general(q_ref[:, hs], wuk_ref[h], _NT, preferred_element_type=jnp.float32)
        qlat.append((ql * HEAD_DIM ** -0.5).astype(jnp.bfloat16))
    qlat = jnp.concatenate(qlat, axis=0)
    m_ref[...] = jnp.full_like(m_ref, NEG_BIG)
    l_ref[...] = jnp.zeros_like(l_ref)
    acc_ref[...] = jnp.zeros_like(acc_ref)
    thr = thr_ref[...]
    jst = jst_ref[...]

    def abody(c, _):
        off = pl.multiple_of(c * kc, kc)
        cn = cn_ref[pl.ds(off, kc), :]
        sc = sc_ref[c]
        kpos = off + lax.broadcasted_iota(jnp.int32, (1, kc), 1)
        sel = jnp.logical_or(sc > thr, jnp.logical_and(sc == thr, kpos <= jst))
        logits = lax.dot_general(qlat, cn, _NT, preferred_element_type=jnp.float32)
        for h in range(HEADS):
            rs = slice(h * QB, (h + 1) * QB)
            s = jnp.where(sel, logits[rs, :], NEG_BIG)
            m_old = m_ref[rs, :]
            m_new = jnp.maximum(m_old, jnp.max(s, axis=-1, keepdims=True))
            alpha = jnp.exp(m_old - m_new)
            p = jnp.exp(s - m_new)
            l_ref[rs, :] = alpha * l_ref[rs, :] + jnp.sum(p, axis=-1, keepdims=True)
            pv = jnp.dot(p.astype(jnp.bfloat16), cn, preferred_element_type=jnp.float32)
            acc_ref[rs, :] = alpha * acc_ref[rs, :] + pv
            m_ref[rs, :] = m_new
        return 0

    lax.fori_loop(0, nch, abody, 0)

    for h in range(HEADS):
        hs = slice(h * HEAD_DIM, (h + 1) * HEAD_DIM)
        rs = slice(h * QB, (h + 1) * QB)
        o_lat = (acc_ref[rs, :] / l_ref[rs, :]).astype(jnp.bfloat16)
        o = jnp.dot(o_lat, wuv_ref[h], preferred_element_type=jnp.float32)
        o_ref[:, hs] = (o * _silu(z_ref[:, hs])).astype(o_ref.dtype)


def _dsa(p16b, p32b, kv_g, w_uk, w_uv):
    b, s, _ = p16b.shape
    kc = min(512, s)
    ksel = min(TOPK_MAX, s // 4)
    assert ksel % QB == 0 and s % kc == 0
    col = lambda: pltpu.VMEM((QB, 1), jnp.float32)
    full3 = pl.BlockSpec((HEADS, LATENT, HEAD_DIM), lambda bi, i: (0, 0, 0))
    return pl.pallas_call(
        functools.partial(_dsa_kernel, s_len=s, kc=kc, ksel=ksel, n_bisect=24),
        grid=(b, s // QB),
        in_specs=[pl.BlockSpec((None, QB, W), lambda bi, i: (bi, i, P16_BQ)),
                  pl.BlockSpec((None, QB, W), lambda bi, i: (bi, i, P16_QI)),
                  pl.BlockSpec((None, QB, LANE), lambda bi, i: (bi, i, P32_SM)),
                  pl.BlockSpec((None, QB, W), lambda bi, i: (bi, i, P32_BZ)),
                  pl.BlockSpec((None, s, LANE), lambda bi, i: (bi, 0, P16_KI)),
                  pl.BlockSpec((None, s, LANE), lambda bi, i: (bi, 0, P32_BC)),
                  pl.BlockSpec((1, LATENT), lambda bi, i: (0, 0)),
                  full3, full3],
        out_specs=pl.BlockSpec((None, QB, W), lambda bi, i: (bi, i, 0)),
        out_shape=jax.ShapeDtypeStruct((b, s, W), jnp.bfloat16),
        scratch_shapes=[pltpu.VMEM((s, LATENT), jnp.bfloat16),
                        pltpu.VMEM((s // kc, QB, kc), jnp.float32),
                        col(),
                        pltpu.VMEM((QB, 1), jnp.int32),
                        col(), col(), col(), col(),
                        pltpu.VMEM((HEADS * QB, 1), jnp.float32),
                        pltpu.VMEM((HEADS * QB, 1), jnp.float32),
                        pltpu.VMEM((HEADS * QB, LATENT), jnp.float32)],
        compiler_params=_cparams(("arbitrary", "arbitrary")),
        name="dsa_attn",
    )(p16b, p16b, p32b, p32b, p16b, p32b, kv_g.reshape(1, LATENT),
      w_uk.astype(jnp.bfloat16), w_uv.astype(jnp.bfloat16))


def _dsat_kernel(q_ref, qi_ref, sm_ref, z_ref, ki_ref, c_ref, kvg_ref, wuk_ref, wuv_ref, o_ref,
                 cn_ref, cnt_ref, sc_ref, scb_ref, qia_ref, qlat_ref, lga_ref, lgb_ref, bias_ref,
                 thr_ref, jst_ref, lo_ref, hib_ref, hi_ref, clo_ref,
                 m_ref, l_ref, acc_ref, *, s_len, kc, ksel, n_coarse, n_bisect):
    i = pl.program_id(1)
    ndiag = (i * QBE) // kc
    nch = ndiag + 1
    qpos = i * QBE + lax.broadcasted_iota(jnp.int32, (1, QBE), 1)
    kf = float(ksel)
    inf = float("inf")

    def key_pos(c):
        return c * kc + lax.broadcasted_iota(jnp.int32, (kc, QBE), 0)

    @pl.when(i == 0)
    def _():
        def nbody(c, _):
            off = pl.multiple_of(c * kc, kc)
            x = c_ref[pl.ds(off, kc), :]
            y = x * lax.rsqrt(jnp.mean(x * x, axis=-1, keepdims=True) + EPS) * kvg_ref[...]
            cn_ref[pl.ds(off, kc), :] = y.astype(cn_ref.dtype)
            cnt_ref[c] = jnp.transpose(y).astype(cnt_ref.dtype)
            return 0
        lax.fori_loop(0, s_len // kc, nbody, 0)

    wit = jnp.transpose(sm_ref[...])[HEADS:HEADS + IDX_HEADS, :] * (IDX_HEADS ** -0.5 * IDX_DIM ** -0.5)
    for h in range(IDX_HEADS):
        qia_ref[h * QBE:(h + 1) * QBE, :] = qi_ref[:, h * LANE:(h + 1) * LANE]

    def dots_into(buf_ref, c):
        off = pl.multiple_of(c * kc, kc)
        buf_ref[...] = lax.dot_general(ki_ref[pl.ds(off, kc), :], qia_ref[...], _NT,
                                       preferred_element_type=jnp.float32)

    def scores_from(buf_ref, c):
        sc = None
        for h in range(IDX_HEADS):
            term = wit[h:h + 1, :] * jnp.maximum(buf_ref[:, h * QBE:(h + 1) * QBE], 0.0)
            sc = term if sc is None else sc + term
        sc_ref[c] = sc
        scb_ref[c] = sc.astype(scb_ref.dtype)

    def run_pairs(produce, consume):
        produce(lga_ref, 0)

        def pair_body(j, _):
            produce(lgb_ref, 2 * j + 1)
            consume(lga_ref, 2 * j)
            produce(lga_ref, jnp.minimum(2 * j + 2, ndiag))
            consume(lgb_ref, 2 * j + 1)
            return 0

        lax.fori_loop(0, nch // 2, pair_body, 0)

        @pl.when(nch % 2 == 1)
        def _():
            consume(lga_ref, ndiag)

    run_pairs(dots_into, scores_from)
    diag = jnp.where(key_pos(ndiag) <= qpos, sc_ref[ndiag], -inf)
    sc_ref[ndiag] = diag
    scb_ref[ndiag] = diag.astype(scb_ref.dtype)

    def fold_slabs(op, fn, init_val):
        def cbody(c, acc):
            x = sc_ref[c]
            for j in range(kc // FOLD):
                acc = op(acc, fn(x[j * FOLD:(j + 1) * FOLD, :], c, j * FOLD))
            return acc
        return lax.fori_loop(0, nch, cbody, jnp.full((FOLD, QBE), init_val, jnp.float32))

    def count(pred, t):
        acc = fold_slabs(lambda a, m: jnp.where(m, a + 1.0, a), lambda x, c, r0: pred(x, t), 0.0)
        return jnp.sum(acc, axis=0, keepdims=True)

    def count_rounded(tb):
        one, zero = jnp.ones((), jnp.bfloat16), jnp.zeros((), jnp.bfloat16)

        def cbody(c, acc):
            x = scb_ref[c]
            for j in range(kc // FOLD):
                acc = acc + jnp.where(x[j * FOLD:(j + 1) * FOLD, :] >= tb, one, zero)
            return acc
        acc = lax.fori_loop(0, nch, cbody, jnp.zeros((FOLD, QBE), jnp.bfloat16))
        return jnp.sum(acc.astype(jnp.float32), axis=0, keepdims=True)

    def any_lane(flag):
        return (jnp.max(jnp.where(flag, 1.0, 0.0)) > 0.5).astype(jnp.int32)

    searched = qpos + 1 > ksel
    thr_ref[...] = jnp.full_like(thr_ref, -inf)
    jst_ref[...] = jnp.full_like(jst_ref, -1)

    @pl.when((i + 1) * QBE > ksel)
    def _():
        mn = fold_slabs(jnp.minimum, lambda x, c, r0: jnp.where(x == -inf, inf, x), inf)
        mx = fold_slabs(jnp.maximum, lambda x, c, r0: x, -inf)
        lo_ref[...] = jnp.min(mn, axis=0, keepdims=True)
        hib_ref[...] = jnp.max(mx, axis=0, keepdims=True)
        hi_ref[...] = jnp.full_like(hi_ref, inf)
        clo_ref[...] = (qpos + 1).astype(jnp.float32)

        def coarse_body(_, carry):
            lo, hib = lo_ref[...], hib_ref[...]
            tb = (0.5 * lo + 0.5 * hib).astype(jnp.bfloat16)
            t = tb.astype(jnp.float32)
            ge = count_rounded(tb) >= kf
            lo_ref[...] = jnp.where(ge, jnp.maximum(lo, t - (jnp.abs(t) * 2.0 ** -6 + 1e-30)), lo)
            hib_ref[...] = jnp.where(ge, hib, jnp.minimum(hib, t))
            hi_ref[...] = jnp.where(ge, hi_ref[...], jnp.minimum(hi_ref[...], t))
            return carry

        lax.fori_loop(0, n_coarse, coarse_body, 0)
        clo_ref[...] = jnp.full_like(clo_ref, -1.0)

        def bis_body(_, carry):
            lo, hib = lo_ref[...], hib_ref[...]
            mid = 0.5 * lo + 0.5 * hib
            cnt = count(lambda x, t: x >= t, mid)
            ge = cnt >= kf
            lo_ref[...] = jnp.where(ge, mid, lo)
            clo_ref[...] = jnp.where(ge, cnt, clo_ref[...])
            hib_ref[...] = jnp.where(ge, hib, mid)
            hi_ref[...] = jnp.where(ge, hi_ref[...], mid)
            return carry

        lax.fori_loop(0, n_bisect, bis_body, 0)

        resolved = jnp.logical_or(clo_ref[...] == kf, jnp.logical_not(searched))
        thr_ref[...] = jnp.where(resolved, lo_ref[...], inf)
        clo_ref[...] = jnp.where(resolved, kf, 0.0)

        def peel_body(go):
            done = clo_ref[...] >= kf
            hi = hi_ref[...]
            nxt = fold_slabs(jnp.maximum, lambda x, c, r0: jnp.where(x < hi, x, -inf), -inf)
            t = jnp.where(done, thr_ref[...], jnp.max(nxt, axis=0, keepdims=True))
            cnt = count(lambda x, tt: x >= tt, t)
            thr_ref[...] = t
            clo_ref[...] = jnp.where(done, clo_ref[...], cnt)
            hi_ref[...] = jnp.where(done, hi, t)
            return any_lane(jnp.logical_and(jnp.logical_not(done), cnt < kf))

        lax.while_loop(lambda go: go > 0, peel_body, any_lane(clo_ref[...] < kf))

        jst_ref[...] = jnp.where(searched, s_len, -1)
        thr_ref[...] = jnp.where(searched, thr_ref[...], -inf)
        excess = jnp.logical_and(searched, clo_ref[...] > kf)

        @pl.when(any_lane(excess) > 0)
        def _():
            thr = thr_ref[...]
            surplus = clo_ref[...] - kf
            half = kc // 2
            triu = jnp.where(lax.broadcasted_iota(jnp.int32, (half, half), 0) <=
                             lax.broadcasted_iota(jnp.int32, (half, half), 1), 1.0, 0.0).astype(jnp.bfloat16)

            def tie_body(k, carry):
                after, last = carry
                c = nch - 1 - k
                tie = sc_ref[c] == thr
                t01 = jnp.where(tie, 1.0, 0.0).astype(jnp.bfloat16)
                r_bot = jnp.dot(triu, t01[half:, :], preferred_element_type=jnp.float32) + after
                r_top = jnp.dot(triu, t01[:half, :], preferred_element_type=jnp.float32) + r_bot[0:1, :]
                keep = jnp.logical_and(tie, jnp.concatenate([r_top, r_bot], axis=0) > surplus)
                kept_pos = jnp.where(keep, key_pos(c).astype(jnp.float32), -1.0)
                return r_top[0:1, :], jnp.maximum(last, jnp.max(kept_pos, axis=0, keepdims=True))

            _, last = lax.fori_loop(0, nch, tie_body, (jnp.zeros((1, QBE), jnp.float32),
                                                       jnp.full((1, QBE), -1.0, jnp.float32)))
            jst_ref[...] = jnp.where(excess, last.astype(jnp.int32), jst_ref[...])

    for h in range(HEADS):
        hs = slice(h * HEAD_DIM, (h + 1) * HEAD_DIM)
        ql = lax.dot_general(q_ref[:, hs], wuk_ref[h], _NT, preferred_element_type=jnp.float32)
        qlat_ref[h * QBE:(h + 1) * QBE, :] = (ql * (HEAD_DIM ** -0.5 * LOG2E)).astype(qlat_ref.dtype)
    m_ref[...] = jnp.full_like(m_ref, NEG_BIG)
    l_ref[...] = jnp.zeros_like(l_ref)
    acc_ref[...] = jnp.zeros_like(acc_ref)
    thr = thr_ref[...]
    jst = jst_ref[...]

    def logits_into(buf_ref, c):
        off = pl.multiple_of(c * kc, kc)
        buf_ref[...] = lax.dot_general(cn_ref[pl.ds(off, kc), :], qlat_ref[...], _NT,
                                       preferred_element_type=jnp.float32)

    def softmax_pv(buf_ref, c):
        sc = sc_ref[c]
        sel = jnp.logical_or(sc > thr, jnp.logical_and(sc == thr, key_pos(c) <= jst))
        bias_ref[...] = jnp.where(sel, 0.0, NEG_BIG)
        for h in range(HEADS):
            qs = slice(h * QBE, (h + 1) * QBE)
            st = buf_ref[:, qs] + bias_ref[...]
            m_old = m_ref[:, qs]
            m_new = jnp.maximum(m_old, jnp.max(st, axis=0, keepdims=True))
            alpha = jnp.exp2(m_old - m_new)
            p = jnp.exp2(st - m_new)
            l_ref[:, qs] = alpha * l_ref[:, qs] + jnp.sum(p, axis=0, keepdims=True)
            pv = jnp.dot(cnt_ref[c], p.astype(jnp.bfloat16), preferred_element_type=jnp.float32)
            acc_ref[:, qs] = alpha * acc_ref[:, qs] + pv
            m_ref[:, qs] = m_new

    run_pairs(logits_into, softmax_pv)

    for h in range(HEADS):
        hs = slice(h * HEAD_DIM, (h + 1) * HEAD_DIM)
        qs = slice(h * QBE, (h + 1) * QBE)
        o_lat = jnp.transpose(acc_ref[:, qs] / l_ref[:, qs]).astype(jnp.bfloat16)
        o = jnp.dot(o_lat, wuv_ref[h], preferred_element_type=jnp.float32)
        o_ref[:, hs] = (o * _silu(z_ref[:, hs])).astype(o_ref.dtype)


def _dsat(p16b, p32b, kv_g, w_uk, w_uv):
    b, s, _ = p16b.shape
    kc = min(KC, s)
    ksel = min(TOPK_MAX, s // 4)
    assert s % kc == 0 and kc % QBE == 0
    assert s // FOLD <= 256
    nq = HEADS * QBE
    row = lambda n: pltpu.VMEM((1, n), jnp.float32)
    full3 = pl.BlockSpec((HEADS, LATENT, HEAD_DIM), lambda bi, i: (0, 0, 0))
    return pl.pallas_call(
        functools.partial(_dsat_kernel, s_len=s, kc=kc, ksel=ksel, n_coarse=8, n_bisect=9),
        grid=(b, s // QBE),
        in_specs=[pl.BlockSpec((None, QBE, W), lambda bi, i: (bi, i, P16_BQ)),
                  pl.BlockSpec((None, QBE, W), lambda bi, i: (bi, i, P16_QI)),
                  pl.BlockSpec((None, QBE, LANE), lambda bi, i: (bi, i, P32_SM)),
                  pl.BlockSpec((None, QBE, W), lambda bi, i: (bi, i, P32_BZ)),
                  pl.BlockSpec((None, s, LANE), lambda bi, i: (bi, 0, P16_KI)),
                  pl.BlockSpec((None, s, LANE), lambda bi, i: (bi, 0, P32_BC)),
                  pl.BlockSpec((1, LATENT), lambda bi, i: (0, 0)),
                  full3, full3],
        out_specs=pl.BlockSpec((None, QBE, W), lambda bi, i: (bi, i, 0)),
        out_shape=jax.ShapeDtypeStruct((b, s, W), jnp.bfloat16),
        scratch_shapes=[pltpu.VMEM((s, LATENT), jnp.bfloat16),
                        pltpu.VMEM((s // kc, LATENT, kc), jnp.bfloat16),
                        pltpu.VMEM((s // kc, kc, QBE), jnp.float32),
                        pltpu.VMEM((s // kc, kc, QBE), jnp.bfloat16),
                        pltpu.VMEM((IDX_HEADS * QBE, LANE), jnp.bfloat16),
                        pltpu.VMEM((nq, LATENT), jnp.bfloat16),
                        pltpu.VMEM((kc, nq), jnp.float32),
                        pltpu.VMEM((kc, nq), jnp.float32),
                        pltpu.VMEM((kc, QBE), jnp.float32),
                        row(QBE),
                        pltpu.VMEM((1, QBE), jnp.int32),
                        row(QBE), row(QBE), row(QBE), row(QBE),
                        row(nq), row(nq),
                        pltpu.VMEM((LATENT, nq), jnp.float32)],
        compiler_params=_cparams(("arbitrary", "arbitrary")),
        name="dsa_attn",
    )(p16b, p16b, p32b, p32b, p16b, p32b, kv_g.reshape(1, LATENT),
      w_uk.astype(jnp.bfloat16), w_uv.astype(jnp.bfloat16))


def _merge_kernel(au_ref, av_ref, az_ref, cb_ref, cc_ref, cx_ref, cz_ref,
                  lng_ref, lnb_ref, ws_ref, bs_ref, cw_ref,
                  yb_ref, yd_ref, hn_in_ref, x_ref, wg_ref, wb_ref, wo_ref, gn_ref,
                  h_ref, hn_ref, ya_ref, yc_ref, halo_ref, *, tm, tiles_per_seq):
    first_tile = pl.program_id(0) % tiles_per_seq == 0
    _mix_ac_tile(au_ref, av_ref, az_ref, cb_ref, cc_ref, cx_ref, cz_ref,
                 lng_ref, lnb_ref, ws_ref, bs_ref, cw_ref, ya_ref, yc_ref, halo_ref, first_tile, tm)
    hn_in = hn_in_ref[...]
    merged = None
    for n, y_ref in enumerate((ya_ref, yb_ref, yc_ref, yd_ref)):
        cols = slice(n * D_MODEL, (n + 1) * D_MODEL)
        gate = _sigmoid(jnp.dot(hn_in, wg_ref[:, cols], preferred_element_type=jnp.float32))
        term = gate * jnp.dot(y_ref[...], wb_ref[n], preferred_element_type=jnp.float32)
        merged = term if merged is None else merged + term
    h = x_ref[...] + jnp.dot(merged.astype(jnp.bfloat16), wo_ref[...], preferred_element_type=jnp.float32)
    h_ref[...] = h
    hn = h * lax.rsqrt(jnp.mean(h * h, axis=-1, keepdims=True) + EPS) * gn_ref[...]
    hn_ref[...] = hn.astype(hn_ref.dtype)


def _merge(p32, yb, yd, hn, h, ln_g, ln_b, w_s, b_s, conv_w, wg, wb, wo, g_next, hn_dtype, s_len, tm=256):
    m, d = h.shape
    assert s_len % tm == 0 and tm % GM_CHUNK == 0
    pblk = lambda idx: pl.BlockSpec((tm, W), lambda i, idx=idx: (i, idx))
    yblk = pl.BlockSpec((tm, W), lambda i: (i, 0))
    hblk = pl.BlockSpec((tm, d), lambda i: (i, 0))
    const = lambda shp: pl.BlockSpec(shp, lambda i: (0,) * len(shp))
    bs_full = jnp.repeat(jnp.transpose(b_s), LANE, axis=1)
    return pl.pallas_call(
        functools.partial(_merge_kernel, tm=tm, tiles_per_seq=s_len // tm),
        grid=(m // tm,),
        in_specs=[pblk(P32_AU), pblk(P32_AV), pblk(P32_AZ), pblk(P32_CB), pblk(P32_CC), pblk(P32_CX), pblk(P32_CZ),
                  const((1, W)), const((1, W)), const((GM_GROUPS, GM_CHUNK, GM_CHUNK)),
                  const((GM_CHUNK, W)), const((CONV_WIDTH, W)),
                  yblk, yblk, hblk, hblk,
                  const((d, N_BRANCH * d)), const((N_BRANCH, W, d)), const((d, d)), const((1, d))],
        out_specs=[hblk, hblk],
        out_shape=[jax.ShapeDtypeStruct((m, d), jnp.float32), jax.ShapeDtypeStruct((m, d), hn_dtype)],
        scratch_shapes=[pltpu.VMEM((tm, W), jnp.bfloat16), pltpu.VMEM((tm, W), jnp.bfloat16),
                        pltpu.VMEM((8, W), jnp.float32)],
        compiler_params=_cparams(("arbitrary",)),
        name="merge",
    )(p32, p32, p32, p32, p32, p32, p32,
      ln_g.reshape(1, W), ln_b.reshape(1, W), w_s, bs_full, conv_w,
      yb, yd, hn, h, wg, wb, wo, g_next.reshape(1, d))


def kernel(x, norm_g, w_in, gm_ln_g, gm_ln_b, gm_w_s, gm_b_s, dsa_kv_g, dsa_w_uk, dsa_w_uv,
           conv_w, fox_b_f, w_branch, w_out, final_g):
    b, s, d = x.shape
    depth = w_in.shape[0]
    m = b * s
    h = x.reshape(m, d)
    hn = _rmsnorm(h, norm_g[0], jnp.bfloat16)
    for l in range(depth):
        w16, w32, wvt, wg = _prep_w_in(w_in[l])
        p16 = _matmul(hn, w16, jnp.bfloat16, tm=512, tn=N16, name="in_proj16")
        p32 = _matmul(hn, w32, jnp.float32, tm=512, tn=N32 // 2, name="in_proj32")
        vt = _matmul_t(hn, wvt, tm=min(KC, s), name="in_proj_vt")
        p16b = p16.reshape(b, s, N16)
        p32b = p32.reshape(b, s, N32)
        qaug, kaug = _fox_cum(p32b, fox_b_f[l])
        yd = _fox(p16b, p32b, qaug, kaug, vt)
        yb = _dsat(p16b, p32b, dsa_kv_g[l], dsa_w_uk[l], dsa_w_uv[l])
        last = l == depth - 1
        g_next = final_g if last else norm_g[l + 1]
        h, hn = _merge(p32, yb.reshape(m, W), yd.reshape(m, W), hn, h,
                       gm_ln_g[l], gm_ln_b[l], gm_w_s[l], gm_b_s[l], conv_w[l],
                       wg, w_branch[l].astype(jnp.bfloat16), w_out[l].astype(jnp.bfloat16), g_next,
                       jnp.float32 if last else jnp.bfloat16, s)
    return hn.reshape(b, s, d)
```

```python
import functools

import numpy as np
import jax
import jax.numpy as jnp
from jax import lax
from jax.experimental import pallas as pl
from jax.experimental.pallas import tpu as pltpu

D_MODEL = 1024
N_BRANCH = 4
W = 512
EPS = 1e-6
QBE = 256
KC = 512
FOLD = 64
CUM_ROWS = 512
GM_GROUPS = 4
GM_CHUNK = 128
HEADS = 4
HEAD_DIM = W // HEADS
LATENT = 128
IDX_HEADS = 4
IDX_DIM = 64
TOPK_MAX = 256
CONV_WIDTH = 3
LANE = 128
NEG_BIG = -1e30
LOG2E = 1.4426950408889634
VMEM_LIMIT = 56 * 1024 * 1024

P16_DQ, P16_DK, P16_BQ = 0, 1, 2
P16_QI = 3
P16_KI = 16
N16 = 17 * LANE
P32_AU, P32_AV, P32_AZ, P32_BZ = 0, 1, 2, 3
P32_CB, P32_CC, P32_CX, P32_CZ, P32_DZ = 4, 5, 6, 7, 8
P32_BC = 36
P32_SM = 37
N32 = 38 * LANE

_NT = (((1,), (1,)), ((), ()))


def _cparams(sem):
    return pltpu.CompilerParams(dimension_semantics=sem, vmem_limit_bytes=VMEM_LIMIT)


def _sigmoid(z):
    return 1.0 / (1.0 + jnp.exp(-z))


def _silu(z):
    return z * _sigmoid(z)


def _rmsnorm_kernel(x_ref, g_ref, o_ref):
    x = x_ref[...]
    y = x * lax.rsqrt(jnp.mean(x * x, axis=-1, keepdims=True) + EPS)
    o_ref[...] = (y * g_ref[...]).astype(o_ref.dtype)


def _rmsnorm(x2, g, out_dtype, tm=512):
    m, d = x2.shape
    return pl.pallas_call(
        _rmsnorm_kernel,
        grid=(m // tm,),
        in_specs=[pl.BlockSpec((tm, d), lambda i: (i, 0)), pl.BlockSpec((1, d), lambda i: (0, 0))],
        out_specs=pl.BlockSpec((tm, d), lambda i: (i, 0)),
        out_shape=jax.ShapeDtypeStruct((m, d), out_dtype),
        compiler_params=_cparams(("arbitrary",)),
        name="rmsnorm",
    )(x2, g.reshape(1, d))


def _matmul_kernel(a_ref, w_ref, o_ref):
    o_ref[...] = jnp.dot(a_ref[...], w_ref[...], preferred_element_type=jnp.float32).astype(o_ref.dtype)


def _matmul(a, w, out_dtype, tm, tn, name):
    m, k = a.shape
    n = w.shape[1]
    return pl.pallas_call(
        _matmul_kernel,
        grid=(n // tn, m // tm),
        in_specs=[pl.BlockSpec((tm, k), lambda j, i: (i, 0)), pl.BlockSpec((k, tn), lambda j, i: (0, j))],
        out_specs=pl.BlockSpec((tm, tn), lambda j, i: (i, j)),
        out_shape=jax.ShapeDtypeStruct((m, n), out_dtype),
        compiler_params=_cparams(("arbitrary", "arbitrary")),
        name=name,
    )(a, w)


def _prep_w_in(w):
    d = w.shape[0]
    w = w.astype(jnp.bfloat16)
    sizes = (W, W, W,
             W, LATENT, IDX_HEADS * IDX_DIM, IDX_DIM, IDX_HEADS, W,
             W, W, W, W,
             W, W, W, HEADS, W,
             N_BRANCH * D_MODEL)
    parts, off = [], 0
    for s in sizes:
        parts.append(w[:, off:off + s])
        off += s
    (a_u, a_v, a_z, b_q, b_c, b_qi, b_ki, b_wi, b_z,
     c_b, c_c, c_x, c_z, d_q, d_k, d_v, d_f, d_z, gates) = parts
    zeros = lambda n: jnp.zeros((d, n), w.dtype)
    qi = []
    for h in range(IDX_HEADS):
        qi += [b_qi[:, h * IDX_DIM:(h + 1) * IDX_DIM], zeros(LANE - IDX_DIM)]
    w16 = jnp.concatenate([d_q, d_k, b_q] + qi + [b_ki, zeros(LANE - IDX_DIM)], axis=1)
    w32 = jnp.concatenate([a_u, a_v, a_z, b_z, c_b, c_c, c_x, c_z, d_z, b_c,
                           d_f, b_wi, zeros(LANE - HEADS - IDX_HEADS)], axis=1)
    return w16, w32, jnp.transpose(d_v), gates


def _matmul_t_kernel(wt_ref, a_ref, o_ref):
    o_ref[...] = lax.dot_general(wt_ref[...], a_ref[...], _NT,
                                 preferred_element_type=jnp.float32).astype(o_ref.dtype)


def _matmul_t(a, wt, tm, name):
    m, k = a.shape
    n = wt.shape[0]
    return pl.pallas_call(
        _matmul_t_kernel,
        grid=(m // tm,),
        in_specs=[pl.BlockSpec((n, k), lambda i: (0, 0)), pl.BlockSpec((tm, k), lambda i: (i, 0))],
        out_specs=pl.BlockSpec((None, n, tm), lambda i: (i, 0, 0)),
        out_shape=jax.ShapeDtypeStruct((m // tm, n, tm), jnp.bfloat16),
        compiler_params=_cparams(("arbitrary",)),
        name=name,
    )(wt, a)


def _mix_ac_tile(au_ref, av_ref, az_ref, cb_ref, cc_ref, cx_ref, cz_ref,
                 lng_ref, lnb_ref, ws_ref, bs_ref, cw_ref, ya_ref, yc_ref, halo_ref, first_tile, tt):
    v = av_ref[...]
    mu = jnp.mean(v, axis=-1, keepdims=True)
    vc = v - mu
    var = jnp.mean(vc * vc, axis=-1, keepdims=True)
    vn = (vc * lax.rsqrt(var + EPS) * lng_ref[...] + lnb_ref[...]).astype(jnp.bfloat16)
    row = lax.broadcasted_iota(jnp.int32, (GM_CHUNK, GM_CHUNK), 0)
    col = lax.broadcasted_iota(jnp.int32, (GM_CHUNK, GM_CHUNK), 1)
    tril = row >= col
    wg = [jnp.where(tril, ws_ref[g], 0.0).astype(jnp.bfloat16) for g in range(GM_GROUPS)]
    for ch in range(tt // GM_CHUNK):
        rows = slice(ch * GM_CHUNK, (ch + 1) * GM_CHUNK)
        for g in range(GM_GROUPS):
            cols = slice(g * LANE, (g + 1) * LANE)
            mixed = jnp.dot(wg[g], vn[rows, cols], preferred_element_type=jnp.float32) + bs_ref[:, cols]
            ya_ref[rows, cols] = (au_ref[rows, cols] * mixed * _silu(az_ref[rows, cols])).astype(ya_ref.dtype)

    @pl.when(first_tile)
    def _():
        halo_ref[...] = jnp.zeros_like(halo_ref)

    y = cc_ref[...] * cx_ref[...]
    ext = jnp.concatenate([halo_ref[...], y], axis=0)
    conv = cw_ref[2:3, :] * y
    for j in range(CONV_WIDTH - 1):
        shift = CONV_WIDTH - 1 - j
        conv = conv + cw_ref[j:j + 1, :] * ext[8 - shift:8 - shift + tt, :]
    yc_ref[...] = (cb_ref[...] * conv * _silu(cz_ref[...])).astype(yc_ref.dtype)
    halo_ref[...] = y[tt - 8:, :]


def _split3(x):
    hi = x.astype(jnp.bfloat16)
    r1 = x - hi.astype(jnp.float32)
    mid = r1.astype(jnp.bfloat16)
    lo = (r1 - mid.astype(jnp.float32)).astype(jnp.bfloat16)
    return hi, mid, lo


def _aug_placement():
    pq = np.zeros((3 * LANE, W), np.float32)
    pk = np.zeros((3 * LANE, W), np.float32)
    cq = np.zeros((1, W), np.float32)
    ck = np.zeros((1, W), np.float32)
    for h in range(HEADS):
        for j in range(3):
            pq[j * LANE + h, h * HEAD_DIM + j] = 1.0
            cq[0, h * HEAD_DIM + 3 + j] = 1.0
            pk[j * LANE + h, h * HEAD_DIM + 3 + j] = -1.0
            ck[0, h * HEAD_DIM + j] = 1.0
    return (jnp.asarray(pq, jnp.bfloat16), jnp.asarray(pk, jnp.bfloat16), jnp.asarray(cq), jnp.asarray(ck))


def _fox_cum_kernel(f_ref, bias_ref, pq_ref, pk_ref, cq_ref, ck_ref, qa_ref, ka_ref, carry_ref, *, s):
    rows = min(CUM_ROWS, s)
    row = lax.broadcasted_iota(jnp.int32, (rows, rows), 0)
    col = lax.broadcasted_iota(jnp.int32, (rows, rows), 1)
    ones_tril = jnp.where(row >= col, 1.0, 0.0).astype(jnp.bfloat16)
    carry_ref[...] = jnp.zeros_like(carry_ref)

    def body(c, _):
        off = pl.multiple_of(c * rows, rows)
        x = f_ref[pl.ds(off, rows), :] + bias_ref[...]
        ls = jnp.minimum(x, 0.0) - jnp.log1p(jnp.exp(-jnp.abs(x)))
        hi, mid, lo = _split3(ls * LOG2E)
        dot = lambda p: jnp.dot(ones_tril, p, preferred_element_type=jnp.float32)
        cs = (dot(hi) + dot(mid)) + dot(lo) + carry_ref[0:1, :]
        carry_ref[0:1, :] = cs[rows - 1:rows, :]
        parts = jnp.concatenate(_split3(cs), axis=1)
        qa = jnp.dot(parts, pq_ref[...], preferred_element_type=jnp.float32) + cq_ref[...]
        ka = jnp.dot(parts, pk_ref[...], preferred_element_type=jnp.float32) + ck_ref[...]
        qa_ref[pl.ds(off, rows), :] = qa.astype(qa_ref.dtype)
        ka_ref[pl.ds(off, rows), :] = ka.astype(ka_ref.dtype)
        return 0

    lax.fori_loop(0, s // rows, body, 0)


def _fox_cum(p32b, b_f):
    b, s, _ = p32b.shape
    bias = jnp.zeros((1, LANE), jnp.float32).at[0, :HEADS].set(b_f)
    pq, pk, cq, ck = _aug_placement()
    const = lambda shp: pl.BlockSpec(shp, lambda bi: (0, 0))
    out = jax.ShapeDtypeStruct((b, s, W), jnp.bfloat16)
    return pl.pallas_call(
        functools.partial(_fox_cum_kernel, s=s),
        grid=(b,),
        in_specs=[pl.BlockSpec((None, s, LANE), lambda bi: (bi, 0, P32_SM)), const((1, LANE)),
                  const((3 * LANE, W)), const((3 * LANE, W)), const((1, W)), const((1, W))],
        out_specs=[pl.BlockSpec((None, s, W), lambda bi: (bi, 0, 0))] * 2,
        out_shape=[out, out],
        scratch_shapes=[pltpu.VMEM((8, LANE), jnp.float32)],
        compiler_params=_cparams(("arbitrary",)),
        name="fox_cum",
    )(p32b, bias, pq, pk, cq, ck)


def _fox_kernel(q_ref, qa_ref, k_ref, ka_ref, vt_ref, z_ref, o_ref,
                qf_ref, sa_ref, sb_ref, m_ref, l_ref, acc_ref, *, kc):
    i = pl.program_id(1)
    ndiag = (i * QBE) // kc
    scale = HEAD_DIM ** -0.5 * LOG2E
    for h in range(HEADS):
        hs = slice(h * HEAD_DIM, (h + 1) * HEAD_DIM)
        qs = (q_ref[:, hs].astype(jnp.float32) * scale).astype(jnp.bfloat16)
        qf_ref[h] = jnp.concatenate([qs, qa_ref[:, hs]], axis=1)
    m_ref[...] = jnp.full_like(m_ref, NEG_BIG)
    l_ref[...] = jnp.zeros_like(l_ref)
    acc_ref[...] = jnp.zeros_like(acc_ref)

    def scores_into(buf_ref, c):
        off = pl.multiple_of(c * kc, kc)
        for h in range(HEADS):
            hs = slice(h * HEAD_DIM, (h + 1) * HEAD_DIM)
            kf = jnp.concatenate([k_ref[pl.ds(off, kc), hs], ka_ref[pl.ds(off, kc), hs]], axis=1)
            buf_ref[:, h * QBE:(h + 1) * QBE] = lax.dot_general(
                kf, qf_ref[h], _NT, preferred_element_type=jnp.float32)

    def softmax_pv(buf_ref, c, masked):
        if masked:
            causal = (c * kc + lax.broadcasted_iota(jnp.int32, (kc, QBE), 0) <=
                      i * QBE + lax.broadcasted_iota(jnp.int32, (kc, QBE), 1))
        for h in range(HEADS):
            hs = slice(h * HEAD_DIM, (h + 1) * HEAD_DIM)
            st = buf_ref[:, h * QBE:(h + 1) * QBE]
            if masked:
                st = jnp.where(causal, st, NEG_BIG)
            m_old = m_ref[h]
            m_new = jnp.maximum(m_old, jnp.max(st, axis=0, keepdims=True))
            alpha = jnp.exp2(m_old - m_new)
            p = jnp.exp2(st - m_new)
            l_ref[h] = alpha * l_ref[h] + jnp.sum(p, axis=0, keepdims=True)
            pv = jnp.dot(vt_ref[c, hs, :], p.astype(jnp.bfloat16), preferred_element_type=jnp.float32)
            acc_ref[h] = alpha * acc_ref[h] + pv
            m_ref[h] = m_new

    scores_into(sa_ref, 0)

    def pair_body(j, _):
        scores_into(sb_ref, 2 * j + 1)
        softmax_pv(sa_ref, 2 * j, False)
        scores_into(sa_ref, 2 * j + 2)
        softmax_pv(sb_ref, 2 * j + 1, False)
        return 0

    lax.fori_loop(0, ndiag // 2, pair_body, 0)

    @pl.when(ndiag % 2 == 1)
    def _():
        scores_into(sb_ref, ndiag)
        softmax_pv(sa_ref, ndiag - 1, False)
        softmax_pv(sb_ref, ndiag, True)

    @pl.when(ndiag % 2 == 0)
    def _():
        softmax_pv(sa_ref, ndiag, True)

    for h in range(HEADS):
        hs = slice(h * HEAD_DIM, (h + 1) * HEAD_DIM)
        o = jnp.transpose(acc_ref[h] / l_ref[h])
        o_ref[:, hs] = (o * _silu(z_ref[:, hs])).astype(o_ref.dtype)


def _fox(p16b, p32b, qaug, kaug, vt):
    b, s, _ = p16b.shape
    kc = vt.shape[-1]
    assert s % kc == 0 and kc % QBE == 0
    nck = s // kc
    vt = vt.reshape(b, nck, W, kc)
    row = lambda: pltpu.VMEM((HEADS, 1, QBE), jnp.float32)
    return pl.pallas_call(
        functools.partial(_fox_kernel, kc=kc),
        grid=(b, s // QBE),
        in_specs=[pl.BlockSpec((None, QBE, W), lambda bi, i: (bi, i, P16_DQ)),
                  pl.BlockSpec((None, QBE, W), lambda bi, i: (bi, i, 0)),
                  pl.BlockSpec((None, s, W), lambda bi, i: (bi, 0, P16_DK)),
                  pl.BlockSpec((None, s, W), lambda bi, i: (bi, 0, 0)),
                  pl.BlockSpec((None, nck, W, kc), lambda bi, i: (bi, 0, 0, 0)),
                  pl.BlockSpec((None, QBE, W), lambda bi, i: (bi, i, P32_DZ))],
        out_specs=pl.BlockSpec((None, QBE, W), lambda bi, i: (bi, i, 0)),
        out_shape=jax.ShapeDtypeStruct((b, s, W), jnp.bfloat16),
        scratch_shapes=[pltpu.VMEM((HEADS, QBE, 2 * HEAD_DIM), jnp.bfloat16),
                        pltpu.VMEM((kc, HEADS * QBE), jnp.float32),
                        pltpu.VMEM((kc, HEADS * QBE), jnp.float32),
                        row(), row(),
                        pltpu.VMEM((HEADS, HEAD_DIM, QBE), jnp.float32)],
        compiler_params=_cparams(("arbitrary", "arbitrary")),
        name="fox_attn",
    )(p16b, qaug, p16b, kaug, vt, p32b)


def _dsa_kernel(q_ref, qi_ref, sm_ref, z_ref, ki_ref, c_ref, kvg_ref, wuk_ref, wuv_ref, o_ref,
                 cn_ref, cnt_ref, sc_ref, scb_ref, qia_ref, qlat_ref, lga_ref, lgb_ref, bias_ref,
                 thr_ref, jst_ref, lo_ref, hib_ref, hi_ref, clo_ref,
                 m_ref, l_ref, acc_ref, *, s_len, kc, ksel, n_coarse, n_bisect):
    i = pl.program_id(1)
    ndiag = (i * QBE) // kc
    nch = ndiag + 1
    qpos = i * QBE + lax.broadcasted_iota(jnp.int32, (1, QBE), 1)
    kf = float(ksel)
    inf = float("inf")

    def key_pos(c):
        return c * kc + lax.broadcasted_iota(jnp.int32, (kc, QBE), 0)

    @pl.when(i == 0)
    def _():
        def nbody(c, _):
            off = pl.multiple_of(c * kc, kc)
            x = c_ref[pl.ds(off, kc), :]
            y = x * lax.rsqrt(jnp.mean(x * x, axis=-1, keepdims=True) + EPS) * kvg_ref[...]
            cn_ref[pl.ds(off, kc), :] = y.astype(cn_ref.dtype)
            cnt_ref[c] = jnp.transpose(y).astype(cnt_ref.dtype)
            return 0
        lax.fori_loop(0, s_len // kc, nbody, 0)

    wit = jnp.transpose(sm_ref[...])[HEADS:HEADS + IDX_HEADS, :] * (IDX_HEADS ** -0.5 * IDX_DIM ** -0.5)
    for h in range(IDX_HEADS):
        qia_ref[h * QBE:(h + 1) * QBE, :] = qi_ref[:, h * LANE:(h + 1) * LANE]

    def dots_into(buf_ref, c):
        off = pl.multiple_of(c * kc, kc)
        buf_ref[...] = lax.dot_general(ki_ref[pl.ds(off, kc), :], qia_ref[...], _NT,
                                       preferred_element_type=jnp.float32)

    def scores_from(buf_ref, c):
        sc = None
        for h in range(IDX_HEADS):
            term = wit[h:h + 1, :] * jnp.maximum(buf_ref[:, h * QBE:(h + 1) * QBE], 0.0)
            sc = term if sc is None else sc + term
        sc_ref[c] = sc
        scb_ref[c] = sc.astype(scb_ref.dtype)

    def run_pairs(produce, consume):
        produce(lga_ref, 0)

        def pair_body(j, _):
            produce(lgb_ref, 2 * j + 1)
            consume(lga_ref, 2 * j)
            produce(lga_ref, jnp.minimum(2 * j + 2, ndiag))
            consume(lgb_ref, 2 * j + 1)
            return 0

        lax.fori_loop(0, nch // 2, pair_body, 0)

        @pl.when(nch % 2 == 1)
        def _():
            consume(lga_ref, ndiag)

    run_pairs(dots_into, scores_from)
    diag = jnp.where(key_pos(ndiag) <= qpos, sc_ref[ndiag], -inf)
    sc_ref[ndiag] = diag
    scb_ref[ndiag] = diag.astype(scb_ref.dtype)

    def fold_slabs(op, fn, init_val):
        def cbody(c, acc):
            x = sc_ref[c]
            for j in range(kc // FOLD):
                acc = op(acc, fn(x[j * FOLD:(j + 1) * FOLD, :], c, j * FOLD))
            return acc
        return lax.fori_loop(0, nch, cbody, jnp.full((FOLD, QBE), init_val, jnp.float32))

    def count(pred, t):
        acc = fold_slabs(lambda a, m: jnp.where(m, a + 1.0, a), lambda x, c, r0: pred(x, t), 0.0)
        return jnp.sum(acc, axis=0, keepdims=True)

    def count_rounded(tb):
        one, zero = jnp.ones((), jnp.bfloat16), jnp.zeros((), jnp.bfloat16)

        def cbody(c, acc):
            x = scb_ref[c]
            for j in range(kc // FOLD):
                acc = acc + jnp.where(x[j * FOLD:(j + 1) * FOLD, :] >= tb, one, zero)
            return acc
        acc = lax.fori_loop(0, nch, cbody, jnp.zeros((FOLD, QBE), jnp.bfloat16))
        return jnp.sum(acc.astype(jnp.float32), axis=0, keepdims=True)

    def any_lane(flag):
        return (jnp.max(jnp.where(flag, 1.0, 0.0)) > 0.5).astype(jnp.int32)

    searched = qpos + 1 > ksel
    thr_ref[...] = jnp.full_like(thr_ref, -inf)
    jst_ref[...] = jnp.full_like(jst_ref, -1)

    @pl.when((i + 1) * QBE > ksel)
    def _():
        mn = fold_slabs(jnp.minimum, lambda x, c, r0: jnp.where(x == -inf, inf, x), inf)
        mx = fold_slabs(jnp.maximum, lambda x, c, r0: x, -inf)
        lo_ref[...] = jnp.min(mn, axis=0, keepdims=True)
        hib_ref[...] = jnp.max(mx, axis=0, keepdims=True)
        hi_ref[...] = jnp.full_like(hi_ref, inf)
        clo_ref[...] = (qpos + 1).astype(jnp.float32)

        def coarse_body(_, carry):
            lo, hib = lo_ref[...], hib_ref[...]
            tb = (0.5 * lo + 0.5 * hib).astype(jnp.bfloat16)
            t = tb.astype(jnp.float32)
            ge = count_rounded(tb) >= kf
            lo_ref[...] = jnp.where(ge, jnp.maximum(lo, t - (jnp.abs(t) * 2.0 ** -6 + 1e-30)), lo)
            hib_ref[...] = jnp.where(ge, hib, jnp.minimum(hib, t))
            hi_ref[...] = jnp.where(ge, hi_ref[...], jnp.minimum(hi_ref[...], t))
            return carry

        lax.fori_loop(0, n_coarse, coarse_body, 0)
        clo_ref[...] = jnp.full_like(clo_ref, -1.0)

        def bis_body(_, carry):
            lo, hib = lo_ref[...], hib_ref[...]
            mid = 0.5 * lo + 0.5 * hib
            cnt = count(lambda x, t: x >= t, mid)
            ge = cnt >= kf
            lo_ref[...] = jnp.where(ge, mid, lo)
            clo_ref[...] = jnp.where(ge, cnt, clo_ref[...])
            hib_ref[...] = jnp.where(ge, hib, mid)
            hi_ref[...] = jnp.where(ge, hi_ref[...], mid)
            return carry

        lax.fori_loop(0, n_bisect, bis_body, 0)

        resolved = jnp.logical_or(clo_ref[...] == kf, jnp.logical_not(searched))
        thr_ref[...] = jnp.where(resolved, lo_ref[...], inf)
        clo_ref[...] = jnp.where(resolved, kf, 0.0)

        def peel_body(go):
            done = clo_ref[...] >= kf
            hi = hi_ref[...]
            nxt = fold_slabs(jnp.maximum, lambda x, c, r0: jnp.where(x < hi, x, -inf), -inf)
            t = jnp.where(done, thr_ref[...], jnp.max(nxt, axis=0, keepdims=True))
            cnt = count(lambda x, tt: x >= tt, t)
            thr_ref[...] = t
            clo_ref[...] = jnp.where(done, clo_ref[...], cnt)
            hi_ref[...] = jnp.where(done, hi, t)
            return any_lane(jnp.logical_and(jnp.logical_not(done), cnt < kf))

        lax.while_loop(lambda go: go > 0, peel_body, any_lane(clo_ref[...] < kf))

        jst_ref[...] = jnp.where(searched, s_len, -1)
        thr_ref[...] = jnp.where(searched, thr_ref[...], -inf)
        excess = jnp.logical_and(searched, clo_ref[...] > kf)

        @pl.when(any_lane(excess) > 0)
        def _():
            thr = thr_ref[...]
            surplus = clo_ref[...] - kf
            half = kc // 2
            triu = jnp.where(lax.broadcasted_iota(jnp.int32, (half, half), 0) <=
                             lax.broadcasted_iota(jnp.int32, (half, half), 1), 1.0, 0.0).astype(jnp.bfloat16)

            def tie_body(k, carry):
                after, last = carry
                c = nch - 1 - k
                tie = sc_ref[c] == thr
                t01 = jnp.where(tie, 1.0, 0.0).astype(jnp.bfloat16)
                r_bot = jnp.dot(triu, t01[half:, :], preferred_element_type=jnp.float32) + after
                r_top = jnp.dot(triu, t01[:half, :], preferred_element_type=jnp.float32) + r_bot[0:1, :]
                keep = jnp.logical_and(tie, jnp.concatenate([r_top, r_bot], axis=0) > surplus)
                kept_pos = jnp.where(keep, key_pos(c).astype(jnp.float32), -1.0)
                return r_top[0:1, :], jnp.maximum(last, jnp.max(kept_pos, axis=0, keepdims=True))

            _, last = lax.fori_loop(0, nch, tie_body, (jnp.zeros((1, QBE), jnp.float32),
                                                       jnp.full((1, QBE), -1.0, jnp.float32)))
            jst_ref[...] = jnp.where(excess, last.astype(jnp.int32), jst_ref[...])

    for h in range(HEADS):
        hs = slice(h * HEAD_DIM, (h + 1) * HEAD_DIM)
        ql = lax.dot_general(q_ref[:, hs], wuk_ref[h], _NT, preferred_element_type=jnp.float32)
        qlat_ref[h * QBE:(h + 1) * QBE, :] = (ql * (HEAD_DIM ** -0.5 * LOG2E)).astype(qlat_ref.dtype)
    m_ref[...] = jnp.full_like(m_ref, NEG_BIG)
    l_ref[...] = jnp.zeros_like(l_ref)
    acc_ref[...] = jnp.zeros_like(acc_ref)
    thr = thr_ref[...]
    jst = jst_ref[...]

    def logits_into(buf_ref, c):
        off = pl.multiple_of(c * kc, kc)
        buf_ref[...] = lax.dot_general(cn_ref[pl.ds(off, kc), :], qlat_ref[...], _NT,
                                       preferred_element_type=jnp.float32)

    def softmax_pv(buf_ref, c):
        sc = sc_ref[c]
        sel = jnp.logical_or(sc > thr, jnp.logical_and(sc == thr, key_pos(c) <= jst))
        bias_ref[...] = jnp.where(sel, 0.0, NEG_BIG)
        for h in range(HEADS):
            qs = slice(h * QBE, (h + 1) * QBE)
            st = buf_ref[:, qs] + bias_ref[...]
            m_old = m_ref[:, qs]
            m_new = jnp.maximum(m_old, jnp.max(st, axis=0, keepdims=True))
            alpha = jnp.exp2(m_old - m_new)
            p = jnp.exp2(st - m_new)
            l_ref[:, qs] = alpha * l_ref[:, qs] + jnp.sum(p, axis=0, keepdims=True)
            pv = jnp.dot(cnt_ref[c], p.astype(jnp.bfloat16), preferred_element_type=jnp.float32)
            acc_ref[:, qs] = alpha * acc_ref[:, qs] + pv
            m_ref[:, qs] = m_new

    run_pairs(logits_into, softmax_pv)

    for h in range(HEADS):
        hs = slice(h * HEAD_DIM, (h + 1) * HEAD_DIM)
        qs = slice(h * QBE, (h + 1) * QBE)
        o_lat = jnp.transpose(acc_ref[:, qs] / l_ref[:, qs]).astype(jnp.bfloat16)
        o = jnp.dot(o_lat, wuv_ref[h], preferred_element_type=jnp.float32)
        o_ref[:, hs] = (o * _silu(z_ref[:, hs])).astype(o_ref.dtype)


def _dsa(p16b, p32b, kv_g, w_uk, w_uv):
    b, s, _ = p16b.shape
    kc = min(KC, s)
    ksel = min(TOPK_MAX, s // 4)
    assert s % kc == 0 and kc % QBE == 0
    assert s // FOLD <= 256
    nq = HEADS * QBE
    row = lambda n: pltpu.VMEM((1, n), jnp.float32)
    full3 = pl.BlockSpec((HEADS, LATENT, HEAD_DIM), lambda bi, i: (0, 0, 0))
    return pl.pallas_call(
        functools.partial(_dsa_kernel, s_len=s, kc=kc, ksel=ksel, n_coarse=8, n_bisect=9),
        grid=(b, s // QBE),
        in_specs=[pl.BlockSpec((None, QBE, W), lambda bi, i: (bi, i, P16_BQ)),
                  pl.BlockSpec((None, QBE, W), lambda bi, i: (bi, i, P16_QI)),
                  pl.BlockSpec((None, QBE, LANE), lambda bi, i: (bi, i, P32_SM)),
                  pl.BlockSpec((None, QBE, W), lambda bi, i: (bi, i, P32_BZ)),
                  pl.BlockSpec((None, s, LANE), lambda bi, i: (bi, 0, P16_KI)),
                  pl.BlockSpec((None, s, LANE), lambda bi, i: (bi, 0, P32_BC)),
                  pl.BlockSpec((1, LATENT), lambda bi, i: (0, 0)),
                  full3, full3],
        out_specs=pl.BlockSpec((None, QBE, W), lambda bi, i: (bi, i, 0)),
        out_shape=jax.ShapeDtypeStruct((b, s, W), jnp.bfloat16),
        scratch_shapes=[pltpu.VMEM((s, LATENT), jnp.bfloat16),
                        pltpu.VMEM((s // kc, LATENT, kc), jnp.bfloat16),
                        pltpu.VMEM((s // kc, kc, QBE), jnp.float32),
                        pltpu.VMEM((s // kc, kc, QBE), jnp.bfloat16),
                        pltpu.VMEM((IDX_HEADS * QBE, LANE), jnp.bfloat16),
                        pltpu.VMEM((nq, LATENT), jnp.bfloat16),
                        pltpu.VMEM((kc, nq), jnp.float32),
                        pltpu.VMEM((kc, nq), jnp.float32),
                        pltpu.VMEM((kc, QBE), jnp.float32),
                        row(QBE),
                        pltpu.VMEM((1, QBE), jnp.int32),
                        row(QBE), row(QBE), row(QBE), row(QBE),
                        row(nq), row(nq),
                        pltpu.VMEM((LATENT, nq), jnp.float32)],
        compiler_params=_cparams(("arbitrary", "arbitrary")),
        name="dsa_attn",
    )(p16b, p16b, p32b, p32b, p16b, p32b, kv_g.reshape(1, LATENT),
      w_uk.astype(jnp.bfloat16), w_uv.astype(jnp.bfloat16))


def _merge_kernel(au_ref, av_ref, az_ref, cb_ref, cc_ref, cx_ref, cz_ref,
                  lng_ref, lnb_ref, ws_ref, bs_ref, cw_ref,
                  yb_ref, yd_ref, hn_in_ref, x_ref, wg_ref, wb_ref, wo_ref, gn_ref,
                  h_ref, hn_ref, ya_ref, yc_ref, halo_ref, *, tm, tiles_per_seq):
    first_tile = pl.program_id(0) % tiles_per_seq == 0
    _mix_ac_tile(au_ref, av_ref, az_ref, cb_ref, cc_ref, cx_ref, cz_ref,
                 lng_ref, lnb_ref, ws_ref, bs_ref, cw_ref, ya_ref, yc_ref, halo_ref, first_tile, tm)
    hn_in = hn_in_ref[...]
    merged = None
    for n, y_ref in enumerate((ya_ref, yb_ref, yc_ref, yd_ref)):
        cols = slice(n * D_MODEL, (n + 1) * D_MODEL)
        gate = _sigmoid(jnp.dot(hn_in, wg_ref[:, cols], preferred_element_type=jnp.float32))
        term = gate * jnp.dot(y_ref[...], wb_ref[n], preferred_element_type=jnp.float32)
        merged = term if merged is None else merged + term
    h = x_ref[...] + jnp.dot(merged.astype(jnp.bfloat16), wo_ref[...], preferred_element_type=jnp.float32)
    h_ref[...] = h
    hn = h * lax.rsqrt(jnp.mean(h * h, axis=-1, keepdims=True) + EPS) * gn_ref[...]
    hn_ref[...] = hn.astype(hn_ref.dtype)


def _merge(p32, yb, yd, hn, h, ln_g, ln_b, w_s, b_s, conv_w, wg, wb, wo, g_next, hn_dtype, s_len, tm=256):
    m, d = h.shape
    assert s_len % tm == 0 and tm % GM_CHUNK == 0
    pblk = lambda idx: pl.BlockSpec((tm, W), lambda i, idx=idx: (i, idx))
    yblk = pl.BlockSpec((tm, W), lambda i: (i, 0))
    hblk = pl.BlockSpec((tm, d), lambda i: (i, 0))
    const = lambda shp: pl.BlockSpec(shp, lambda i: (0,) * len(shp))
    bs_full = jnp.repeat(jnp.transpose(b_s), LANE, axis=1)
    return pl.pallas_call(
        functools.partial(_merge_kernel, tm=tm, tiles_per_seq=s_len // tm),
        grid=(m // tm,),
        in_specs=[pblk(P32_AU), pblk(P32_AV), pblk(P32_AZ), pblk(P32_CB), pblk(P32_CC), pblk(P32_CX), pblk(P32_CZ),
                  const((1, W)), const((1, W)), const((GM_GROUPS, GM_CHUNK, GM_CHUNK)),
                  const((GM_CHUNK, W)), const((CONV_WIDTH, W)),
                  yblk, yblk, hblk, hblk,
                  const((d, N_BRANCH * d)), const((N_BRANCH, W, d)), const((d, d)), const((1, d))],
        out_specs=[hblk, hblk],
        out_shape=[jax.ShapeDtypeStruct((m, d), jnp.float32), jax.ShapeDtypeStruct((m, d), hn_dtype)],
        scratch_shapes=[pltpu.VMEM((tm, W), jnp.bfloat16), pltpu.VMEM((tm, W), jnp.bfloat16),
                        pltpu.VMEM((8, W), jnp.float32)],
        compiler_params=_cparams(("arbitrary",)),
        name="merge",
    )(p32, p32, p32, p32, p32, p32, p32,
      ln_g.reshape(1, W), ln_b.reshape(1, W), w_s, bs_full, conv_w,
      yb, yd, hn, h, wg, wb, wo, g_next.reshape(1, d))


def kernel(x, norm_g, w_in, gm_ln_g, gm_ln_b, gm_w_s, gm_b_s, dsa_kv_g, dsa_w_uk, dsa_w_uv,
           conv_w, fox_b_f, w_branch, w_out, final_g):
    b, s, d = x.shape
    depth = w_in.shape[0]
    m = b * s
    h = x.reshape(m, d)
    hn = _rmsnorm(h, norm_g[0], jnp.bfloat16)
    for l in range(depth):
        w16, w32, wvt, wg = _prep_w_in(w_in[l])
        p16 = _matmul(hn, w16, jnp.bfloat16, tm=512, tn=N16, name="in_proj16")
        p32 = _matmul(hn, w32, jnp.float32, tm=512, tn=N32 // 2, name="in_proj32")
        vt = _matmul_t(hn, wvt, tm=min(KC, s), name="in_proj_vt")
        p16b = p16.reshape(b, s, N16)
        p32b = p32.reshape(b, s, N32)
        qaug, kaug = _fox_cum(p32b, fox_b_f[l])
        yd = _fox(p16b, p32b, qaug, kaug, vt)
        yb = _dsa(p16b, p32b, dsa_kv_g[l], dsa_w_uk[l], dsa_w_uv[l])
        last = l == depth - 1
        g_next = final_g if last else norm_g[l + 1]
        h, hn = _merge(p32, yb.reshape(m, W), yd.reshape(m, W), hn, h,
                       gm_ln_g[l], gm_ln_b[l], gm_w_s[l], gm_b_s[l], conv_w[l],
                       wg, w_branch[l].astype(jnp.bfloat16), w_out[l].astype(jnp.bfloat16), g_next,
                       jnp.float32 if last else jnp.bfloat16, s)
    return hn.reshape(b, s, d)
```

```python
import functools

import numpy as np
import jax
import jax.numpy as jnp
from jax import lax
from jax.experimental import pallas as pl
from jax.experimental.pallas import tpu as pltpu

D_MODEL = 1024
N_BRANCH = 4
W = 512
EPS = 1e-6
QBE = 256
KC = 512
FOLD = 64
CUM_ROWS = 512
GM_GROUPS = 4
GM_CHUNK = 128
HEADS = 4
HEAD_DIM = W // HEADS
LATENT = 128
IDX_HEADS = 4
IDX_DIM = 64
TOPK_MAX = 256
CONV_WIDTH = 3
LANE = 128
NEG_BIG = -1e30
LOG2E = 1.4426950408889634
VMEM_LIMIT = 56 * 1024 * 1024

P16_DQ, P16_DK, P16_BQ = 0, 1, 2
P16_QI = 3
P16_KI = 16
N16 = 17 * LANE
P32_AU, P32_AV, P32_AZ, P32_BZ = 0, 1, 2, 3
P32_CB, P32_CC, P32_CX, P32_CZ, P32_DZ = 4, 5, 6, 7, 8
P32_BC = 36
P32_SM = 37
N32 = 38 * LANE

_NT = (((1,), (1,)), ((), ()))


def _cparams(sem):
    return pltpu.CompilerParams(dimension_semantics=sem, vmem_limit_bytes=VMEM_LIMIT)


def _sigmoid(z):
    return 1.0 / (1.0 + jnp.exp(-z))


def _silu(z):
    return z * _sigmoid(z)


def _rmsnorm_kernel(x_ref, g_ref, o_ref):
    x = x_ref[...]
    y = x * lax.rsqrt(jnp.mean(x * x, axis=-1, keepdims=True) + EPS)
    o_ref[...] = (y * g_ref[...]).astype(o_ref.dtype)


def _rmsnorm(x2, g, out_dtype, tm=512):
    m, d = x2.shape
    return pl.pallas_call(
        _rmsnorm_kernel,
        grid=(m // tm,),
        in_specs=[pl.BlockSpec((tm, d), lambda i: (i, 0)), pl.BlockSpec((1, d), lambda i: (0, 0))],
        out_specs=pl.BlockSpec((tm, d), lambda i: (i, 0)),
        out_shape=jax.ShapeDtypeStruct((m, d), out_dtype),
        compiler_params=_cparams(("arbitrary",)),
        name="rmsnorm",
    )(x2, g.reshape(1, d))


def _matmul_kernel(a_ref, w_ref, o_ref):
    o_ref[...] = jnp.dot(a_ref[...], w_ref[...], preferred_element_type=jnp.float32).astype(o_ref.dtype)


def _matmul(a, w, out_dtype, tm, tn, name):
    m, k = a.shape
    n = w.shape[1]
    return pl.pallas_call(
        _matmul_kernel,
        grid=(n // tn, m // tm),
        in_specs=[pl.BlockSpec((tm, k), lambda j, i: (i, 0)), pl.BlockSpec((k, tn), lambda j, i: (0, j))],
        out_specs=pl.BlockSpec((tm, tn), lambda j, i: (i, j)),
        out_shape=jax.ShapeDtypeStruct((m, n), out_dtype),
        compiler_params=_cparams(("arbitrary", "arbitrary")),
        name=name,
    )(a, w)


def _prep_w_in(w):
    d = w.shape[0]
    w = w.astype(jnp.bfloat16)
    sizes = (W, W, W,
             W, LATENT, IDX_HEADS * IDX_DIM, IDX_DIM, IDX_HEADS, W,
             W, W, W, W,
             W, W, W, HEADS, W,
             N_BRANCH * D_MODEL)
    parts, off = [], 0
    for s in sizes:
        parts.append(w[:, off:off + s])
        off += s
    (a_u, a_v, a_z, b_q, b_c, b_qi, b_ki, b_wi, b_z,
     c_b, c_c, c_x, c_z, d_q, d_k, d_v, d_f, d_z, gates) = parts
    zeros = lambda n: jnp.zeros((d, n), w.dtype)
    qi = []
    for h in range(IDX_HEADS):
        qi += [b_qi[:, h * IDX_DIM:(h + 1) * IDX_DIM], zeros(LANE - IDX_DIM)]
    w16 = jnp.concatenate([d_q, d_k, b_q] + qi + [b_ki, zeros(LANE - IDX_DIM)], axis=1)
    w32 = jnp.concatenate([a_u, a_v, a_z, b_z, c_b, c_c, c_x, c_z, d_z, b_c,
                           d_f, b_wi, zeros(LANE - HEADS - IDX_HEADS)], axis=1)
    return w16, w32, jnp.transpose(d_v), gates


def _matmul_t_kernel(wt_ref, a_ref, o_ref):
    o_ref[...] = lax.dot_general(wt_ref[...], a_ref[...], _NT,
                                 preferred_element_type=jnp.float32).astype(o_ref.dtype)


def _matmul_t(a, wt, tm, name):
    m, k = a.shape
    n = wt.shape[0]
    return pl.pallas_call(
        _matmul_t_kernel,
        grid=(m // tm,),
        in_specs=[pl.BlockSpec((n, k), lambda i: (0, 0)), pl.BlockSpec((tm, k), lambda i: (i, 0))],
        out_specs=pl.BlockSpec((None, n, tm), lambda i: (i, 0, 0)),
        out_shape=jax.ShapeDtypeStruct((m // tm, n, tm), jnp.bfloat16),
        compiler_params=_cparams(("arbitrary",)),
        name=name,
    )(wt, a)


def _mix_ac_tile(au_ref, av_ref, az_ref, cb_ref, cc_ref, cx_ref, cz_ref,
                 lng_ref, lnb_ref, ws_ref, bs_ref, cw_ref, ya_ref, yc_ref, halo_ref, first_tile, tt):
    v = av_ref[...]
    mu = jnp.mean(v, axis=-1, keepdims=True)
    vc = v - mu
    var = jnp.mean(vc * vc, axis=-1, keepdims=True)
    vn = (vc * lax.rsqrt(var + EPS) * lng_ref[...] + lnb_ref[...]).astype(jnp.bfloat16)
    row = lax.broadcasted_iota(jnp.int32, (GM_CHUNK, GM_CHUNK), 0)
    col = lax.broadcasted_iota(jnp.int32, (GM_CHUNK, GM_CHUNK), 1)
    tril = row >= col
    wg = [jnp.where(tril, ws_ref[g], 0.0).astype(jnp.bfloat16) for g in range(GM_GROUPS)]
    for ch in range(tt // GM_CHUNK):
        rows = slice(ch * GM_CHUNK, (ch + 1) * GM_CHUNK)
        for g in range(GM_GROUPS):
            cols = slice(g * LANE, (g + 1) * LANE)
            mixed = jnp.dot(wg[g], vn[rows, cols], preferred_element_type=jnp.float32) + bs_ref[:, cols]
            ya_ref[rows, cols] = (au_ref[rows, cols] * mixed * _silu(az_ref[rows, cols])).astype(ya_ref.dtype)

    @pl.when(first_tile)
    def _():
        halo_ref[...] = jnp.zeros_like(halo_ref)

    y = cc_ref[...] * cx_ref[...]
    ext = jnp.concatenate([halo_ref[...], y], axis=0)
    conv = cw_ref[2:3, :] * y
    for j in range(CONV_WIDTH - 1):
        shift = CONV_WIDTH - 1 - j
        conv = conv + cw_ref[j:j + 1, :] * ext[8 - shift:8 - shift + tt, :]
    yc_ref[...] = (cb_ref[...] * conv * _silu(cz_ref[...])).astype(yc_ref.dtype)
    halo_ref[...] = y[tt - 8:, :]


def _split3(x):
    hi = x.astype(jnp.bfloat16)
    r1 = x - hi.astype(jnp.float32)
    mid = r1.astype(jnp.bfloat16)
    lo = (r1 - mid.astype(jnp.float32)).astype(jnp.bfloat16)
    return hi, mid, lo


def _aug_placement():
    pq = np.zeros((3 * LANE, W), np.float32)
    pk = np.zeros((3 * LANE, W), np.float32)
    cq = np.zeros((1, W), np.float32)
    ck = np.zeros((1, W), np.float32)
    for h in range(HEADS):
        for j in range(3):
            pq[j * LANE + h, h * HEAD_DIM + j] = 1.0
            cq[0, h * HEAD_DIM + 3 + j] = 1.0
            pk[j * LANE + h, h * HEAD_DIM + 3 + j] = -1.0
            ck[0, h * HEAD_DIM + j] = 1.0
    return (jnp.asarray(pq, jnp.bfloat16), jnp.asarray(pk, jnp.bfloat16), jnp.asarray(cq), jnp.asarray(ck))


def _fox_cum_kernel(f_ref, bias_ref, pq_ref, pk_ref, cq_ref, ck_ref, qa_ref, ka_ref, carry_ref, *, s):
    rows = min(CUM_ROWS, s)
    row = lax.broadcasted_iota(jnp.int32, (rows, rows), 0)
    col = lax.broadcasted_iota(jnp.int32, (rows, rows), 1)
    ones_tril = jnp.where(row >= col, 1.0, 0.0).astype(jnp.bfloat16)
    carry_ref[...] = jnp.zeros_like(carry_ref)

    def body(c, _):
        off = pl.multiple_of(c * rows, rows)
        x = f_ref[pl.ds(off, rows), :] + bias_ref[...]
        ls = jnp.minimum(x, 0.0) - jnp.log1p(jnp.exp(-jnp.abs(x)))
        hi, mid, lo = _split3(ls * LOG2E)
        dot = lambda p: jnp.dot(ones_tril, p, preferred_element_type=jnp.float32)
        cs = (dot(hi) + dot(mid)) + dot(lo) + carry_ref[0:1, :]
        carry_ref[0:1, :] = cs[rows - 1:rows, :]
        parts = jnp.concatenate(_split3(cs), axis=1)
        qa = jnp.dot(parts, pq_ref[...], preferred_element_type=jnp.float32) + cq_ref[...]
        ka = jnp.dot(parts, pk_ref[...], preferred_element_type=jnp.float32) + ck_ref[...]
        qa_ref[pl.ds(off, rows), :] = qa.astype(qa_ref.dtype)
        ka_ref[pl.ds(off, rows), :] = ka.astype(ka_ref.dtype)
        return 0

    lax.fori_loop(0, s // rows, body, 0)


def _fox_cum(p32b, b_f):
    b, s, _ = p32b.shape
    bias = jnp.zeros((1, LANE), jnp.float32).at[0, :HEADS].set(b_f)
    pq, pk, cq, ck = _aug_placement()
    const = lambda shp: pl.BlockSpec(shp, lambda bi: (0, 0))
    out = jax.ShapeDtypeStruct((b, s, W), jnp.bfloat16)
    return pl.pallas_call(
        functools.partial(_fox_cum_kernel, s=s),
        grid=(b,),
        in_specs=[pl.BlockSpec((None, s, LANE), lambda bi: (bi, 0, P32_SM)), const((1, LANE)),
                  const((3 * LANE, W)), const((3 * LANE, W)), const((1, W)), const((1, W))],
        out_specs=[pl.BlockSpec((None, s, W), lambda bi: (bi, 0, 0))] * 2,
        out_shape=[out, out],
        scratch_shapes=[pltpu.VMEM((8, LANE), jnp.float32)],
        compiler_params=_cparams(("arbitrary",)),
        name="fox_cum",
    )(p32b, bias, pq, pk, cq, ck)


def _fox_kernel(q_ref, qa_ref, k_ref, ka_ref, vt_ref, z_ref, o_ref,
                qf_ref, sa_ref, sb_ref, m_ref, l_ref, acc_ref, *, kc):
    i = pl.program_id(1)
    ndiag = (i * QBE) // kc
    scale = HEAD_DIM ** -0.5 * LOG2E
    for h in range(HEADS):
        hs = slice(h * HEAD_DIM, (h + 1) * HEAD_DIM)
        qs = (q_ref[:, hs].astype(jnp.float32) * scale).astype(jnp.bfloat16)
        qf_ref[h] = jnp.concatenate([qs, qa_ref[:, hs]], axis=1)
    m_ref[...] = jnp.full_like(m_ref, NEG_BIG)
    l_ref[...] = jnp.zeros_like(l_ref)
    acc_ref[...] = jnp.zeros_like(acc_ref)

    def scores_into(buf_ref, c):
        off = pl.multiple_of(c * kc, kc)
        for h in range(HEADS):
            hs = slice(h * HEAD_DIM, (h + 1) * HEAD_DIM)
            kf = jnp.concatenate([k_ref[pl.ds(off, kc), hs], ka_ref[pl.ds(off, kc), hs]], axis=1)
            buf_ref[:, h * QBE:(h + 1) * QBE] = lax.dot_general(
                kf, qf_ref[h], _NT, preferred_element_type=jnp.float32)

    def softmax_pv(buf_ref, c, masked):
        if masked:
            causal = (c * kc + lax.broadcasted_iota(jnp.int32, (kc, QBE), 0) <=
                      i * QBE + lax.broadcasted_iota(jnp.int32, (kc, QBE), 1))
        for h in range(HEADS):
            hs = slice(h * HEAD_DIM, (h + 1) * HEAD_DIM)
            st = buf_ref[:, h * QBE:(h + 1) * QBE]
            if masked:
                st = jnp.where(causal, st, NEG_BIG)
            m_old = m_ref[h]
            m_new = jnp.maximum(m_old, jnp.max(st, axis=0, keepdims=True))
            alpha = jnp.exp2(m_old - m_new)
            p = jnp.exp2(st - m_new)
            l_ref[h] = alpha * l_ref[h] + jnp.sum(p, axis=0, keepdims=True)
            pv = jnp.dot(vt_ref[c, hs, :], p.astype(jnp.bfloat16), preferred_element_type=jnp.float32)
            acc_ref[h] = alpha * acc_ref[h] + pv
            m_ref[h] = m_new

    scores_into(sa_ref, 0)

    def pair_body(j, _):
        scores_into(sb_ref, 2 * j + 1)
        softmax_pv(sa_ref, 2 * j, False)
        scores_into(sa_ref, 2 * j + 2)
        softmax_pv(sb_ref, 2 * j + 1, False)
        return 0

    lax.fori_loop(0, ndiag // 2, pair_body, 0)

    @pl.when(ndiag % 2 == 1)
    def _():
        scores_into(sb_ref, ndiag)
        softmax_pv(sa_ref, ndiag - 1, False)
        softmax_pv(sb_ref, ndiag, True)

    @pl.when(ndiag % 2 == 0)
    def _():
        softmax_pv(sa_ref, ndiag, True)

    for h in range(HEADS):
        hs = slice(h * HEAD_DIM, (h + 1) * HEAD_DIM)
        o = jnp.transpose(acc_ref[h] / l_ref[h])
        o_ref[:, hs] = (o * _silu(z_ref[:, hs])).astype(o_ref.dtype)


def _fox(p16b, p32b, qaug, kaug, vt):
    b, s, _ = p16b.shape
    kc = vt.shape[-1]
    assert s % kc == 0 and kc % QBE == 0
    nck = s // kc
    vt = vt.reshape(b, nck, W, kc)
    row = lambda: pltpu.VMEM((HEADS, 1, QBE), jnp.float32)
    return pl.pallas_call(
        functools.partial(_fox_kernel, kc=kc),
        grid=(b, s // QBE),
        in_specs=[pl.BlockSpec((None, QBE, W), lambda bi, i: (bi, i, P16_DQ)),
                  pl.BlockSpec((None, QBE, W), lambda bi, i: (bi, i, 0)),
                  pl.BlockSpec((None, s, W), lambda bi, i: (bi, 0, P16_DK)),
                  pl.BlockSpec((None, s, W), lambda bi, i: (bi, 0, 0)),
                  pl.BlockSpec((None, nck, W, kc), lambda bi, i: (bi, 0, 0, 0)),
                  pl.BlockSpec((None, QBE, W), lambda bi, i: (bi, i, P32_DZ))],
        out_specs=pl.BlockSpec((None, QBE, W), lambda bi, i: (bi, i, 0)),
        out_shape=jax.ShapeDtypeStruct((b, s, W), jnp.bfloat16),
        scratch_shapes=[pltpu.VMEM((HEADS, QBE, 2 * HEAD_DIM), jnp.bfloat16),
                        pltpu.VMEM((kc, HEADS * QBE), jnp.float32),
                        pltpu.VMEM((kc, HEADS * QBE), jnp.float32),
                        row(), row(),
                        pltpu.VMEM((HEADS, HEAD_DIM, QBE), jnp.float32)],
        compiler_params=_cparams(("arbitrary", "arbitrary")),
        name="fox_attn",
    )(p16b, qaug, p16b, kaug, vt, p32b)


def _dsa_kernel(q_ref, qi_ref, sm_ref, z_ref, ki_ref, c_ref, kvg_ref, wuk_ref, wuv_ref, o_ref,
                 cn_ref, cnt_ref, sc_ref, scb_ref, qia_ref, qlat_ref, lga_ref, lgb_ref, bias_ref,
                 thr_ref, jst_ref, lo_ref, hib_ref, hi_ref, clo_ref,
                 m_ref, l_ref, acc_ref, *, s_len, kc, ksel, n_coarse, n_bisect):
    i = pl.program_id(1)
    ndiag = (i * QBE) // kc
    nch = ndiag + 1
    qpos = i * QBE + lax.broadcasted_iota(jnp.int32, (1, QBE), 1)
    kf = float(ksel)
    inf = float("inf")

    def key_pos(c):
        return c * kc + lax.broadcasted_iota(jnp.int32, (kc, QBE), 0)

    @pl.when(i == 0)
    def _():
        def nbody(c, _):
            off = pl.multiple_of(c * kc, kc)
            x = c_ref[pl.ds(off, kc), :]
            y = x * lax.rsqrt(jnp.mean(x * x, axis=-1, keepdims=True) + EPS) * kvg_ref[...]
            cn_ref[pl.ds(off, kc), :] = y.astype(cn_ref.dtype)
            cnt_ref[c] = jnp.transpose(y).astype(cnt_ref.dtype)
            return 0
        lax.fori_loop(0, s_len // kc, nbody, 0)

    wit = jnp.transpose(sm_ref[...])[HEADS:HEADS + IDX_HEADS, :] * (IDX_HEADS ** -0.5 * IDX_DIM ** -0.5)
    for h in range(IDX_HEADS):
        qia_ref[h * QBE:(h + 1) * QBE, :] = qi_ref[:, h * LANE:(h + 1) * LANE]

    def dots_into(buf_ref, c):
        off = pl.multiple_of(c * kc, kc)
        buf_ref[...] = lax.dot_general(ki_ref[pl.ds(off, kc), :], qia_ref[...], _NT,
                                       preferred_element_type=jnp.float32)

    def scores_from(buf_ref, c):
        sc = None
        for h in range(IDX_HEADS):
            term = wit[h:h + 1, :] * jnp.maximum(buf_ref[:, h * QBE:(h + 1) * QBE], 0.0)
            sc = term if sc is None else sc + term
        sc_ref[c] = sc
        scb_ref[c] = sc.astype(scb_ref.dtype)

    def run_pairs(produce, consume):
        produce(lga_ref, 0)

        def pair_body(j, _):
            produce(lgb_ref, 2 * j + 1)
            consume(lga_ref, 2 * j)
            produce(lga_ref, jnp.minimum(2 * j + 2, ndiag))
            consume(lgb_ref, 2 * j + 1)
            return 0

        lax.fori_loop(0, nch // 2, pair_body, 0)

        @pl.when(nch % 2 == 1)
        def _():
            consume(lga_ref, ndiag)

    run_pairs(dots_into, scores_from)
    diag = jnp.where(key_pos(ndiag) <= qpos, sc_ref[ndiag], -inf)
    sc_ref[ndiag] = diag
    scb_ref[ndiag] = diag.astype(scb_ref.dtype)

    def fold_slabs(op, fn, init_val):
        def cbody(c, acc):
            x = sc_ref[c]
            for j in range(kc // FOLD):
                acc = op(acc, fn(x[j * FOLD:(j + 1) * FOLD, :], c, j * FOLD))
            return acc
        return lax.fori_loop(0, nch, cbody, jnp.full((FOLD, QBE), init_val, jnp.float32))

    def count(pred, t):
        acc = fold_slabs(lambda a, m: jnp.where(m, a + 1.0, a), lambda x, c, r0: pred(x, t), 0.0)
        return jnp.sum(acc, axis=0, keepdims=True)

    def count_rounded(tb):
        one, zero = jnp.ones((), jnp.bfloat16), jnp.zeros((), jnp.bfloat16)

        def cbody(c, acc):
            x = scb_ref[c]
            for j in range(kc // FOLD):
                acc = acc + jnp.where(x[j * FOLD:(j + 1) * FOLD, :] >= tb, one, zero)
            return acc
        acc = lax.fori_loop(0, nch, cbody, jnp.zeros((FOLD, QBE), jnp.bfloat16))
        return jnp.sum(acc.astype(jnp.float32), axis=0, keepdims=True)

    def any_lane(flag):
        return (jnp.max(jnp.where(flag, 1.0, 0.0)) > 0.5).astype(jnp.int32)

    searched = qpos + 1 > ksel
    thr_ref[...] = jnp.full_like(thr_ref, -inf)
    jst_ref[...] = jnp.full_like(jst_ref, -1)

    @pl.when((i + 1) * QBE > ksel)
    def _():
        mn = fold_slabs(jnp.minimum, lambda x, c, r0: jnp.where(x == -inf, inf, x), inf)
        mx = fold_slabs(jnp.maximum, lambda x, c, r0: x, -inf)
        lo_ref[...] = jnp.min(mn, axis=0, keepdims=True)
        hib_ref[...] = jnp.max(mx, axis=0, keepdims=True)
        hi_ref[...] = jnp.full_like(hi_ref, inf)
        clo_ref[...] = (qpos + 1).astype(jnp.float32)

        def coarse_body(_, carry):
            lo, hib = lo_ref[...], hib_ref[...]
            tb = (0.5 * lo + 0.5 * hib).astype(jnp.bfloat16)
            t = tb.astype(jnp.float32)
            ge = count_rounded(tb) >= kf
            lo_ref[...] = jnp.where(ge, jnp.maximum(lo, t - (jnp.abs(t) * 2.0 ** -6 + 1e-30)), lo)
            hib_ref[...] = jnp.where(ge, hib, jnp.minimum(hib, t))
            hi_ref[...] = jnp.where(ge, hi_ref[...], jnp.minimum(hi_ref[...], t))
            return carry

        lax.fori_loop(0, n_coarse, coarse_body, 0)
        clo_ref[...] = jnp.full_like(clo_ref, -1.0)

        def bis_body(_, carry):
            lo, hib = lo_ref[...], hib_ref[...]
            mid = 0.5 * lo + 0.5 * hib
            cnt = count(lambda x, t: x >= t, mid)
            ge = cnt >= kf
            lo_ref[...] = jnp.where(ge, mid, lo)
            clo_ref[...] = jnp.where(ge, cnt, clo_ref[...])
            hib_ref[...] = jnp.where(ge, hib, mid)
            hi_ref[...] = jnp.where(ge, hi_ref[...], mid)
            return carry

        lax.fori_loop(0, n_bisect, bis_body, 0)

        resolved = jnp.logical_or(clo_ref[...] == kf, jnp.logical_not(searched))
        thr_ref[...] = jnp.where(resolved, lo_ref[...], inf)
        clo_ref[...] = jnp.where(resolved, kf, 0.0)

        def peel_body(go):
            done = clo_ref[...] >= kf
            hi = hi_ref[...]
            nxt = fold_slabs(jnp.maximum, lambda x, c, r0: jnp.where(x < hi, x, -inf), -inf)
            t = jnp.where(done, thr_ref[...], jnp.max(nxt, axis=0, keepdims=True))
            cnt = count(lambda x, tt: x >= tt, t)
            thr_ref[...] = t
            clo_ref[...] = jnp.where(done, clo_ref[...], cnt)
            hi_ref[...] = jnp.where(done, hi, t)
            return any_lane(jnp.logical_and(jnp.logical_not(done), cnt < kf))

        lax.while_loop(lambda go: go > 0, peel_body, any_lane(clo_ref[...] < kf))

        jst_ref[...] = jnp.where(searched, s_len, -1)
        thr_ref[...] = jnp.where(searched, thr_ref[...], -inf)
        excess = jnp.logical_and(searched, clo_ref[...] > kf)

        @pl.when(any_lane(excess) > 0)
        def _():
            thr = thr_ref[...]
            surplus = clo_ref[...] - kf
            half = kc // 2
            triu = jnp.where(lax.broadcasted_iota(jnp.int32, (half, half), 0) <=
                             lax.broadcasted_iota(jnp.int32, (half, half), 1), 1.0, 0.0).astype(jnp.bfloat16)

            def tie_cond(carry):
                k, go, _, _ = carry
                return jnp.logical_and(k < nch, go > 0)

            def tie_body(carry):
                k, _, after, last = carry
                c = nch - 1 - k
                tie = sc_ref[c] == thr
                t01 = jnp.where(tie, 1.0, 0.0).astype(jnp.bfloat16)
                r_bot = jnp.dot(triu, t01[half:, :], preferred_element_type=jnp.float32) + after
                r_top = jnp.dot(triu, t01[:half, :], preferred_element_type=jnp.float32) + r_bot[0:1, :]
                keep = jnp.logical_and(tie, jnp.concatenate([r_top, r_bot], axis=0) > surplus)
                kept_pos = jnp.where(keep, key_pos(c).astype(jnp.float32), -1.0)
                seen = r_top[0:1, :]
                open_rows = jnp.logical_and(excess, seen <= surplus)
                return (k + 1, any_lane(open_rows), seen,
                        jnp.maximum(last, jnp.max(kept_pos, axis=0, keepdims=True)))

            _, _, _, last = lax.while_loop(tie_cond, tie_body, (
                jnp.int32(0), jnp.int32(1), jnp.zeros((1, QBE), jnp.float32),
                jnp.full((1, QBE), -1.0, jnp.float32)))
            jst_ref[...] = jnp.where(excess, last.astype(jnp.int32), jst_ref[...])

    for h in range(HEADS):
        hs = slice(h * HEAD_DIM, (h + 1) * HEAD_DIM)
        ql = lax.dot_general(q_ref[:, hs], wuk_ref[h], _NT, preferred_element_type=jnp.float32)
        qlat_ref[h * QBE:(h + 1) * QBE, :] = (ql * (HEAD_DIM ** -0.5 * LOG2E)).astype(qlat_ref.dtype)
    m_ref[...] = jnp.full_like(m_ref, NEG_BIG)
    l_ref[...] = jnp.zeros_like(l_ref)
    acc_ref[...] = jnp.zeros_like(acc_ref)
    thr = thr_ref[...]
    jst = jst_ref[...]

    def logits_into(buf_ref, c):
        off = pl.multiple_of(c * kc, kc)
        buf_ref[...] = lax.dot_general(cn_ref[pl.ds(off, kc), :], qlat_ref[...], _NT,
                                       preferred_element_type=jnp.float32)

    def softmax_pv(buf_ref, c):
        sc = sc_ref[c]
        sel = jnp.logical_or(sc > thr, jnp.logical_and(sc == thr, key_pos(c) <= jst))
        bias_ref[...] = jnp.where(sel, 0.0, NEG_BIG)
        for h in range(HEADS):
            qs = slice(h * QBE, (h + 1) * QBE)
            st = buf_ref[:, qs] + bias_ref[...]
            m_old = m_ref[:, qs]
            m_new = jnp.maximum(m_old, jnp.max(st, axis=0, keepdims=True))
            alpha = jnp.exp2(m_old - m_new)
            p = jnp.exp2(st - m_new)
            l_ref[:, qs] = alpha * l_ref[:, qs] + jnp.sum(p, axis=0, keepdims=True)
            pv = jnp.dot(cnt_ref[c], p.astype(jnp.bfloat16), preferred_element_type=jnp.float32)
            acc_ref[:, qs] = alpha * acc_ref[:, qs] + pv
            m_ref[:, qs] = m_new

    run_pairs(logits_into, softmax_pv)

    for h in range(HEADS):
        hs = slice(h * HEAD_DIM, (h + 1) * HEAD_DIM)
        qs = slice(h * QBE, (h + 1) * QBE)
        o_lat = jnp.transpose(acc_ref[:, qs] / l_ref[:, qs]).astype(jnp.bfloat16)
        o = jnp.dot(o_lat, wuv_ref[h], preferred_element_type=jnp.float32)
        o_ref[:, hs] = (o * _silu(z_ref[:, hs])).astype(o_ref.dtype)


def _dsa(p16b, p32b, kv_g, w_uk, w_uv):
    b, s, _ = p16b.shape
    kc = min(KC, s)
    ksel = min(TOPK_MAX, s // 4)
    assert s % kc == 0 and kc % QBE == 0
    assert s // FOLD <= 256
    nq = HEADS * QBE
    row = lambda n: pltpu.VMEM((1, n), jnp.float32)
    full3 = pl.BlockSpec((HEADS, LATENT, HEAD_DIM), lambda bi, i: (0, 0, 0))
    return pl.pallas_call(
        functools.partial(_dsa_kernel, s_len=s, kc=kc, ksel=ksel, n_coarse=8, n_bisect=9),
        grid=(b, s // QBE),
        in_specs=[pl.BlockSpec((None, QBE, W), lambda bi, i: (bi, i, P16_BQ)),
                  pl.BlockSpec((None, QBE, W), lambda bi, i: (bi, i, P16_QI)),
                  pl.BlockSpec((None, QBE, LANE), lambda bi, i: (bi, i, P32_SM)),
                  pl.BlockSpec((None, QBE, W), lambda bi, i: (bi, i, P32_BZ)),
                  pl.BlockSpec((None, s, LANE), lambda bi, i: (bi, 0, P16_KI)),
                  pl.BlockSpec((None, s, LANE), lambda bi, i: (bi, 0, P32_BC)),
                  pl.BlockSpec((1, LATENT), lambda bi, i: (0, 0)),
                  full3, full3],
        out_specs=pl.BlockSpec((None, QBE, W), lambda bi, i: (bi, i, 0)),
        out_shape=jax.ShapeDtypeStruct((b, s, W), jnp.bfloat16),
        scratch_shapes=[pltpu.VMEM((s, LATENT), jnp.bfloat16),
                        pltpu.VMEM((s // kc, LATENT, kc), jnp.bfloat16),
                        pltpu.VMEM((s // kc, kc, QBE), jnp.float32),
                        pltpu.VMEM((s // kc, kc, QBE), jnp.bfloat16),
                        pltpu.VMEM((IDX_HEADS * QBE, LANE), jnp.bfloat16),
                        pltpu.VMEM((nq, LATENT), jnp.bfloat16),
                        pltpu.VMEM((kc, nq), jnp.float32),
                        pltpu.VMEM((kc, nq), jnp.float32),
                        pltpu.VMEM((kc, QBE), jnp.float32),
                        row(QBE),
                        pltpu.VMEM((1, QBE), jnp.int32),
                        row(QBE), row(QBE), row(QBE), row(QBE),
                        row(nq), row(nq),
                        pltpu.VMEM((LATENT, nq), jnp.float32)],
        compiler_params=_cparams(("arbitrary", "arbitrary")),
        name="dsa_attn",
    )(p16b, p16b, p32b, p32b, p16b, p32b, kv_g.reshape(1, LATENT),
      w_uk.astype(jnp.bfloat16), w_uv.astype(jnp.bfloat16))


def _merge_kernel(au_ref, av_ref, az_ref, cb_ref, cc_ref, cx_ref, cz_ref,
                  lng_ref, lnb_ref, ws_ref, bs_ref, cw_ref,
                  yb_ref, yd_ref, hn_in_ref, x_ref, wg_ref, wb_ref, wo_ref, gn_ref,
                  h_ref, hn_ref, ya_ref, yc_ref, halo_ref, *, tm, tiles_per_seq):
    first_tile = pl.program_id(0) % tiles_per_seq == 0
    _mix_ac_tile(au_ref, av_ref, az_ref, cb_ref, cc_ref, cx_ref, cz_ref,
                 lng_ref, lnb_ref, ws_ref, bs_ref, cw_ref, ya_ref, yc_ref, halo_ref, first_tile, tm)
    hn_in = hn_in_ref[...]
    merged = None
    for n, y_ref in enumerate((ya_ref, yb_ref, yc_ref, yd_ref)):
        cols = slice(n * D_MODEL, (n + 1) * D_MODEL)
        gate = _sigmoid(jnp.dot(hn_in, wg_ref[:, cols], preferred_element_type=jnp.float32))
        term = gate * jnp.dot(y_ref[...], wb_ref[n], preferred_element_type=jnp.float32)
        merged = term if merged is None else merged + term
    h = x_ref[...] + jnp.dot(merged.astype(jnp.bfloat16), wo_ref[...], preferred_element_type=jnp.float32)
    h_ref[...] = h
    hn = h * lax.rsqrt(jnp.mean(h * h, axis=-1, keepdims=True) + EPS) * gn_ref[...]
    hn_ref[...] = hn.astype(hn_ref.dtype)


def _merge(p32, yb, yd, hn, h, ln_g, ln_b, w_s, b_s, conv_w, wg, wb, wo, g_next, hn_dtype, s_len, tm=512):
    m, d = h.shape
    assert s_len % tm == 0 and tm % GM_CHUNK == 0
    pblk = lambda idx: pl.BlockSpec((tm, W), lambda i, idx=idx: (i, idx))
    yblk = pl.BlockSpec((tm, W), lambda i: (i, 0))
    hblk = pl.BlockSpec((tm, d), lambda i: (i, 0))
    const = lambda shp: pl.BlockSpec(shp, lambda i: (0,) * len(shp), pipeline_mode=pl.Buffered(1))
    bs_full = jnp.repeat(jnp.transpose(b_s), LANE, axis=1)
    return pl.pallas_call(
        functools.partial(_merge_kernel, tm=tm, tiles_per_seq=s_len // tm),
        grid=(m // tm,),
        in_specs=[pblk(P32_AU), pblk(P32_AV), pblk(P32_AZ), pblk(P32_CB), pblk(P32_CC), pblk(P32_CX), pblk(P32_CZ),
                  const((1, W)), const((1, W)), const((GM_GROUPS, GM_CHUNK, GM_CHUNK)),
                  const((GM_CHUNK, W)), const((CONV_WIDTH, W)),
                  yblk, yblk, hblk, hblk,
                  const((d, N_BRANCH * d)), const((N_BRANCH, W, d)), const((d, d)), const((1, d))],
        out_specs=[hblk, hblk],
        out_shape=[jax.ShapeDtypeStruct((m, d), jnp.float32), jax.ShapeDtypeStruct((m, d), hn_dtype)],
        scratch_shapes=[pltpu.VMEM((tm, W), jnp.bfloat16), pltpu.VMEM((tm, W), jnp.bfloat16),
                        pltpu.VMEM((8, W), jnp.float32)],
        compiler_params=_cparams(("arbitrary",)),
        name="merge",
    )(p32, p32, p32, p32, p32, p32, p32,
      ln_g.reshape(1, W), ln_b.reshape(1, W), w_s, bs_full, conv_w,
      yb, yd, hn, h, wg, wb, wo, g_next.reshape(1, d))


def kernel(x, norm_g, w_in, gm_ln_g, gm_ln_b, gm_w_s, gm_b_s, dsa_kv_g, dsa_w_uk, dsa_w_uv,
           conv_w, fox_b_f, w_branch, w_out, final_g):
    b, s, d = x.shape
    depth = w_in.shape[0]
    m = b * s
    h = x.reshape(m, d)
    hn = _rmsnorm(h, norm_g[0], jnp.bfloat16)
    for l in range(depth):
        w16, w32, wvt, wg = _prep_w_in(w_in[l])
        p16 = _matmul(hn, w16, jnp.bfloat16, tm=512, tn=N16, name="in_proj16")
        p32 = _matmul(hn, w32, jnp.float32, tm=512, tn=N32 // 2, name="in_proj32")
        vt = _matmul_t(hn, wvt, tm=min(KC, s), name="in_proj_vt")
        p16b = p16.reshape(b, s, N16)
        p32b = p32.reshape(b, s, N32)
        qaug, kaug = _fox_cum(p32b, fox_b_f[l])
        yd = _fox(p16b, p32b, qaug, kaug, vt)
        yb = _dsa(p16b, p32b, dsa_kv_g[l], dsa_w_uk[l], dsa_w_uv[l])
        last = l == depth - 1
        g_next = final_g if last else norm_g[l + 1]
        h, hn = _merge(p32, yb.reshape(m, W), yd.reshape(m, W), hn, h,
                       gm_ln_g[l], gm_ln_b[l], gm_w_s[l], gm_b_s[l], conv_w[l],
                       wg, w_branch[l].astype(jnp.bfloat16), w_out[l].astype(jnp.bfloat16), g_next,
                       jnp.float32 if last else jnp.bfloat16, s)
    return hn.reshape(b, s, d)
```
